```python
import math
import jax, jax.numpy as jnp
from jax import lax
import numpy as np

D_MODEL = 1024
BATCH = 32
SEQ = 2048
DEPTH = 1
DEC_BATCH = 4
DEC_SEQ = 8192
PAST_LEN = 128

HEAD_DIM = 64
MIX_WIDTH = D_MODEL
ATTN_WIDTH = MIX_WIDTH // 2
GMLP_WIDTH = MIX_WIDTH - ATTN_WIDTH
N_ATTN_HEADS = ATTN_WIDTH // HEAD_DIM
N_GMLP_GROUPS = GMLP_WIDTH // HEAD_DIM
IN_PROJ_WIDTH = 3 * ATTN_WIDTH + 2 * GMLP_WIDTH
GMLP_CHUNK = 128
DILATED_BRANCHES = ((128, 1), (512, 4), (2048, 16))
ATTN_BLOCK = 64
N_REL_BUCKETS = 32
REL_MAX_DISTANCE = 1024
N_EXPERTS = 16
EC_CAPACITY_FACTOR = 2
D_FF_EXPERT = 2816
DEEPNORM_ALPHA = (2 * DEPTH) ** 0.25
DEEPNORM_BETA = (8 * DEPTH) ** -0.25
LN_EPS = 1e-5

kernel_name = 'hybrid_dilated_attn_gmlp_ec_moe_encoder'


def layer_norm(x, g, b):
    xf = x.astype(jnp.float32)
    mu = jnp.mean(xf, -1, keepdims=True)
    var = jnp.mean(jnp.square(xf - mu), -1, keepdims=True)
    y = (xf - mu) * lax.rsqrt(var + LN_EPS)
    return (y * g.astype(jnp.float32) + b.astype(jnp.float32)).astype(x.dtype)


def t5_bucket(rel):
    half = N_REL_BUCKETS // 2
    ret = np.where(rel > 0, half, 0)
    n = np.abs(rel)
    max_exact = half // 2
    large = max_exact + (np.log(np.maximum(n, 1) / max_exact) / np.log(REL_MAX_DISTANCE / max_exact) * (half - max_exact)).astype(np.int32)
    large = np.minimum(large, half - 1)
    return (ret + np.where(n < max_exact, n, large)).astype(np.int32)


def dilated_window_branch(q, k, v, rel_table, window, dilation):
    B, L, H, Dh = q.shape
    d = dilation
    W = window // 2 // d
    Lr = L // d
    blk = math.gcd(ATTN_BLOCK, Lr)
    nb = Lr // blk
    win = blk + 2 * W

    def to_residue(t):
        return t.reshape(B, Lr, d, H, Dh).transpose(0, 2, 3, 1, 4)

    qb = to_residue(q).reshape(B, d, H, nb, blk, Dh)
    pad = ((0, 0), (0, 0), (0, 0), (W, W), (0, 0))
    kp = jnp.pad(to_residue(k), pad)
    vp = jnp.pad(to_residue(v), pad)
    idx = np.arange(nb)[:, None] * blk + np.arange(win)[None, :]
    kw = kp[:, :, :, idx]
    vw = vp[:, :, :, idx]
    off = np.arange(win)[None, :] - W - np.arange(blk)[:, None]
    key_pos = idx - W
    valid = (np.abs(off) <= W)[None] & ((key_pos >= 0) & (key_pos < Lr))[:, None, :]
    bias = rel_table[t5_bucket(np.clip(off, -W, W) * d)]
    s = jnp.einsum('bdhnqc,bdhnkc->bdhnqk', qb, kw).astype(jnp.float32) * (HEAD_DIM ** -0.5)
    s = s + jnp.transpose(bias, (2, 0, 1)).astype(jnp.float32)[None, None, :, None]
    s = jnp.where(valid[None, None, None], s, -jnp.inf)
    m = jnp.max(s, -1, keepdims=True)
    p = jnp.exp(s - m)
    den = jnp.sum(p, -1, keepdims=True)
    o = jnp.einsum('bdhnqk,bdhnkc->bdhnqc', (p / den).astype(v.dtype), vw)
    lse = (m + jnp.log(den))[..., 0]
    o = o.reshape(B, d, H, Lr, Dh).transpose(0, 3, 1, 2, 4).reshape(B, L, H, Dh)
    lse = lse.reshape(B, d, H, Lr).transpose(0, 3, 1, 2).reshape(B, L, H)
    return o, lse


def dilated_attention(q, k, v, rel_table):
    outs, lses = [], []
    for window, dilation in DILATED_BRANCHES:
        o, lse = dilated_window_branch(q, k, v, rel_table, window, dilation)
        outs.append(o)
        lses.append(lse)
    w = jax.nn.softmax(jnp.stack(lses, 0), axis=0)
    return jnp.einsum('nblh,nblhc->blhc', w.astype(q.dtype), jnp.stack(outs, 0))


def gmlp_spatial_gating(u, vg, ln_g, ln_b, w_s, b_s):
    B, L, _ = u.shape
    vn = layer_norm(vg, ln_g, ln_b)
    vc = vn.reshape(B, L // GMLP_CHUNK, GMLP_CHUNK, N_GMLP_GROUPS, HEAD_DIM)
    vm = jnp.einsum('gpq,bnqgc->bnpgc', w_s, vc) + b_s.T[None, None, :, :, None]
    return u * vm.reshape(B, L, GMLP_WIDTH)


def hybrid_mixer(x, rel_table, w_in, b_in, gmlp_ln_g, gmlp_ln_b, gmlp_w_s, gmlp_b_s, w_out):
    B, L, _ = x.shape
    h = jnp.einsum('bld,de->ble', x, w_in) + b_in
    q, k, v, gu, gv = jnp.split(h, [ATTN_WIDTH, 2 * ATTN_WIDTH, 3 * ATTN_WIDTH, 3 * ATTN_WIDTH + GMLP_WIDTH], axis=-1)
    shp = (B, L, N_ATTN_HEADS, HEAD_DIM)
    attn = dilated_attention(q.reshape(shp), k.reshape(shp), v.reshape(shp), rel_table).reshape(B, L, ATTN_WIDTH)
    gm = gmlp_spatial_gating(jax.nn.gelu(gu), jax.nn.gelu(gv), gmlp_ln_g, gmlp_ln_b, gmlp_w_s, gmlp_b_s)
    return jnp.einsum('ble,ed->bld', jnp.concatenate([attn, gm], axis=-1), w_out)


def expert_choice_moe(x, w_router, w_gate, w_up, w_down):
    B, L, D = x.shape
    n = B * L
    xt = x.reshape(n, D)
    cap = EC_CAPACITY_FACTOR * n // N_EXPERTS
    aff = jax.nn.softmax((xt @ w_router).astype(jnp.float32), axis=-1)
    gates, idx = lax.top_k(aff.T, cap)
    xs = xt[idx]

    def expert(args):
        xe, wg, wu, wd = args
        return (jax.nn.silu(xe @ wg) * (xe @ wu)) @ wd

    ys = lax.map(expert, (xs, w_gate, w_up, w_down))
    ys = ys * gates[..., None].astype(ys.dtype)
    out = jnp.zeros_like(xt).at[idx.reshape(-1)].add(ys.reshape(-1, D))
    return out.reshape(B, L, D)


def encoder_trunk(x, rel_bias_table, w_in, b_in, gmlp_ln_g, gmlp_ln_b, gmlp_w_s, gmlp_b_s, w_out,
                  ln1_g, ln1_b, w_router, w_gate, w_up, w_down, ln2_g, ln2_b):
    for l in range(DEPTH):
        mix = hybrid_mixer(x, rel_bias_table, w_in[l], b_in[l], gmlp_ln_g[l], gmlp_ln_b[l],
                           gmlp_w_s[l], gmlp_b_s[l], w_out[l])
        x = layer_norm(DEEPNORM_ALPHA * x + mix, ln1_g[l], ln1_b[l])
        ffn = expert_choice_moe(x, w_router[l], w_gate[l], w_up[l], w_down[l])
        x = layer_norm(DEEPNORM_ALPHA * x + ffn, ln2_g[l], ln2_b[l])
    return x


def setup_inputs(seed: int = 0) -> dict:
    key = jax.random.key(seed)
    ks = jax.random.split(key, 20)
    f32 = jnp.float32
    nrm = lambda k, shape, s: jax.random.normal(k, shape, f32) * s
    return {
        'x_prompt': nrm(ks[0], (BATCH, SEQ, D_MODEL), 1.0),
        'x_sample': nrm(ks[1], (DEC_BATCH, DEC_SEQ, D_MODEL), 1.0),
        'rel_bias_table': nrm(ks[2], (N_REL_BUCKETS, N_ATTN_HEADS), 0.5),
        'w_in': nrm(ks[3], (DEPTH, D_MODEL, IN_PROJ_WIDTH), D_MODEL ** -0.5),
        'b_in': nrm(ks[4], (DEPTH, IN_PROJ_WIDTH), 0.02),
        'gmlp_ln_g': 1.0 + nrm(ks[5], (DEPTH, GMLP_WIDTH), 0.02),
        'gmlp_ln_b': nrm(ks[6], (DEPTH, GMLP_WIDTH), 0.02),
        'gmlp_w_s': nrm(ks[7], (DEPTH, N_GMLP_GROUPS, GMLP_CHUNK, GMLP_CHUNK), GMLP_CHUNK ** -0.5),
        'gmlp_b_s': 1.0 + nrm(ks[8], (DEPTH, N_GMLP_GROUPS, GMLP_CHUNK), 0.02),
        'w_out': nrm(ks[9], (DEPTH, MIX_WIDTH, D_MODEL), MIX_WIDTH ** -0.5 * DEEPNORM_BETA),
        'ln1_g': 1.0 + nrm(ks[10], (DEPTH, D_MODEL), 0.02),
        'ln1_b': nrm(ks[11], (DEPTH, D_MODEL), 0.02),
        'w_router': nrm(ks[12], (DEPTH, D_MODEL, N_EXPERTS), D_MODEL ** -0.5),
        'w_gate': nrm(ks[13], (DEPTH, N_EXPERTS, D_MODEL, D_FF_EXPERT), D_MODEL ** -0.5),
        'w_up': nrm(ks[14], (DEPTH, N_EXPERTS, D_MODEL, D_FF_EXPERT), D_MODEL ** -0.5),
        'w_down': nrm(ks[15], (DEPTH, N_EXPERTS, D_FF_EXPERT, D_MODEL), D_FF_EXPERT ** -0.5 * DEEPNORM_BETA),
        'ln2_g': 1.0 + nrm(ks[16], (DEPTH, D_MODEL), 0.02),
        'ln2_b': nrm(ks[17], (DEPTH, D_MODEL), 0.02),
    }


def reference(x_prompt, x_sample, rel_bias_table, w_in, b_in, gmlp_ln_g, gmlp_ln_b, gmlp_w_s, gmlp_b_s,
              w_out, ln1_g, ln1_b, w_router, w_gate, w_up, w_down, ln2_g, ln2_b):
    y_prompt = encoder_trunk(x_prompt, rel_bias_table, w_in, b_in, gmlp_ln_g, gmlp_ln_b, gmlp_w_s, gmlp_b_s,
                             w_out, ln1_g, ln1_b, w_router, w_gate, w_up, w_down, ln2_g, ln2_b)
    y_sample = encoder_trunk(x_sample, rel_bias_table, w_in, b_in, gmlp_ln_g, gmlp_ln_b, gmlp_w_s, gmlp_b_s,
                             w_out, ln1_g, ln1_b, w_router, w_gate, w_up, w_down, ln2_g, ln2_b)
    return (y_prompt, y_sample)
```

```python
import functools
import math

import numpy as np
import jax
import jax.numpy as jnp
from jax import lax
from jax.experimental import pallas as pl
from jax.experimental.pallas import tpu as pltpu

F32 = jnp.float32
BF16 = jnp.bfloat16
I32 = jnp.int32

D_MODEL = 1024
HEAD_DIM = 64
ATTN_WIDTH = 512
GMLP_WIDTH = 512
N_HEADS = ATTN_WIDTH // HEAD_DIM
N_GROUPS = GMLP_WIDTH // HEAD_DIM
IN_WIDTH = 3 * ATTN_WIDTH + 2 * GMLP_WIDTH
GMLP_CHUNK = 128
BRANCHES = ((128, 1), (512, 4), (2048, 16))
HALF_WINDOW = 64
N_BUCKETS = 32
REL_MAX_DISTANCE = 1024
N_EXPERTS = 16
CAPACITY_FACTOR = 2
D_FF = 2816
ALPHA = 2.0 ** 0.25
LN_EPS = 1e-5
MASK_VALUE = -1e30

LANES = 128
MXU_DIM = 256
VMEM_LIMIT = 56 * 1024 * 1024

Q_TILE = 128
HEAD_GROUP = 4
HG_WIDTH = HEAD_GROUP * HEAD_DIM
ROW_TILE = 512
SCAN_TILE = 256
SLOT_BLOCK = 2048
TOKEN_CHUNK = 1024
SLOT_TILE = 128
FFN_ROWS = 2048
FFN_COLS = 256


def _params(*sem):
    return pltpu.CompilerParams(dimension_semantics=sem, vmem_limit_bytes=VMEM_LIMIT)


def _layer_norm(y, g, b):
    mu = jnp.mean(y, axis=-1, keepdims=True)
    yc = y - mu
    var = jnp.mean(yc * yc, axis=-1, keepdims=True)
    return yc * lax.rsqrt(var + LN_EPS) * g + b


def _gelu_tanh(x):
    return 0.5 * x * (1.0 + jnp.tanh(math.sqrt(2.0 / math.pi) * (x + 0.044715 * (x * x * x))))


def _mixer_in_kernel(x_ref, w_ref, b_ref, lng_ref, lnb_ref, ws_ref, bs_ref,
                     q_ref, k_ref, v_ref, gm_ref):
    x = x_ref[...].astype(BF16)

    def proj(lo, hi):
        return jnp.dot(x, w_ref[:, lo:hi], preferred_element_type=F32) + b_ref[:, lo:hi]

    a = ATTN_WIDTH
    q_ref[...] = (proj(0, a) * (HEAD_DIM ** -0.5)).astype(BF16)
    k_ref[...] = proj(a, 2 * a).astype(BF16)
    v_ref[...] = proj(2 * a, 3 * a).astype(BF16)
    gu = _gelu_tanh(proj(3 * a, 3 * a + GMLP_WIDTH))
    gv = _gelu_tanh(proj(3 * a + GMLP_WIDTH, IN_WIDTH))
    vn = _layer_norm(gv, lng_ref[...], lnb_ref[...]).astype(BF16)

    rows = x_ref.shape[0]
    low_half = lax.broadcasted_iota(I32, (1, LANES), 1) < HEAD_DIM
    for c in range(rows // GMLP_CHUNK):
        r0 = c * GMLP_CHUNK
        for s in range(GMLP_WIDTH // LANES):
            c0 = s * LANES
            vs = vn[r0:r0 + GMLP_CHUNK, c0:c0 + LANES]
            zero = jnp.zeros_like(vs)
            rhs = jnp.concatenate([jnp.where(low_half, vs, zero), jnp.where(low_half, zero, vs)], axis=0)
            vm = jnp.dot(ws_ref[s], rhs, preferred_element_type=F32) + bs_ref[:, c0:c0 + LANES]
            gm_ref[r0:r0 + GMLP_CHUNK, c0:c0 + LANES] = (gu[r0:r0 + GMLP_CHUNK, c0:c0 + LANES] * vm).astype(BF16)


def _mixer_in(x, w_in, b_in, ln_g, ln_b, ws_pairs, bs_full):
    B, L, D = x.shape
    tm = min(ROW_TILE, L)
    row = lambda w: pl.BlockSpec((None, tm, w), lambda b, i: (b, i, 0))
    full = lambda shape: pl.BlockSpec(shape, lambda b, i: (0,) * len(shape))
    out = jax.ShapeDtypeStruct((B, L, ATTN_WIDTH), BF16)
    return pl.pallas_call(
        _mixer_in_kernel,
        grid=(B, L // tm),
        in_specs=[row(D), full(w_in.shape), full(b_in.shape), full(ln_g.shape), full(ln_b.shape),
                  full(ws_pairs.shape), full(bs_full.shape)],
        out_specs=[row(ATTN_WIDTH)] * 4,
        out_shape=[out] * 4,
        compiler_params=_params("parallel", "parallel"),
        name="mixer_in",
    )(x, w_in, b_in, ln_g, ln_b, ws_pairs, bs_full)


def _attn_kernel(bias_ref, q_ref, k_ref, v_ref, o_ref, lse_ref, *, lr, tk, n_tiles_total):
    qt = pl.program_id(3)
    tiles_here = q_ref.shape[0] // Q_TILE
    lane = lax.broadcasted_iota(I32, (1, HG_WIDTH), 1)

    def tile(j, carry):
        jg = qt * tiles_here + j
        q0 = pl.multiple_of(j * Q_TILE, Q_TILE)
        start = jnp.clip(jg * Q_TILE - HALF_WINDOW, 0, lr - tk)
        start = pl.multiple_of(start, HALF_WINDOW)
        variant = jnp.where(jg == 0, 0, jnp.where(jg == n_tiles_total - 1, 2, 1))
        q = q_ref[pl.ds(q0, Q_TILE), :]
        kw = k_ref[pl.ds(start, tk), :]
        vw = v_ref[pl.ds(start, tk), :]
        o_acc = jnp.zeros((Q_TILE, HG_WIDTH), F32)
        lse_acc = jnp.zeros((Q_TILE, HG_WIDTH), F32)
        for h in range(HEAD_GROUP):
            hm = (lane >= h * HEAD_DIM) & (lane < (h + 1) * HEAD_DIM)
            qh = jnp.where(hm, q, jnp.zeros_like(q))
            s = lax.dot_general(qh, kw, (((1,), (1,)), ((), ())), preferred_element_type=F32)
            s = s + bias_ref[variant, h]
            m = jnp.max(s, axis=1, keepdims=True)
            p = jnp.exp(s - m)
            l = jnp.sum(p, axis=1, keepdims=True)
            oh = jnp.dot(p.astype(BF16), vw, preferred_element_type=F32)
            o_acc = jnp.where(hm, oh * (1.0 / l), o_acc)
            lse_acc = jnp.where(hm, m + jnp.log(l), lse_acc)
        o_ref[pl.ds(q0, Q_TILE), :] = o_acc.astype(BF16)
        lse_ref[pl.ds(q0, Q_TILE), :] = lse_acc
        return carry

    lax.fori_loop(0, tiles_here, tile, 0)


def _t5_bucket(rel):
    half = N_BUCKETS // 2
    ret = np.where(rel > 0, half, 0)
    n = np.abs(rel)
    max_exact = half // 2
    large = max_exact + (np.log(np.maximum(n, 1) / max_exact) / np.log(REL_MAX_DISTANCE / max_exact)
                         * (half - max_exact)).astype(np.int32)
    large = np.minimum(large, half - 1)
    return (ret + np.where(n < max_exact, n, large)).astype(np.int32)


def _bias_tables(rel_table, dilation, tk):
    tables = []
    for delta in (0, -HALF_WINDOW, Q_TILE - tk):
        off = delta + np.arange(tk)[None, :] - np.arange(Q_TILE)[:, None]
        valid = np.abs(off) <= HALF_WINDOW
        bucket = _t5_bucket(np.clip(off, -HALF_WINDOW, HALF_WINDOW) * dilation)
        bias = jnp.transpose(rel_table[bucket], (2, 0, 1)).astype(F32)
        tables.append(jnp.where(valid[None], bias, MASK_VALUE))
    return jnp.stack(tables, 0)


def _attention_branch(q, k, v, rel_table, dilation):
    B, L, _ = q.shape
    d = dilation
    lr = L // d
    tk = min(2 * Q_TILE, lr)
    tq = min(lr, 2048)
    n_hg = N_HEADS // HEAD_GROUP
    bias = _bias_tables(rel_table, d, tk)
    view = lambda t: t.reshape(B, lr, d * ATTN_WIDTH)
    seq = pl.BlockSpec((None, lr, HG_WIDTH), lambda g, b, r, t: (b, 0, r * n_hg + g))
    qspec = pl.BlockSpec((None, tq, HG_WIDTH), lambda g, b, r, t: (b, t, r * n_hg + g))
    kern = functools.partial(_attn_kernel, lr=lr, tk=tk, n_tiles_total=lr // Q_TILE)
    o, lse = pl.pallas_call(
        kern,
        grid=(n_hg, B, d, lr // tq),
        in_specs=[pl.BlockSpec((3, HEAD_GROUP, Q_TILE, tk), lambda g, b, r, t: (0, g, 0, 0)),
                  qspec, seq, seq],
        out_specs=[qspec, qspec],
        out_shape=[jax.ShapeDtypeStruct((B, lr, d * ATTN_WIDTH), BF16),
                   jax.ShapeDtypeStruct((B, lr, d * ATTN_WIDTH), F32)],
        compiler_params=_params("parallel", "parallel", "parallel", "parallel"),
        name=f"attn_d{d}",
    )(bias, view(q), view(k), view(v))
    return o.reshape(B, L, ATTN_WIDTH), lse.reshape(B, L, ATTN_WIDTH)


def _mixer_out_kernel(x_ref, o1_ref, o2_ref, o3_ref, l1_ref, l2_ref, l3_ref, gm_ref,
                      wo_ref, g_ref, b_ref, wrh_ref, wrl_ref,
                      x1_ref, x1b_ref, aff_ref):
    l1, l2, l3 = l1_ref[...], l2_ref[...], l3_ref[...]
    mx = jnp.maximum(jnp.maximum(l1, l2), l3)
    w1, w2, w3 = jnp.exp(l1 - mx), jnp.exp(l2 - mx), jnp.exp(l3 - mx)
    num = w1 * o1_ref[...].astype(F32) + w2 * o2_ref[...].astype(F32) + w3 * o3_ref[...].astype(F32)
    attn = (num / (w1 + w2 + w3)).astype(BF16)
    mix = jnp.dot(attn, wo_ref[0:ATTN_WIDTH, :], preferred_element_type=F32)
    mix = mix + jnp.dot(gm_ref[...], wo_ref[ATTN_WIDTH:, :], preferred_element_type=F32)
    x1 = _layer_norm(ALPHA * x_ref[...] + mix, g_ref[...], b_ref[...])
    x1_ref[...] = x1
    hi = x1.astype(BF16)
    x1b_ref[...] = hi
    lo = (x1 - hi.astype(F32)).astype(BF16)
    nt = (((1,), (1,)), ((), ()))
    logits = (lax.dot_general(wrh_ref[...], hi, nt, preferred_element_type=F32)
              + lax.dot_general(wrl_ref[...], hi, nt, preferred_element_type=F32)
              + lax.dot_general(wrh_ref[...], lo, nt, preferred_element_type=F32))
    m = jnp.max(logits, axis=0, keepdims=True)
    e = jnp.exp(logits - m)
    aff_ref[...] = e / jnp.sum(e, axis=0, keepdims=True)


def _mixer_out(x, o, lse, gm, w_out, ln_g, ln_b, wr_hi, wr_lo):
    B, L, D = x.shape
    tm = min(ROW_TILE, L)
    per_seq = L // tm
    row = lambda w: pl.BlockSpec((None, tm, w), lambda b, i: (b, i, 0))
    full = lambda shape: pl.BlockSpec(shape, lambda b, i: (0,) * len(shape))
    return pl.pallas_call(
        _mixer_out_kernel,
        grid=(B, per_seq),
        in_specs=[row(D)] + [row(ATTN_WIDTH)] * 7
                 + [full(w_out.shape), full(ln_g.shape), full(ln_b.shape), full(wr_hi.shape), full(wr_lo.shape)],
        out_specs=[row(D), row(D), pl.BlockSpec((N_EXPERTS, tm), lambda b, i: (0, b * per_seq + i))],
        out_shape=[jax.ShapeDtypeStruct((B, L, D), F32), jax.ShapeDtypeStruct((B, L, D), BF16),
                   jax.ShapeDtypeStruct((N_EXPERTS, B * L), F32)],
        compiler_params=_params("parallel", "parallel"),
        name="mixer_out",
    )(x, *o, *lse, gm, w_out, ln_g, ln_b, wr_hi, wr_lo)


def _threshold_kernel(aff_ref, tau_ref, need_ref, *, cap):
    bits = lax.bitcast_convert_type(aff_ref[...], I32)

    def step(i, tau):
        cand = tau | jnp.left_shift(jnp.int32(1), 30 - i)
        cnt = jnp.sum((bits >= cand).astype(F32), axis=1, keepdims=True)
        return jnp.where(cnt >= cap, cand, tau)

    tau = lax.fori_loop(0, 31, step, jnp.zeros((bits.shape[0], 1), I32))
    above = jnp.sum((bits > tau).astype(F32), axis=1, keepdims=True)
    tau_ref[...] = jnp.broadcast_to(tau, tau_ref.shape)
    need_ref[...] = jnp.broadcast_to(cap - above, need_ref.shape)


def _slot_kernel(aff_ref, tau_ref, need_ref, tri_ref, slot_ref, pos_ref, carry_sel, carry_eq):
    @pl.when(pl.program_id(0) == 0)
    def _():
        carry_sel[...] = jnp.zeros_like(carry_sel)
        carry_eq[...] = jnp.zeros_like(carry_eq)

    tau = tau_ref[:, 0:1]
    need = need_ref[:, 0:1]
    c_sel = carry_sel[:, 0:1]
    c_eq = carry_eq[:, 0:1]
    tri = tri_ref[...]
    for s in range(aff_ref.shape[1] // SCAN_TILE):
        sl = slice(s * SCAN_TILE, (s + 1) * SCAN_TILE)
        bits = lax.bitcast_convert_type(aff_ref[:, sl], I32)
        eq = (bits == tau).astype(F32)
        eq_incl = jnp.dot(eq.astype(BF16), tri, preferred_element_type=F32)
        tie_taken = (eq_incl - eq + c_eq) < need
        sel = jnp.where((bits > tau) | ((bits == tau) & tie_taken), 1.0, 0.0)
        incl = jnp.dot(sel.astype(BF16), tri, preferred_element_type=F32)
        before = (incl - sel + c_sel).astype(I32)
        pos_ref[:, sl] = before
        slot_ref[:, sl] = jnp.where(sel > 0.0, before, -1)
        c_sel = c_sel + incl[:, SCAN_TILE - 1:SCAN_TILE]
        c_eq = c_eq + eq_incl[:, SCAN_TILE - 1:SCAN_TILE]
    carry_sel[...] = jnp.broadcast_to(c_sel, carry_sel.shape)
    carry_eq[...] = jnp.broadcast_to(c_eq, carry_eq.shape)


def _select(aff_t, cap):
    E, n = aff_t.shape
    stat = jax.ShapeDtypeStruct((E, LANES), I32)
    tau, need = pl.pallas_call(
        functools.partial(_threshold_kernel, cap=float(cap)),
        out_shape=[stat, jax.ShapeDtypeStruct((E, LANES), F32)],
        compiler_params=pltpu.CompilerParams(vmem_limit_bytes=VMEM_LIMIT),
        name="threshold",
    )(aff_t)
    tri = jnp.asarray(np.triu(np.ones((SCAN_TILE, SCAN_TILE), np.float32)), BF16)
    tb = min(SLOT_BLOCK, n)
    blk = pl.BlockSpec((E, tb), lambda i: (0, i))
    const = lambda shape: pl.BlockSpec(shape, lambda i: (0, 0))
    slot, pos = pl.pallas_call(
        _slot_kernel,
        grid=(n // tb,),
        in_specs=[blk, const((E, LANES)), const((E, LANES)), const((SCAN_TILE, SCAN_TILE))],
        out_specs=[blk, blk],
        out_shape=[jax.ShapeDtypeStruct((E, n), I32)] * 2,
        scratch_shapes=[pltpu.VMEM((E, LANES), F32), pltpu.VMEM((E, LANES), F32)],
        compiler_params=_params("arbitrary"),
        name="slots",
    )(aff_t, tau, need, tri)
    return slot, pos


def _work_lists(pos, cap, t_chunk, t_slot):
    E, n = pos.shape
    nch, ntl = n // t_chunk, cap // t_slot
    kmax = t_chunk // t_slot + 1
    base = pos[:, ::t_chunk]
    end = jnp.concatenate([base[:, 1:], jnp.full((E, 1), cap, I32)], axis=1)
    first = jnp.minimum(base // t_slot, ntl - 1)
    last = jnp.where(end > base, (end - 1) // t_slot, first)
    k = jnp.arange(kmax, dtype=I32)
    tile = first[..., None] + k
    valid = tile <= last[..., None]
    tile = jnp.minimum(tile, ntl - 1)
    e_id = jnp.broadcast_to(jnp.arange(E, dtype=I32)[:, None, None], tile.shape)
    c_id = jnp.broadcast_to(jnp.arange(nch, dtype=I32)[None, :, None], tile.shape)
    length = E * (nch + ntl)

    def compact(arrs, valid, key_of):
        flat_valid = valid.reshape(-1)
        count = jnp.sum(flat_valid.astype(I32))
        idx = jnp.nonzero(flat_valid, size=length, fill_value=0)[0].astype(I32)
        ar = jnp.arange(length, dtype=I32)
        live = ar < count
        idx = jnp.where(live, idx, idx[count - 1])
        e, c, t = (a.reshape(-1)[idx] for a in arrs)
        key = key_of(e, c, t)
        prev = jnp.concatenate([jnp.full((1,), -1, I32), key[:-1]])
        nxt = jnp.concatenate([key[1:], jnp.full((1,), -1, I32)])
        is_first = live & ((ar == 0) | (key != prev))
        is_last = live & ((ar == count - 1) | (key != nxt))
        flags = live.astype(I32) + 2 * is_first.astype(I32) + 4 * is_last.astype(I32)
        return e, c, t, flags

    dispatch = compact((e_id, c_id, tile), valid, lambda e, c, t: e * ntl + t)
    tr = lambda a: jnp.transpose(a, (1, 0, 2))
    combine = compact((tr(e_id), tr(c_id), tr(tile)), tr(valid), lambda e, c, t: c)
    return dispatch, combine


def _dispatch_kernel(we_ref, wc_ref, wt_ref, wf_ref, x_ref, slot_ref, o_ref):
    w = pl.program_id(0)
    flags = wf_ref[w]

    @pl.when((flags & 1) == 1)
    def _():
        ts, tc = o_ref.shape[0], x_ref.shape[0]
        rel = slot_ref[...] - wt_ref[w] * ts
        onehot = (lax.broadcasted_iota(I32, (ts, tc), 0) == rel).astype(BF16)
        rows = jnp.dot(onehot, x_ref[...], preferred_element_type=F32).astype(BF16)

        @pl.when((flags & 2) == 2)
        def _():
            o_ref[...] = rows

        @pl.when((flags & 2) == 0)
        def _():
            o_ref[...] += rows


def _dispatch(x1b, slot, work, cap):
    n, D = x1b.shape
    E = slot.shape[0]
    length = work[0].shape[0]
    return pl.pallas_call(
        _dispatch_kernel,
        grid_spec=pltpu.PrefetchScalarGridSpec(
            num_scalar_prefetch=4,
            grid=(length,),
            in_specs=[pl.BlockSpec((TOKEN_CHUNK, D), lambda w, we, wc, wt, wf: (wc[w], 0)),
                      pl.BlockSpec((None, 1, TOKEN_CHUNK), lambda w, we, wc, wt, wf: (we[w], 0, wc[w]))],
            out_specs=pl.BlockSpec((None, SLOT_TILE, D), lambda w, we, wc, wt, wf: (we[w], wt[w], 0)),
        ),
        out_shape=jax.ShapeDtypeStruct((E, cap, D), BF16),
        compiler_params=_params("arbitrary"),
        name="dispatch",
    )(*work, x1b, slot.reshape(E, 1, n))


def _ffn_kernel(x_ref, wg_ref, wu_ref, wd_ref, y_ref, acc_ref):
    f = pl.program_id(2)

    @pl.when(f == 0)
    def _():
        acc_ref[...] = jnp.zeros_like(acc_ref)

    x = x_ref[...]
    g = jnp.dot(x, wg_ref[...], preferred_element_type=F32)
    u = jnp.dot(x, wu_ref[...], preferred_element_type=F32)
    h = (g * jax.nn.sigmoid(g) * u).astype(BF16)
    acc_ref[...] += jnp.dot(h, wd_ref[...], preferred_element_type=F32)

    @pl.when(f == pl.num_programs(2) - 1)
    def _():
        y_ref[...] = acc_ref[...].astype(BF16)


def _experts(xs, w_gate, w_up, w_down):
    E, cap, D = xs.shape
    tm = min(FFN_ROWS, cap)
    tf = FFN_COLS
    return pl.pallas_call(
        _ffn_kernel,
        grid=(E, cap // tm, D_FF // tf),
        in_specs=[pl.BlockSpec((None, tm, D), lambda e, m, f: (e, m, 0)),
                  pl.BlockSpec((None, D, tf), lambda e, m, f: (e, 0, f)),
                  pl.BlockSpec((None, D, tf), lambda e, m, f: (e, 0, f)),
                  pl.BlockSpec((None, tf, D), lambda e, m, f: (e, f, 0))],
        out_specs=pl.BlockSpec((None, tm, D), lambda e, m, f: (e, m, 0)),
        out_shape=jax.ShapeDtypeStruct((E, cap, D), BF16),
        scratch_shapes=[pltpu.VMEM((tm, D), F32)],
        compiler_params=_params("parallel", "parallel", "arbitrary"),
        name="experts",
    )(xs, w_gate, w_up, w_down)


def _combine_kernel(we_ref, wc_ref, wt_ref, wf_ref, y_ref, slot_ref, gate_ref, x1_ref, g_ref, b_ref, o_ref):
    w = pl.program_id(0)
    flags = wf_ref[w]

    @pl.when((flags & 2) == 2)
    def _():
        o_ref[...] = jnp.zeros_like(o_ref)

    @pl.when((flags & 1) == 1)
    def _():
        ts, tc = y_ref.shape[0], o_ref.shape[0]
        e = we_ref[w]
        mine = lax.broadcasted_iota(I32, slot_ref.shape, 1) == e
        slot = jnp.max(jnp.where(mine, slot_ref[...], -1.0), axis=1, keepdims=True)
        gate = jnp.sum(jnp.where(mine, gate_ref[...], 0.0), axis=1, keepdims=True)
        rel = slot - (wt_ref[w] * ts).astype(F32)
        onehot = (lax.broadcasted_iota(I32, (tc, ts), 1).astype(F32) == rel).astype(BF16)
        o_ref[...] += gate * jnp.dot(onehot, y_ref[...], preferred_element_type=F32)

    @pl.when((flags & 4) == 4)
    def _():
        o_ref[...] = _layer_norm(ALPHA * x1_ref[...] + o_ref[...], g_ref[...], b_ref[...])


def _combine(ys, slot_tm, gate_tm, x1, ln_g, ln_b, work):
    E, cap, D = ys.shape
    n = x1.shape[0]
    length = work[0].shape[0]
    chunk = lambda w, we, wc, wt, wf: (wc[w], 0)
    const = lambda w, we, wc, wt, wf: (0, 0)
    return pl.pallas_call(
        _combine_kernel,
        grid_spec=pltpu.PrefetchScalarGridSpec(
            num_scalar_prefetch=4,
            grid=(length,),
            in_specs=[pl.BlockSpec((None, SLOT_TILE, D), lambda w, we, wc, wt, wf: (we[w], wt[w], 0)),
                      pl.BlockSpec((TOKEN_CHUNK, E), chunk),
                      pl.BlockSpec((TOKEN_CHUNK, E), chunk),
                      pl.BlockSpec((TOKEN_CHUNK, D), chunk),
                      pl.BlockSpec((1, D), const), pl.BlockSpec((1, D), const)],
            out_specs=pl.BlockSpec((TOKEN_CHUNK, D), chunk),
        ),
        out_shape=jax.ShapeDtypeStruct((n, D), F32),
        compiler_params=_params("arbitrary"),
        name="combine",
    )(*work, ys, slot_tm, gate_tm, x1, ln_g, ln_b)


def _prepare_weights(rel_bias_table, w_in, b_in, gmlp_ln_g, gmlp_ln_b, gmlp_w_s, gmlp_b_s, w_out,
                     ln1_g, ln1_b, w_router, w_gate, w_up, w_down, ln2_g, ln2_b):
    row = lambda t: t[0].reshape(1, -1).astype(F32)
    ws = gmlp_w_s[0].astype(BF16)
    ws_pairs = jnp.concatenate([ws[0::2], ws[1::2]], axis=-1)
    bs_full = jnp.repeat(gmlp_b_s[0].T, HEAD_DIM, axis=1).astype(F32)
    wr_t = w_router[0].T.astype(F32)
    wr_hi = wr_t.astype(BF16)
    wr_lo = (wr_t - wr_hi.astype(F32)).astype(BF16)
    return dict(
        rel=rel_bias_table.astype(F32), w_in=w_in[0].astype(BF16), b_in=row(b_in),
        gln_g=row(gmlp_ln_g), gln_b=row(gmlp_ln_b), ws_pairs=ws_pairs, bs_full=bs_full,
        w_out=w_out[0].astype(BF16), ln1_g=row(ln1_g), ln1_b=row(ln1_b), wr_hi=wr_hi, wr_lo=wr_lo,
        w_gate=w_gate[0].astype(BF16), w_up=w_up[0].astype(BF16), w_down=w_down[0].astype(BF16),
        ln2_g=row(ln2_g), ln2_b=row(ln2_b))


def _trunk(x, p):
    B, L, D = x.shape
    n = B * L
    cap = CAPACITY_FACTOR * n // N_EXPERTS
    q, k, v, gm = _mixer_in(x, p["w_in"], p["b_in"], p["gln_g"], p["gln_b"], p["ws_pairs"], p["bs_full"])
    outs, lses = [], []
    for _, d in BRANCHES:
        o, lse = _attention_branch(q, k, v, p["rel"], d)
        outs.append(o)
        lses.append(lse)
    x1, x1b, aff_t = _mixer_out(x, outs, lses, gm, p["w_out"], p["ln1_g"], p["ln1_b"], p["wr_hi"], p["wr_lo"])
    slot, pos = _select(aff_t, cap)
    dispatch_work, combine_work = _work_lists(pos, cap, TOKEN_CHUNK, SLOT_TILE)
    xs = _dispatch(x1b.reshape(n, D), slot, dispatch_work, cap)
    ys = _experts(xs, p["w_gate"], p["w_up"], p["w_down"])
    y = _combine(ys, slot.T.astype(F32), aff_t.T, x1.reshape(n, D), p["ln2_g"], p["ln2_b"], combine_work)
    return y.reshape(B, L, D)


def kernel(x_prompt, x_sample, rel_bias_table, w_in, b_in, gmlp_ln_g, gmlp_ln_b, gmlp_w_s, gmlp_b_s, w_out,
           ln1_g, ln1_b, w_router, w_gate, w_up, w_down, ln2_g, ln2_b):
    p = _prepare_weights(rel_bias_table, w_in, b_in, gmlp_ln_g, gmlp_ln_b, gmlp_w_s, gmlp_b_s, w_out,
                         ln1_g, ln1_b, w_router, w_gate, w_up, w_down, ln2_g, ln2_b)
    return (_trunk(x_prompt, p), _trunk(x_sample, p))
```

```python
import functools
import math

import numpy as np
import jax
import jax.numpy as jnp
from jax import lax
from jax.experimental import pallas as pl
from jax.experimental.pallas import tpu as pltpu

F32 = jnp.float32
BF16 = jnp.bfloat16
I32 = jnp.int32

D_MODEL = 1024
HEAD_DIM = 64
ATTN_WIDTH = 512
GMLP_WIDTH = 512
N_HEADS = ATTN_WIDTH // HEAD_DIM
N_GROUPS = GMLP_WIDTH // HEAD_DIM
IN_WIDTH = 3 * ATTN_WIDTH + 2 * GMLP_WIDTH
GMLP_CHUNK = 128
BRANCHES = ((128, 1), (512, 4), (2048, 16))
HALF_WINDOW = 64
N_BUCKETS = 32
REL_MAX_DISTANCE = 1024
N_EXPERTS = 16
CAPACITY_FACTOR = 2
D_FF = 2816
ALPHA = 2.0 ** 0.25
LN_EPS = 1e-5
MASK_VALUE = -1e30

LANES = 128
MXU_DIM = 256
VMEM_LIMIT = 56 * 1024 * 1024

Q_TILE = 128
HEAD_GROUP = 4
HG_WIDTH = HEAD_GROUP * HEAD_DIM
ROW_TILE = 512
SCAN_TILE = 256
SLOT_BLOCK = 2048
TOKEN_CHUNK = 1024
SLOT_TILE = 128
FFN_ROWS = 2048
FFN_COLS = 256


def _params(*sem):
    return pltpu.CompilerParams(dimension_semantics=sem, vmem_limit_bytes=VMEM_LIMIT)


def _layer_norm(y, g, b):
    mu = jnp.mean(y, axis=-1, keepdims=True)
    yc = y - mu
    var = jnp.mean(yc * yc, axis=-1, keepdims=True)
    return yc * lax.rsqrt(var + LN_EPS) * g + b


def _gelu_tanh(x):
    return 0.5 * x * (1.0 + jnp.tanh(math.sqrt(2.0 / math.pi) * (x + 0.044715 * (x * x * x))))


def _mixer_in_kernel(x_ref, w_ref, b_ref, lng_ref, lnb_ref, ws_ref, bs_ref,
                     q_ref, k_ref, v_ref, gm_ref):
    x = x_ref[...].astype(BF16)

    def proj(lo, hi):
        return jnp.dot(x, w_ref[:, lo:hi], preferred_element_type=F32) + b_ref[:, lo:hi]

    a = ATTN_WIDTH
    for s in range(a // LANES):
        q_ref[s] = proj(s * LANES, (s + 1) * LANES) * (HEAD_DIM ** -0.5)
        k_ref[s] = proj(a + s * LANES, a + (s + 1) * LANES)
        v_ref[s] = proj(2 * a + s * LANES, 2 * a + (s + 1) * LANES)
    gu =_gelu_tanh(proj(3 * a, 3 * a + GMLP_WIDTH))
    gv = _gelu_tanh(proj(3 * a + GMLP_WIDTH, IN_WIDTH))
    vn = _layer_norm(gv, lng_ref[...], lnb_ref[...]).astype(BF16)

    rows = x_ref.shape[0]
    low_half = lax.broadcasted_iota(I32, (1, LANES), 1) < HEAD_DIM
    for c in range(rows // GMLP_CHUNK):
        r0 = c * GMLP_CHUNK
        for s in range(GMLP_WIDTH // LANES):
            c0 = s * LANES
            vs = vn[r0:r0 + GMLP_CHUNK, c0:c0 + LANES]
            zero = jnp.zeros_like(vs)
            rhs = jnp.concatenate([jnp.where(low_half, vs, zero), jnp.where(low_half, zero, vs)], axis=0)
            vm = jnp.dot(ws_ref[s], rhs, preferred_element_type=F32) + bs_ref[:, c0:c0 + LANES]
            gm_ref[r0:r0 + GMLP_CHUNK, c0:c0 + LANES] = (gu[r0:r0 + GMLP_CHUNK, c0:c0 + LANES] * vm).astype(BF16)


def _mixer_in(x, w_in, b_in, ln_g, ln_b, ws_pairs, bs_full):
    B, L, D = x.shape
    tm = min(ROW_TILE, L)
    row = lambda w: pl.BlockSpec((None, tm, w), lambda b, i: (b, i, 0))
    full = lambda shape: pl.BlockSpec(shape, lambda b, i: (0,) * len(shape))
    n_slab = ATTN_WIDTH // LANES
    slab = pl.BlockSpec((None, n_slab, tm, LANES), lambda b, i: (b, 0, i, 0))
    slab_shape = jax.ShapeDtypeStruct((B, n_slab, L, LANES), F32)
    return pl.pallas_call(
        _mixer_in_kernel,
        grid=(B, L // tm),
        in_specs=[row(D), full(w_in.shape), full(b_in.shape), full(ln_g.shape), full(ln_b.shape),
                  full(ws_pairs.shape), full(bs_full.shape)],
        out_specs=[slab, slab, slab, row(GMLP_WIDTH)],
        out_shape=[slab_shape] * 3 + [jax.ShapeDtypeStruct((B, L, GMLP_WIDTH), BF16)],
        compiler_params=_params("parallel", "parallel"),
        name="mixer_in",
    )(x, w_in, b_in, ln_g, ln_b, ws_pairs, bs_full)


def _attn_kernel(b16_ref, b4_ref, b1_ref, q_ref, k_ref, v_ref, out_ref, o_acc, m_acc, l_acc, *, seq_len):
    span = out_ref.shape[0]
    span_idx = pl.program_id(2)
    n_slab = HG_WIDTH // LANES
    lane = lax.broadcasted_iota(I32, (1, HG_WIDTH), 1)
    head_masks = [(lane >= h * HEAD_DIM) & (lane < (h + 1) * HEAD_DIM) for h in range(HEAD_GROUP)]

    def rows(ref, start, size, stride):
        idx = pl.ds(start, size) if stride == 1 else pl.ds(start, size, stride=stride)
        return jnp.concatenate([ref[s, idx, :] for s in range(n_slab)], axis=1)

    def put(ref, start, stride, val):
        idx = pl.ds(start, Q_TILE) if stride == 1 else pl.ds(start, Q_TILE, stride=stride)
        for s in range(n_slab):
            ref[s, idx, :] = val[:, s * LANES:(s + 1) * LANES]

    def per_head(cols):
        out = jnp.zeros((Q_TILE, HG_WIDTH), F32)
        for h in range(HEAD_GROUP):
            out = jnp.where(head_masks[h], cols[h * Q_TILE:(h + 1) * Q_TILE], out)
        return out

    branches = ((16, b16_ref), (4, b4_ref), (1, b1_ref))
    for bi, (d, bias_ref) in enumerate(branches):
        lr = seq_len // d
        tk = min(2 * Q_TILE, lr)
        tiles_total = lr // Q_TILE
        tiles_per_residue = span // d // Q_TILE
        first, final = bi == 0, bi == len(branches) - 1

        def tile(idx, carry, d=d, bias_ref=bias_ref, lr=lr, tk=tk, tiles_total=tiles_total,
                 tiles_per_residue=tiles_per_residue, first=first, final=final):
            r = idx // tiles_per_residue
            jt = idx % tiles_per_residue
            jg = span_idx * tiles_per_residue + jt
            start = jnp.clip(jg * Q_TILE - HALF_WINDOW, 0, lr - tk)
            variant = jnp.where(jg == 0, 0, jnp.where(jg == tiles_total - 1, 2, 1))
            q_row = r + d * (jt * Q_TILE)
            q = rows(q_ref, q_row, Q_TILE, d).astype(BF16)
            kw = rows(k_ref, r + d * start, tk, d).astype(BF16)
            vw = rows(v_ref, r + d * start, tk, d).astype(BF16)
            qs = jnp.concatenate([jnp.where(hm, q, jnp.zeros_like(q)) for hm in head_masks], axis=0)
            s = lax.dot_general(qs, kw, (((1,), (1,)), ((), ())), preferred_element_type=F32)
            s = s + bias_ref[variant]
            m = jnp.max(s, axis=1, keepdims=True)
            p = jnp.exp(s - m)
            l = jnp.sum(p, axis=1, keepdims=True)
            pv = jnp.dot(p.astype(BF16), vw, preferred_element_type=F32)
            o_t = jnp.zeros((Q_TILE, HG_WIDTH), F32)
            for h in range(HEAD_GROUP):
                o_t = jnp.where(head_masks[h], pv[h * Q_TILE:(h + 1) * Q_TILE], o_t)
            m_t, l_t = per_head(m), per_head(l)
            if not first:
                m_old = rows(m_acc, q_row, Q_TILE, d)
                m_new = jnp.maximum(m_old, m_t)
                a_old, a_t = jnp.exp(m_old - m_new), jnp.exp(m_t - m_new)
                l_t = a_old * rows(l_acc, q_row, Q_TILE, d) + a_t * l_t
                o_t = a_old * rows(o_acc, q_row, Q_TILE, d) + a_t * o_t
                m_t = m_new
            if final:
                out_ref[pl.ds(pl.multiple_of(q_row, Q_TILE), Q_TILE), :] = (o_t / l_t).astype(BF16)
            else:
                put(o_acc, q_row, d, o_t)
                put(m_acc, q_row, d, m_t)
                put(l_acc, q_row, d, l_t)
            return carry

        lax.fori_loop(0, span // Q_TILE, tile, 0)


def _t5_bucket(rel):
    half = N_BUCKETS // 2
    ret = np.where(rel > 0, half, 0)
    n = np.abs(rel)
    max_exact = half // 2
    large = max_exact + (np.log(np.maximum(n, 1) / max_exact) / np.log(REL_MAX_DISTANCE / max_exact)
                         * (half - max_exact)).astype(np.int32)
    large = np.minimum(large, half - 1)
    return (ret + np.where(n < max_exact, n, large)).astype(np.int32)


def _bias_tables(rel_table, dilation, tk):
    buckets, valids = [], []
    for delta in (0, -HALF_WINDOW, Q_TILE - tk):
        off = delta + np.arange(tk)[None, :] - np.arange(Q_TILE)[:, None]
        valids.append(np.abs(off) <= HALF_WINDOW)
        buckets.append(_t5_bucket(np.clip(off, -HALF_WINDOW, HALF_WINDOW) * dilation))
    bucket = jnp.asarray(np.stack(buckets), I32)
    onehot = (bucket[..., None] == jnp.arange(N_BUCKETS, dtype=I32)).astype(F32)
    bias = jnp.einsum("vqkb,bh->hvqk", onehot, rel_table, precision=lax.Precision.HIGHEST)
    bias = jnp.where(jnp.asarray(np.stack(valids))[None], bias, MASK_VALUE)
    n_hg = N_HEADS // HEAD_GROUP
    bias = bias.reshape(n_hg, HEAD_GROUP, 3, Q_TILE, tk).transpose(0, 2, 1, 3, 4)
    return bias.reshape(n_hg, 3, HEAD_GROUP * Q_TILE, tk)


def _attention(q, k, v, rel_table):
    B, n_slab_total, L, _ = q.shape
    span = min(L, 2048)
    n_hg = N_HEADS // HEAD_GROUP
    slabs = HG_WIDTH // LANES
    biases = [_bias_tables(rel_table, d, min(2 * Q_TILE, L // d)) for d in (16, 4, 1)]
    once = pl.Buffered(1)
    bias_spec = lambda t: pl.BlockSpec((None,) + t.shape[1:], lambda g, b, s: (g, 0, 0, 0), pipeline_mode=once)
    seq = pl.BlockSpec((None, slabs, L, LANES), lambda g, b, s: (b, g, 0, 0), pipeline_mode=once)
    return pl.pallas_call(
        functools.partial(_attn_kernel, seq_len=L),
        grid=(n_hg, B, L // span),
        in_specs=[bias_spec(t) for t in biases]
                 + [pl.BlockSpec((None, slabs, span, LANES), lambda g, b, s: (b, g, s, 0)), seq, seq],
        out_specs=pl.BlockSpec((None, span, HG_WIDTH), lambda g, b, s: (b, s, g)),
        out_shape=jax.ShapeDtypeStruct((B, L, ATTN_WIDTH), BF16),
        scratch_shapes=[pltpu.VMEM((slabs, span, LANES), F32)] * 3,
        compiler_params=_params("parallel", "parallel", "arbitrary"),
        name="attention",
    )(*biases, q, k, v)


def _mixer_out_kernel(x_ref, attn_ref, gm_ref, wo_ref, g_ref, b_ref, wrh_ref, wrl_ref,
                      x1_ref, x1b_ref, aff_ref):
    mix = jnp.dot(attn_ref[...], wo_ref[0:ATTN_WIDTH, :], preferred_element_type=F32)
    mix = mix + jnp.dot(gm_ref[...], wo_ref[ATTN_WIDTH:, :], preferred_element_type=F32)
    x1 = _layer_norm(ALPHA * x_ref[...] + mix, g_ref[...], b_ref[...])
    x1_ref[...] = x1
    hi = x1.astype(BF16)
    x1b_ref[...] = hi
    lo = (x1 - hi.astype(F32)).astype(BF16)
    nt = (((1,), (1,)), ((), ()))
    logits = (lax.dot_general(wrh_ref[...], hi, nt, preferred_element_type=F32)
              + lax.dot_general(wrl_ref[...], hi, nt, preferred_element_type=F32)
              + lax.dot_general(wrh_ref[...], lo, nt, preferred_element_type=F32))
    m = jnp.max(logits, axis=0, keepdims=True)
    e = jnp.exp(logits - m)
    aff_ref[...] = e / jnp.sum(e, axis=0, keepdims=True)


def _mixer_out(x, attn, gm, w_out, ln_g, ln_b, wr_hi, wr_lo):
    B, L, D = x.shape
    tm = min(ROW_TILE, L)
    per_seq = L // tm
    row = lambda w: pl.BlockSpec((None, tm, w), lambda b, i: (b, i, 0))
    full = lambda shape: pl.BlockSpec(shape, lambda b, i: (0,) * len(shape))
    return pl.pallas_call(
        _mixer_out_kernel,
        grid=(B, per_seq),
        in_specs=[row(D), row(ATTN_WIDTH), row(GMLP_WIDTH)]
                 + [full(w_out.shape), full(ln_g.shape), full(ln_b.shape), full(wr_hi.shape), full(wr_lo.shape)],
        out_specs=[row(D), row(D), pl.BlockSpec((N_EXPERTS, tm), lambda b, i: (0, b * per_seq + i))],
        out_shape=[jax.ShapeDtypeStruct((B, L, D), F32), jax.ShapeDtypeStruct((B, L, D), BF16),
                   jax.ShapeDtypeStruct((N_EXPERTS, B * L), F32)],
        compiler_params=_params("parallel", "parallel"),
        name="mixer_out",
    )(x, attn, gm, w_out, ln_g, ln_b, wr_hi, wr_lo)


def _threshold_kernel(aff_ref, tau_ref, need_ref, *, cap):
    bits = lax.bitcast_convert_type(aff_ref[...], I32)

    def step(i, tau):
        cand = tau | jnp.left_shift(jnp.int32(1), 30 - i)
        cnt = jnp.sum((bits >= cand).astype(F32), axis=1, keepdims=True)
        return jnp.where(cnt >= cap, cand, tau)

    tau = lax.fori_loop(0, 31, step, jnp.zeros((bits.shape[0], 1), I32))
    above = jnp.sum((bits > tau).astype(F32), axis=1, keepdims=True)
    tau_ref[...] = jnp.broadcast_to(tau, tau_ref.shape)
    need_ref[...] = jnp.broadcast_to(cap - above, need_ref.shape)


def _slot_kernel(aff_ref, tau_ref, need_ref, tri_ref, slot_ref, pos_ref, carry_sel, carry_eq):
    @pl.when(pl.program_id(0) == 0)
    def _():
        carry_sel[...] = jnp.zeros_like(carry_sel)
        carry_eq[...] = jnp.zeros_like(carry_eq)

    tau = tau_ref[:, 0:1]
    need = need_ref[:, 0:1]
    c_sel = carry_sel[:, 0:1]
    c_eq = carry_eq[:, 0:1]
    tri = tri_ref[...]
    for s in range(aff_ref.shape[1] // SCAN_TILE):
        sl = slice(s * SCAN_TILE, (s + 1) * SCAN_TILE)
        bits = lax.bitcast_convert_type(aff_ref[:, sl], I32)
        eq = (bits == tau).astype(F32)
        eq_incl = jnp.dot(eq.astype(BF16), tri, preferred_element_type=F32)
        tie_taken = (eq_incl - eq + c_eq) < need
        sel = jnp.where((bits > tau) | ((bits == tau) & tie_taken), 1.0, 0.0)
        incl = jnp.dot(sel.astype(BF16), tri, preferred_element_type=F32)
        before = (incl - sel + c_sel).astype(I32)
        pos_ref[:, sl] = before
        slot_ref[:, sl] = jnp.where(sel > 0.0, before, -1)
        c_sel = c_sel + incl[:, SCAN_TILE - 1:SCAN_TILE]
        c_eq = c_eq + eq_incl[:, SCAN_TILE - 1:SCAN_TILE]
    carry_sel[...] = jnp.broadcast_to(c_sel, carry_sel.shape)
    carry_eq[...] = jnp.broadcast_to(c_eq, carry_eq.shape)


def _select(aff_t, cap):
    E, n = aff_t.shape
    stat = jax.ShapeDtypeStruct((E, LANES), I32)
    tau, need = pl.pallas_call(
        functools.partial(_threshold_kernel, cap=float(cap)),
        out_shape=[stat, jax.ShapeDtypeStruct((E, LANES), F32)],
        compiler_params=pltpu.CompilerParams(vmem_limit_bytes=VMEM_LIMIT),
        name="threshold",
    )(aff_t)
    tri = jnp.asarray(np.triu(np.ones((SCAN_TILE, SCAN_TILE), np.float32)), BF16)
    tb = min(SLOT_BLOCK, n)
    blk = pl.BlockSpec((E, tb), lambda i: (0, i))
    const = lambda shape: pl.BlockSpec(shape, lambda i: (0, 0))
    slot, pos = pl.pallas_call(
        _slot_kernel,
        grid=(n // tb,),
        in_specs=[blk, const((E, LANES)), const((E, LANES)), const((SCAN_TILE, SCAN_TILE))],
        out_specs=[blk, blk],
        out_shape=[jax.ShapeDtypeStruct((E, n), I32)] * 2,
        scratch_shapes=[pltpu.VMEM((E, LANES), F32), pltpu.VMEM((E, LANES), F32)],
        compiler_params=_params("arbitrary"),
        name="slots",
    )(aff_t, tau, need, tri)
    return slot, pos


def _work_lists(pos, cap, t_chunk, t_slot):
    E, n = pos.shape
    nch, ntl = n // t_chunk, cap // t_slot
    kmax = t_chunk // t_slot + 1
    base = pos[:, ::t_chunk]
    end = jnp.concatenate([base[:, 1:], jnp.full((E, 1), cap, I32)], axis=1)
    first = jnp.minimum(base // t_slot, ntl - 1)
    last = jnp.where(end > base, (end - 1) // t_slot, first)
    k = jnp.arange(kmax, dtype=I32)
    tile = first[..., None] + k
    valid = tile <= last[..., None]
    tile = jnp.minimum(tile, ntl - 1)
    e_id = jnp.broadcast_to(jnp.arange(E, dtype=I32)[:, None, None], tile.shape)
    c_id = jnp.broadcast_to(jnp.arange(nch, dtype=I32)[None, :, None], tile.shape)
    length = E * (nch + ntl)

    def compact(arrs, valid, key_of):
        flat_valid = valid.reshape(-1)
        count = jnp.sum(flat_valid.astype(I32))
        idx = jnp.nonzero(flat_valid, size=length, fill_value=0)[0].astype(I32)
        ar = jnp.arange(length, dtype=I32)
        live = ar < count
        idx = jnp.where(live, idx, idx[count - 1])
        e, c, t = (a.reshape(-1)[idx] for a in arrs)
        key = key_of(e, c, t)
        prev = jnp.concatenate([jnp.full((1,), -1, I32), key[:-1]])
        nxt = jnp.concatenate([key[1:], jnp.full((1,), -1, I32)])
        is_first = live & ((ar == 0) | (key != prev))
        is_last = live & ((ar == count - 1) | (key != nxt))
        flags = live.astype(I32) + 2 * is_first.astype(I32) + 4 * is_last.astype(I32)
        return e, c, t, flags

    dispatch = compact((e_id, c_id, tile), valid, lambda e, c, t: e * ntl + t)
    tr = lambda a: jnp.transpose(a, (1, 0, 2))
    combine = compact((tr(e_id), tr(c_id), tr(tile)), tr(valid), lambda e, c, t: c)
    return dispatch, combine


def _dispatch_kernel(we_ref, wc_ref, wt_ref, wf_ref, x_ref, slot_ref, o_ref):
    w = pl.program_id(0)
    flags = wf_ref[w]

    @pl.when((flags & 1) == 1)
    def _():
        ts, tc = o_ref.shape[0], x_ref.shape[0]
        rel = slot_ref[...] - wt_ref[w] * ts
        onehot = (lax.broadcasted_iota(I32, (ts, tc), 0) == rel).astype(BF16)
        rows = jnp.dot(onehot, x_ref[...], preferred_element_type=F32).astype(BF16)

        @pl.when((flags & 2) == 2)
        def _():
            o_ref[...] = rows

        @pl.when((flags & 2) == 0)
        def _():
            o_ref[...] += rows


def _dispatch(x1b, slot, work, cap):
    n, D = x1b.shape
    E = slot.shape[0]
    length = work[0].shape[0]
    return pl.pallas_call(
        _dispatch_kernel,
        grid_spec=pltpu.PrefetchScalarGridSpec(
            num_scalar_prefetch=4,
            grid=(length,),
            in_specs=[pl.BlockSpec((TOKEN_CHUNK, D), lambda w, we, wc, wt, wf: (wc[w], 0)),
                      pl.BlockSpec((None, 1, TOKEN_CHUNK), lambda w, we, wc, wt, wf: (we[w], 0, wc[w]))],
            out_specs=pl.BlockSpec((None, SLOT_TILE, D), lambda w, we, wc, wt, wf: (we[w], wt[w], 0)),
        ),
        out_shape=jax.ShapeDtypeStruct((E, cap, D), BF16),
        compiler_params=_params("arbitrary"),
        name="dispatch",
    )(*work, x1b, slot.reshape(E, 1, n))


def _ffn_kernel(x_ref, wg_ref, wu_ref, wd_ref, y_ref, acc_ref):
    f = pl.program_id(2)

    @pl.when(f == 0)
    def _():
        acc_ref[...] = jnp.zeros_like(acc_ref)

    x = x_ref[...]
    g = jnp.dot(x, wg_ref[...], preferred_element_type=F32)
    u = jnp.dot(x, wu_ref[...], preferred_element_type=F32)
    h = (g * jax.nn.sigmoid(g) * u).astype(BF16)
    acc_ref[...] += jnp.dot(h, wd_ref[...], preferred_element_type=F32)

    @pl.when(f == pl.num_programs(2) - 1)
    def _():
        y_ref[...] = acc_ref[...].astype(BF16)


def _experts(xs, w_gate, w_up, w_down):
    E, cap, D = xs.shape
    tm = min(FFN_ROWS, cap)
    tf = FFN_COLS
    return pl.pallas_call(
        _ffn_kernel,
        grid=(E, cap // tm, D_FF // tf),
        in_specs=[pl.BlockSpec((None, tm, D), lambda e, m, f: (e, m, 0)),
                  pl.BlockSpec((None, D, tf), lambda e, m, f: (e, 0, f)),
                  pl.BlockSpec((None, D, tf), lambda e, m, f: (e, 0, f)),
                  pl.BlockSpec((None, tf, D), lambda e, m, f: (e, f, 0))],
        out_specs=pl.BlockSpec((None, tm, D), lambda e, m, f: (e, m, 0)),
        out_shape=jax.ShapeDtypeStruct((E, cap, D), BF16),
        scratch_shapes=[pltpu.VMEM((tm, D), F32)],
        compiler_params=_params("parallel", "parallel", "arbitrary"),
        name="experts",
    )(xs, w_gate, w_up, w_down)


def _combine_kernel(we_ref, wc_ref, wt_ref, wf_ref, y_ref, slot_ref, gate_ref, x1_ref, g_ref, b_ref, o_ref):
    w = pl.program_id(0)
    flags = wf_ref[w]

    @pl.when((flags & 2) == 2)
    def _():
        o_ref[...] = jnp.zeros_like(o_ref)

    @pl.when((flags & 1) == 1)
    def _():
        ts, tc = y_ref.shape[0], o_ref.shape[0]
        e = we_ref[w]
        mine = lax.broadcasted_iota(I32, slot_ref.shape, 1) == e
        slot = jnp.max(jnp.where(mine, slot_ref[...], -1.0), axis=1, keepdims=True)
        gate = jnp.sum(jnp.where(mine, gate_ref[...], 0.0), axis=1, keepdims=True)
        rel = slot - (wt_ref[w] * ts).astype(F32)
        onehot = (lax.broadcasted_iota(I32, (tc, ts), 1).astype(F32) == rel).astype(BF16)
        o_ref[...] += gate * jnp.dot(onehot, y_ref[...], preferred_element_type=F32)

    @pl.when((flags & 4) == 4)
    def _():
        o_ref[...] = _layer_norm(ALPHA * x1_ref[...] + o_ref[...], g_ref[...], b_ref[...])


def _combine(ys, slot_tm, gate_tm, x1, ln_g, ln_b, work):
    E, cap, D = ys.shape
    n = x1.shape[0]
    length = work[0].shape[0]
    chunk = lambda w, we, wc, wt, wf: (wc[w], 0)
    const = lambda w, we, wc, wt, wf: (0, 0)
    return pl.pallas_call(
        _combine_kernel,
        grid_spec=pltpu.PrefetchScalarGridSpec(
            num_scalar_prefetch=4,
            grid=(length,),
            in_specs=[pl.BlockSpec((None, SLOT_TILE, D), lambda w, we, wc, wt, wf: (we[w], wt[w], 0)),
                      pl.BlockSpec((TOKEN_CHUNK, E), chunk),
                      pl.BlockSpec((TOKEN_CHUNK, E), chunk),
                      pl.BlockSpec((TOKEN_CHUNK, D), chunk),
                      pl.BlockSpec((1, D), const), pl.BlockSpec((1, D), const)],
            out_specs=pl.BlockSpec((TOKEN_CHUNK, D), chunk),
        ),
        out_shape=jax.ShapeDtypeStruct((n, D), F32),
        compiler_params=_params("arbitrary"),
        name="combine",
    )(*work, ys, slot_tm, gate_tm, x1, ln_g, ln_b)


def _prepare_weights(rel_bias_table, w_in, b_in, gmlp_ln_g, gmlp_ln_b, gmlp_w_s, gmlp_b_s, w_out,
                     ln1_g, ln1_b, w_router, w_gate, w_up, w_down, ln2_g, ln2_b):
    row = lambda t: t[0].reshape(1, -1).astype(F32)
    ws = gmlp_w_s[0].astype(BF16)
    ws_pairs = jnp.concatenate([ws[0::2], ws[1::2]], axis=-1)
    bs_full = jnp.repeat(gmlp_b_s[0].T, HEAD_DIM, axis=1).astype(F32)
    wr_t = w_router[0].T.astype(F32)
    wr_hi = wr_t.astype(BF16)
    wr_lo = (wr_t - wr_hi.astype(F32)).astype(BF16)
    return dict(
        rel=rel_bias_table.astype(F32), w_in=w_in[0].astype(BF16), b_in=row(b_in),
        gln_g=row(gmlp_ln_g), gln_b=row(gmlp_ln_b), ws_pairs=ws_pairs, bs_full=bs_full,
        w_out=w_out[0].astype(BF16), ln1_g=row(ln1_g), ln1_b=row(ln1_b), wr_hi=wr_hi, wr_lo=wr_lo,
        w_gate=w_gate[0].astype(BF16), w_up=w_up[0].astype(BF16), w_down=w_down[0].astype(BF16),
        ln2_g=row(ln2_g), ln2_b=row(ln2_b))


def _trunk(x, p):
    B, L, D = x.shape
    n = B * L
    cap = CAPACITY_FACTOR * n // N_EXPERTS
    q, k, v, gm = _mixer_in(x, p["w_in"], p["b_in"], p["gln_g"], p["gln_b"], p["ws_pairs"], p["bs_full"])
    attn = _attention(q, k, v, p["rel"])
    x1, x1b, aff_t = _mixer_out(x, attn, gm, p["w_out"], p["ln1_g"], p["ln1_b"], p["wr_hi"], p["wr_lo"])
    slot, pos = _select(aff_t, cap)
    dispatch_work, combine_work = _work_lists(pos, cap, TOKEN_CHUNK, SLOT_TILE)
    xs = _dispatch(x1b.reshape(n, D), slot, dispatch_work, cap)
    ys = _experts(xs, p["w_gate"], p["w_up"], p["w_down"])
    y = _combine(ys, slot.T.astype(F32), aff_t.T, x1.reshape(n, D), p["ln2_g"], p["ln2_b"], combine_work)
    return y.reshape(B, L, D)


def kernel(x_prompt, x_sample, rel_bias_table, w_in, b_in, gmlp_ln_g, gmlp_ln_b, gmlp_w_s, gmlp_b_s, w_out,
           ln1_g, ln1_b, w_router, w_gate, w_up, w_down, ln2_g, ln2_b):
    p = _prepare_weights(rel_bias_table, w_in, b_in, gmlp_ln_g, gmlp_ln_b, gmlp_w_s, gmlp_b_s, w_out,
                         ln1_g, ln1_b, w_router, w_gate, w_up, w_down, ln2_g, ln2_b)
    return (_trunk(x_prompt, p), _trunk(x_sample, p))
```

```python
import dataclasses
import functools
import math

import numpy as np
import jax
import jax.numpy as jnp
from jax import lax
from jax.experimental import pallas as pl
from jax.experimental.pallas import tpu as pltpu
from jax.experimental.pallas import tpu_sc as plsc

F32 = jnp.float32
BF16 = jnp.bfloat16
I32 = jnp.int32

D_MODEL = 1024
HEAD_DIM = 64
ATTN_WIDTH = 512
GMLP_WIDTH = 512
N_HEADS = ATTN_WIDTH // HEAD_DIM
N_GROUPS = GMLP_WIDTH // HEAD_DIM
IN_WIDTH = 3 * ATTN_WIDTH + 2 * GMLP_WIDTH
GMLP_CHUNK = 128
BRANCHES = ((128, 1), (512, 4), (2048, 16))
HALF_WINDOW = 64
N_BUCKETS = 32
REL_MAX_DISTANCE = 1024
N_EXPERTS = 16
CAPACITY_FACTOR = 2
D_FF = 2816
ALPHA = 2.0 ** 0.25
LN_EPS = 1e-5
MASK_VALUE = -1e30

LANES = 128
MXU_DIM = 256
VMEM_LIMIT = 56 * 1024 * 1024

Q_TILE = 128
HEAD_GROUP = 4
HG_WIDTH = HEAD_GROUP * HEAD_DIM
ROW_TILE = 512
SCAN_TILE = 256
SLOT_BLOCK = 2048
TOKEN_CHUNK = 1024
SLOT_TILE = 128
SC_SLOT_CHUNK = 4096
SC_ROWS = 64
FFN_ROWS = 2048
FFN_COLS = 256


def _params(*sem):
    return pltpu.CompilerParams(dimension_semantics=sem, vmem_limit_bytes=VMEM_LIMIT)


def _layer_norm(y, g, b):
    mu = jnp.mean(y, axis=-1, keepdims=True)
    yc = y - mu
    var = jnp.mean(yc * yc, axis=-1, keepdims=True)
    return yc * lax.rsqrt(var + LN_EPS) * g + b


def _gelu_tanh(x):
    return 0.5 * x * (1.0 + jnp.tanh(math.sqrt(2.0 / math.pi) * (x + 0.044715 * (x * x * x))))


def _mixer_in_kernel(x_ref, w_ref, b_ref, lng_ref, lnb_ref, ws_ref, bs_ref,
                     q_ref, k_ref, v_ref, gm_ref):
    x = x_ref[...].astype(BF16)

    def proj(lo, hi):
        return jnp.dot(x, w_ref[:, lo:hi], preferred_element_type=F32) + b_ref[:, lo:hi]

    a = ATTN_WIDTH
    for s in range(a // LANES):
        q_ref[s] = proj(s * LANES, (s + 1) * LANES) * (HEAD_DIM ** -0.5)
        k_ref[s] = proj(a + s * LANES, a + (s + 1) * LANES)
        v_ref[s] = proj(2 * a + s * LANES, 2 * a + (s + 1) * LANES)
    gu =_gelu_tanh(proj(3 * a, 3 * a + GMLP_WIDTH))
    gv = _gelu_tanh(proj(3 * a + GMLP_WIDTH, IN_WIDTH))
    vn = _layer_norm(gv, lng_ref[...], lnb_ref[...]).astype(BF16)

    rows = x_ref.shape[0]
    low_half = lax.broadcasted_iota(I32, (1, LANES), 1) < HEAD_DIM
    for c in range(rows // GMLP_CHUNK):
        r0 = c * GMLP_CHUNK
        for s in range(GMLP_WIDTH // LANES):
            c0 = s * LANES
            vs = vn[r0:r0 + GMLP_CHUNK, c0:c0 + LANES]
            zero = jnp.zeros_like(vs)
            rhs = jnp.concatenate([jnp.where(low_half, vs, zero), jnp.where(low_half, zero, vs)], axis=0)
            vm = jnp.dot(ws_ref[s], rhs, preferred_element_type=F32) + bs_ref[:, c0:c0 + LANES]
            gm_ref[r0:r0 + GMLP_CHUNK, c0:c0 + LANES] = (gu[r0:r0 + GMLP_CHUNK, c0:c0 + LANES] * vm).astype(BF16)


def _mixer_in(x, w_in, b_in, ln_g, ln_b, ws_pairs, bs_full):
    B, L, D = x.shape
    tm = min(ROW_TILE, L)
    row = lambda w: pl.BlockSpec((None, tm, w), lambda b, i: (b, i, 0))
    full = lambda shape: pl.BlockSpec(shape, lambda b, i: (0,) * len(shape))
    n_slab = ATTN_WIDTH // LANES
    slab = pl.BlockSpec((None, n_slab, tm, LANES), lambda b, i: (b, 0, i, 0))
    slab_shape = jax.ShapeDtypeStruct((B, n_slab, L, LANES), F32)
    return pl.pallas_call(
        _mixer_in_kernel,
        grid=(B, L // tm),
        in_specs=[row(D), full(w_in.shape), full(b_in.shape), full(ln_g.shape), full(ln_b.shape),
                  full(ws_pairs.shape), full(bs_full.shape)],
        out_specs=[slab, slab, slab, row(GMLP_WIDTH)],
        out_shape=[slab_shape] * 3 + [jax.ShapeDtypeStruct((B, L, GMLP_WIDTH), BF16)],
        compiler_params=_params("parallel", "parallel"),
        name="mixer_in",
    )(x, w_in, b_in, ln_g, ln_b, ws_pairs, bs_full)


def _attn_kernel(b16_ref, b4_ref, b1_ref, q_ref, k_ref, v_ref, out_ref, o_acc, m_acc, l_acc, *, seq_len):
    span = out_ref.shape[0]
    span_idx = pl.program_id(2)
    n_slab = HG_WIDTH // LANES
    lane = lax.broadcasted_iota(I32, (1, HG_WIDTH), 1)
    head_masks = [(lane >= h * HEAD_DIM) & (lane < (h + 1) * HEAD_DIM) for h in range(HEAD_GROUP)]

    def rows(ref, start, size, stride):
        idx = pl.ds(start, size) if stride == 1 else pl.ds(start, size, stride=stride)
        return jnp.concatenate([ref[s, idx, :] for s in range(n_slab)], axis=1)

    def put(ref, start, stride, val):
        idx = pl.ds(start, Q_TILE) if stride == 1 else pl.ds(start, Q_TILE, stride=stride)
        for s in range(n_slab):
            ref[s, idx, :] = val[:, s * LANES:(s + 1) * LANES]

    def per_head(cols):
        out = jnp.zeros((Q_TILE, HG_WIDTH), F32)
        for h in range(HEAD_GROUP):
            out = jnp.where(head_masks[h], cols[h * Q_TILE:(h + 1) * Q_TILE], out)
        return out

    branches = ((16, b16_ref), (4, b4_ref), (1, b1_ref))
    for bi, (d, bias_ref) in enumerate(branches):
        lr = seq_len // d
        tk = min(2 * Q_TILE, lr)
        tiles_total = lr // Q_TILE
        tiles_per_residue = span // d // Q_TILE
        first, final = bi == 0, bi == len(branches) - 1

        def tile(idx, carry, d=d, bias_ref=bias_ref, lr=lr, tk=tk, tiles_total=tiles_total,
                 tiles_per_residue=tiles_per_residue, first=first, final=final):
            r = idx // tiles_per_residue
            jt = idx % tiles_per_residue
            jg = span_idx * tiles_per_residue + jt
            start = jnp.clip(jg * Q_TILE - HALF_WINDOW, 0, lr - tk)
            variant = jnp.where(jg == 0, 0, jnp.where(jg == tiles_total - 1, 2, 1))
            q_row = r + d * (jt * Q_TILE)
            q = rows(q_ref, q_row, Q_TILE, d).astype(BF16)
            kw = rows(k_ref, r + d * start, tk, d).astype(BF16)
            vw = rows(v_ref, r + d * start, tk, d).astype(BF16)
            qs = jnp.concatenate([jnp.where(hm, q, jnp.zeros_like(q)) for hm in head_masks], axis=0)
            s = lax.dot_general(qs, kw, (((1,), (1,)), ((), ())), preferred_element_type=F32)
            s = s + bias_ref[variant]
            m = jnp.max(s, axis=1, keepdims=True)
            p = jnp.exp(s - m)
            l = jnp.sum(p, axis=1, keepdims=True)
            pv = jnp.dot(p.astype(BF16), vw, preferred_element_type=F32)
            o_t = jnp.zeros((Q_TILE, HG_WIDTH), F32)
            for h in range(HEAD_GROUP):
                o_t = jnp.where(head_masks[h], pv[h * Q_TILE:(h + 1) * Q_TILE], o_t)
            m_t, l_t = per_head(m), per_head(l)
            if not first:
                m_old = rows(m_acc, q_row, Q_TILE, d)
                m_new = jnp.maximum(m_old, m_t)
                a_old, a_t = jnp.exp(m_old - m_new), jnp.exp(m_t - m_new)
                l_t = a_old * rows(l_acc, q_row, Q_TILE, d) + a_t * l_t
                o_t = a_old * rows(o_acc, q_row, Q_TILE, d) + a_t * o_t
                m_t = m_new
            if final:
                out_ref[pl.ds(pl.multiple_of(q_row, Q_TILE), Q_TILE), :] = (o_t / l_t).astype(BF16)
            else:
                put(o_acc, q_row, d, o_t)
                put(m_acc, q_row, d, m_t)
                put(l_acc, q_row, d, l_t)
            return carry

        lax.fori_loop(0, span // Q_TILE, tile, 0)


def _t5_bucket(rel):
    half = N_BUCKETS // 2
    ret = np.where(rel > 0, half, 0)
    n = np.abs(rel)
    max_exact = half // 2
    large = max_exact + (np.log(np.maximum(n, 1) / max_exact) / np.log(REL_MAX_DISTANCE / max_exact)
                         * (half - max_exact)).astype(np.int32)
    large = np.minimum(large, half - 1)
    return (ret + np.where(n < max_exact, n, large)).astype(np.int32)


def _bias_tables(rel_table, dilation, tk):
    buckets, valids = [], []
    for delta in (0, -HALF_WINDOW, Q_TILE - tk):
        off = delta + np.arange(tk)[None, :] - np.arange(Q_TILE)[:, None]
        valids.append(np.abs(off) <= HALF_WINDOW)
        buckets.append(_t5_bucket(np.clip(off, -HALF_WINDOW, HALF_WINDOW) * dilation))
    bucket = jnp.asarray(np.stack(buckets), I32)
    onehot = (bucket[..., None] == jnp.arange(N_BUCKETS, dtype=I32)).astype(F32)
    bias = jnp.einsum("vqkb,bh->hvqk", onehot, rel_table, precision=lax.Precision.HIGHEST)
    bias = jnp.where(jnp.asarray(np.stack(valids))[None], bias, MASK_VALUE)
    n_hg = N_HEADS // HEAD_GROUP
    bias = bias.reshape(n_hg, HEAD_GROUP, 3, Q_TILE, tk).transpose(0, 2, 1, 3, 4)
    return bias.reshape(n_hg, 3, HEAD_GROUP * Q_TILE, tk)


def _attention(q, k, v, rel_table):
    B, n_slab_total, L, _ = q.shape
    span = min(L, 2048)
    n_hg = N_HEADS // HEAD_GROUP
    slabs = HG_WIDTH // LANES
    biases = [_bias_tables(rel_table, d, min(2 * Q_TILE, L // d)) for d in (16, 4, 1)]
    once = pl.Buffered(1)
    bias_spec = lambda t: pl.BlockSpec((None,) + t.shape[1:], lambda g, b, s: (g, 0, 0, 0), pipeline_mode=once)
    seq = pl.BlockSpec((None, slabs, L, LANES), lambda g, b, s: (b, g, 0, 0), pipeline_mode=once)
    return pl.pallas_call(
        functools.partial(_attn_kernel, seq_len=L),
        grid=(n_hg, B, L // span),
        in_specs=[bias_spec(t) for t in biases]
                 + [pl.BlockSpec((None, slabs, span, LANES), lambda g, b, s: (b, g, s, 0)), seq, seq],
        out_specs=pl.BlockSpec((None, span, HG_WIDTH), lambda g, b, s: (b, s, g)),
        out_shape=jax.ShapeDtypeStruct((B, L, ATTN_WIDTH), BF16),
        scratch_shapes=[pltpu.VMEM((slabs, span, LANES), F32)] * 3,
        compiler_params=_params("parallel", "parallel", "arbitrary"),
        name="attention",
    )(*biases, q, k, v)


def _mixer_out_kernel(x_ref, attn_ref, gm_ref, wo_ref, g_ref, b_ref, wrh_ref, wrl_ref,
                      x1_ref, x1p_ref, aff_ref):
    mix = jnp.dot(attn_ref[...], wo_ref[0:ATTN_WIDTH, :], preferred_element_type=F32)
    mix = mix + jnp.dot(gm_ref[...], wo_ref[ATTN_WIDTH:, :], preferred_element_type=F32)
    x1 = _layer_norm(ALPHA * x_ref[...] + mix, g_ref[...], b_ref[...])
    x1_ref[...] = x1
    hi = x1.astype(BF16)
    hi_f32 = hi.astype(F32)
    bits = lax.bitcast_convert_type(hi_f32, I32)
    half = bits.shape[1] // 2
    x1p_ref[...] = lax.shift_right_logical(bits[:, :half], 16) | bits[:, half:]
    lo = (x1 - hi_f32).astype(BF16)
    nt = (((1,), (1,)), ((), ()))
    logits = (lax.dot_general(wrh_ref[...], hi, nt, preferred_element_type=F32)
              + lax.dot_general(wrl_ref[...], hi, nt, preferred_element_type=F32)
              + lax.dot_general(wrh_ref[...], lo, nt, preferred_element_type=F32))
    m = jnp.max(logits, axis=0, keepdims=True)
    e = jnp.exp(logits - m)
    aff_ref[...] = e / jnp.sum(e, axis=0, keepdims=True)


def _mixer_out(x, attn, gm, w_out, ln_g, ln_b, wr_hi, wr_lo):
    B, L, D = x.shape
    tm = min(ROW_TILE, L)
    per_seq = L // tm
    row = lambda w: pl.BlockSpec((None, tm, w), lambda b, i: (b, i, 0))
    full = lambda shape: pl.BlockSpec(shape, lambda b, i: (0,) * len(shape))
    return pl.pallas_call(
        _mixer_out_kernel,
        grid=(B, per_seq),
        in_specs=[row(D), row(ATTN_WIDTH), row(GMLP_WIDTH)]
                 + [full(w_out.shape), full(ln_g.shape), full(ln_b.shape), full(wr_hi.shape), full(wr_lo.shape)],
        out_specs=[row(D), row(D // 2), pl.BlockSpec((N_EXPERTS, tm), lambda b, i: (0, b * per_seq + i))],
        out_shape=[jax.ShapeDtypeStruct((B, L, D), F32), jax.ShapeDtypeStruct((B, L, D // 2), I32),
                   jax.ShapeDtypeStruct((N_EXPERTS, B * L), F32)],
        compiler_params=_params("parallel", "parallel"),
        name="mixer_out",
    )(x, attn, gm, w_out, ln_g, ln_b, wr_hi, wr_lo)


def _threshold_kernel(aff_ref, tau_ref, need_ref, *, cap):
    bits = lax.bitcast_convert_type(aff_ref[...], I32)

    def step(i, tau):
        cand = tau | jnp.left_shift(jnp.int32(1), 30 - i)
        cnt = jnp.sum((bits >= cand).astype(F32), axis=1, keepdims=True)
        return jnp.where(cnt >= cap, cand, tau)

    tau = lax.fori_loop(0, 31, step, jnp.zeros((bits.shape[0], 1), I32))
    above = jnp.sum((bits > tau).astype(F32), axis=1, keepdims=True)
    tau_ref[...] = jnp.broadcast_to(tau, tau_ref.shape)
    need_ref[...] = jnp.broadcast_to(cap - above, need_ref.shape)


def _slot_kernel(aff_ref, tau_ref, need_ref, tri_ref, slot_ref, pos_ref, carry_sel, carry_eq):
    @pl.when(pl.program_id(0) == 0)
    def _():
        carry_sel[...] = jnp.zeros_like(carry_sel)
        carry_eq[...] = jnp.zeros_like(carry_eq)

    tau = tau_ref[:, 0:1]
    need = need_ref[:, 0:1]
    c_sel = carry_sel[:, 0:1]
    c_eq = carry_eq[:, 0:1]
    tri = tri_ref[...]
    for s in range(aff_ref.shape[1] // SCAN_TILE):
        sl = slice(s * SCAN_TILE, (s + 1) * SCAN_TILE)
        bits = lax.bitcast_convert_type(aff_ref[:, sl], I32)
        eq = (bits == tau).astype(F32)
        eq_incl = jnp.dot(eq.astype(BF16), tri, preferred_element_type=F32)
        tie_taken = (eq_incl - eq + c_eq) < need
        sel = jnp.where((bits > tau) | ((bits == tau) & tie_taken), 1.0, 0.0)
        incl = jnp.dot(sel.astype(BF16), tri, preferred_element_type=F32)
        before = (incl - sel + c_sel).astype(I32)
        pos_ref[:, sl] = before
        slot_ref[:, sl] = jnp.where(sel > 0.0, before, -1)
        c_sel = c_sel + incl[:, SCAN_TILE - 1:SCAN_TILE]
        c_eq = c_eq + eq_incl[:, SCAN_TILE - 1:SCAN_TILE]
    carry_sel[...] = jnp.broadcast_to(c_sel, carry_sel.shape)
    carry_eq[...] = jnp.broadcast_to(c_eq, carry_eq.shape)


def _select(aff_t, cap):
    E, n = aff_t.shape
    stat = jax.ShapeDtypeStruct((E, LANES), I32)
    tau, need = pl.pallas_call(
        functools.partial(_threshold_kernel, cap=float(cap)),
        out_shape=[stat, jax.ShapeDtypeStruct((E, LANES), F32)],
        compiler_params=pltpu.CompilerParams(vmem_limit_bytes=VMEM_LIMIT),
        name="threshold",
    )(aff_t)
    tri = jnp.asarray(np.triu(np.ones((SCAN_TILE, SCAN_TILE), np.float32)), BF16)
    tb = min(SLOT_BLOCK, n)
    blk = pl.BlockSpec((E, tb), lambda i: (0, i))
    const = lambda shape: pl.BlockSpec(shape, lambda i: (0, 0))
    slot, pos = pl.pallas_call(
        _slot_kernel,
        grid=(n // tb,),
        in_specs=[blk, const((E, LANES)), const((E, LANES)), const((SCAN_TILE, SCAN_TILE))],
        out_specs=[blk, blk],
        out_shape=[jax.ShapeDtypeStruct((E, n), I32)] * 2,
        scratch_shapes=[pltpu.VMEM((E, LANES), F32), pltpu.VMEM((E, LANES), F32)],
        compiler_params=_params("arbitrary"),
        name="slots",
    )(aff_t, tau, need, tri)
    return slot, pos


def _work_lists(pos, cap, t_chunk, t_slot):
    E, n = pos.shape
    nch, ntl = n // t_chunk, cap // t_slot
    kmax = t_chunk // t_slot + 1
    base = pos[:, ::t_chunk]
    end = jnp.concatenate([base[:, 1:], jnp.full((E, 1), cap, I32)], axis=1)
    first = jnp.minimum(base // t_slot, ntl - 1)
    last = jnp.where(end > base, (end - 1) // t_slot, first)
    k = jnp.arange(kmax, dtype=I32)
    tile = first[..., None] + k
    valid = tile <= last[..., None]
    tile = jnp.minimum(tile, ntl - 1)
    e_id = jnp.broadcast_to(jnp.arange(E, dtype=I32)[:, None, None], tile.shape)
    c_id = jnp.broadcast_to(jnp.arange(nch, dtype=I32)[None, :, None], tile.shape)
    length = E * (nch + ntl)

    def compact(arrs, valid, key_of):
        flat_valid = valid.reshape(-1)
        count = jnp.sum(flat_valid.astype(I32))
        idx = jnp.nonzero(flat_valid, size=length, fill_value=0)[0].astype(I32)
        ar = jnp.arange(length, dtype=I32)
        live = ar < count
        idx = jnp.where(live, idx, idx[count - 1])
        e, c, t = (a.reshape(-1)[idx] for a in arrs)
        key = key_of(e, c, t)
        prev = jnp.concatenate([jnp.full((1,), -1, I32), key[:-1]])
        nxt = jnp.concatenate([key[1:], jnp.full((1,), -1, I32)])
        is_first = live & ((ar == 0) | (key != prev))
        is_last = live & ((ar == count - 1) | (key != nxt))
        flags = live.astype(I32) + 2 * is_first.astype(I32) + 4 * is_last.astype(I32)
        return e, c, t, flags

    dispatch = compact((e_id, c_id, tile), valid, lambda e, c, t: e * ntl + t)
    tr = lambda a: jnp.transpose(a, (1, 0, 2))
    combine = compact((tr(e_id), tr(c_id), tr(tile)), tr(valid), lambda e, c, t: c)
    return dispatch, combine


def _dispatch(x1p, slot, cap):
    n, width = x1p.shape
    E = slot.shape[0]
    info = plsc.get_sparse_core_info()
    n_cores, lanes = info.num_cores, info.num_lanes
    workers = n_cores * info.num_subcores
    per_expert = workers // E
    share = cap // per_expert
    chunk = min(n, SC_SLOT_CHUNK)
    mesh = plsc.VectorSubcoreMesh(core_axis_name="c", subcore_axis_name="s")

    @functools.partial(
        pl.kernel, mesh=mesh,
        out_type=jax.ShapeDtypeStruct((E * cap, width), I32),
        compiler_params=dataclasses.replace(pltpu.CompilerParams(), needs_layout_passes=False),
        scratch_types=[pltpu.VMEM((chunk,), I32), pltpu.VMEM((cap,), I32),
                       pltpu.VMEM((SC_ROWS, width), I32), pltpu.SemaphoreType.DMA],
    )
    def gather(x_hbm, slot_hbm, out_hbm, slot_v, idx_v, rows_v, sem):
        wid = lax.axis_index("s") * n_cores + lax.axis_index("c")
        e = wid // per_expert
        part = wid % per_expert

        @pl.loop(0, n // chunk)
        def _(ci):
            pltpu.sync_copy(slot_hbm.at[e, pl.ds(ci * chunk, chunk)], slot_v)

            @pl.loop(0, chunk // lanes)
            def _(i):
                sv = slot_v[pl.ds(i * lanes, lanes)]
                tok = ci * chunk + i * lanes + lax.iota(I32, lanes)
                plsc.store_scatter(idx_v, [sv], tok, mask=sv >= 0)

        @pl.loop(0, share // SC_ROWS)
        def _(j):
            off = part * share + j * SC_ROWS
            pltpu.async_copy(x_hbm.at[idx_v.at[pl.ds(off, SC_ROWS)]], rows_v, sem).wait()
            pltpu.sync_copy(rows_v, out_hbm.at[pl.ds(e * cap + off, SC_ROWS)])

    return gather(x1p, slot).reshape(E, cap, width)


def _ffn_kernel(x_ref, wg_ref, wu_ref, wd_ref, y_ref, acc_ref, xb_ref):
    f = pl.program_id(2)

    @pl.when(f == 0)
    def _():
        acc_ref[...] = jnp.zeros_like(acc_ref)
        w = x_ref[...]
        left = lax.bitcast_convert_type(lax.shift_left(w, 16), F32)
        right = lax.bitcast_convert_type(w & jnp.int32(-65536), F32)
        xb_ref[...] = jnp.concatenate([left, right], axis=1).astype(BF16)

    x = xb_ref[...]
    g = jnp.dot(x, wg_ref[...], preferred_element_type=F32)
    u = jnp.dot(x, wu_ref[...], preferred_element_type=F32)
    h = (g * jax.nn.sigmoid(g) * u).astype(BF16)
    acc_ref[...] += jnp.dot(h, wd_ref[...], preferred_element_type=F32)

    @pl.when(f == pl.num_programs(2) - 1)
    def _():
        y_ref[...] = acc_ref[...].astype(BF16)


def _experts(xs, w_gate, w_up, w_down):
    E, cap, packed = xs.shape
    D = 2 * packed
    tm = min(FFN_ROWS, cap)
    tf = FFN_COLS
    return pl.pallas_call(
        _ffn_kernel,
        grid=(E, cap // tm, D_FF // tf),
        in_specs=[pl.BlockSpec((None, tm, packed), lambda e, m, f: (e, m, 0)),
                  pl.BlockSpec((None, D, tf), lambda e, m, f: (e, 0, f)),
                  pl.BlockSpec((None, D, tf), lambda e, m, f: (e, 0, f)),
                  pl.BlockSpec((None, tf, D), lambda e, m, f: (e, f, 0))],
        out_specs=pl.BlockSpec((None, tm, D), lambda e, m, f: (e, m, 0)),
        out_shape=jax.ShapeDtypeStruct((E, cap, D), BF16),
        scratch_shapes=[pltpu.VMEM((tm, D), F32), pltpu.VMEM((tm, D), BF16)],
        compiler_params=_params("parallel", "parallel", "arbitrary"),
        name="experts",
    )(xs, w_gate, w_up, w_down)


def _combine_kernel(we_ref, wc_ref, wt_ref, wf_ref, y_ref, slot_ref, gate_ref, x1_ref, g_ref, b_ref, o_ref):
    w = pl.program_id(0)
    flags = wf_ref[w]

    @pl.when((flags & 2) == 2)
    def _():
        o_ref[...] = jnp.zeros_like(o_ref)

    @pl.when((flags & 1) == 1)
    def _():
        ts, tc = y_ref.shape[0], o_ref.shape[0]
        e = we_ref[w]
        mine = lax.broadcasted_iota(I32, slot_ref.shape, 1) == e
        slot = jnp.max(jnp.where(mine, slot_ref[...], -1.0), axis=1, keepdims=True)
        gate = jnp.sum(jnp.where(mine, gate_ref[...], 0.0), axis=1, keepdims=True)
        rel = slot - (wt_ref[w] * ts).astype(F32)
        onehot = (lax.broadcasted_iota(I32, (tc, ts), 1).astype(F32) == rel).astype(BF16)
        o_ref[...] += gate * jnp.dot(onehot, y_ref[...], preferred_element_type=F32)

    @pl.when((flags & 4) == 4)
    def _():
        o_ref[...] = _layer_norm(ALPHA * x1_ref[...] + o_ref[...], g_ref[...], b_ref[...])


def _combine(ys, slot_tm, gate_tm, x1, ln_g, ln_b, work):
    E, cap, D = ys.shape
    n = x1.shape[0]
    length = work[0].shape[0]
    chunk = lambda w, we, wc, wt, wf: (wc[w], 0)
    const = lambda w, we, wc, wt, wf: (0, 0)
    return pl.pallas_call(
        _combine_kernel,
        grid_spec=pltpu.PrefetchScalarGridSpec(
            num_scalar_prefetch=4,
            grid=(length,),
            in_specs=[pl.BlockSpec((None, SLOT_TILE, D), lambda w, we, wc, wt, wf: (we[w], wt[w], 0)),
                      pl.BlockSpec((TOKEN_CHUNK, E), chunk),
                      pl.BlockSpec((TOKEN_CHUNK, E), chunk),
                      pl.BlockSpec((TOKEN_CHUNK, D), chunk),
                      pl.BlockSpec((1, D), const), pl.BlockSpec((1, D), const)],
            out_specs=pl.BlockSpec((TOKEN_CHUNK, D), chunk),
        ),
        out_shape=jax.ShapeDtypeStruct((n, D), F32),
        compiler_params=_params("arbitrary"),
        name="combine",
    )(*work, ys, slot_tm, gate_tm, x1, ln_g, ln_b)


def _prepare_weights(rel_bias_table, w_in, b_in, gmlp_ln_g, gmlp_ln_b, gmlp_w_s, gmlp_b_s, w_out,
                     ln1_g, ln1_b, w_router, w_gate, w_up, w_down, ln2_g, ln2_b):
    row = lambda t: t[0].reshape(1, -1).astype(F32)
    ws = gmlp_w_s[0].astype(BF16)
    ws_pairs = jnp.concatenate([ws[0::2], ws[1::2]], axis=-1)
    bs_full = jnp.repeat(gmlp_b_s[0].T, HEAD_DIM, axis=1).astype(F32)
    wr_t = w_router[0].T.astype(F32)
    wr_hi = wr_t.astype(BF16)
    wr_lo = (wr_t - wr_hi.astype(F32)).astype(BF16)
    return dict(
        rel=rel_bias_table.astype(F32), w_in=w_in[0].astype(BF16), b_in=row(b_in),
        gln_g=row(gmlp_ln_g), gln_b=row(gmlp_ln_b), ws_pairs=ws_pairs, bs_full=bs_full,
        w_out=w_out[0].astype(BF16), ln1_g=row(ln1_g), ln1_b=row(ln1_b), wr_hi=wr_hi, wr_lo=wr_lo,
        w_gate=w_gate[0].astype(BF16), w_up=w_up[0].astype(BF16), w_down=w_down[0].astype(BF16),
        ln2_g=row(ln2_g), ln2_b=row(ln2_b))


def _trunk(x, p):
    B, L, D = x.shape
    n = B * L
    cap = CAPACITY_FACTOR * n // N_EXPERTS
    q, k, v, gm = _mixer_in(x, p["w_in"], p["b_in"], p["gln_g"], p["gln_b"], p["ws_pairs"], p["bs_full"])
    attn = _attention(q, k, v, p["rel"])
    x1, x1p, aff_t = _mixer_out(x, attn, gm, p["w_out"], p["ln1_g"], p["ln1_b"], p["wr_hi"], p["wr_lo"])
    slot, pos = _select(aff_t, cap)
    _, combine_work = _work_lists(pos, cap, TOKEN_CHUNK, SLOT_TILE)
    xs = _dispatch(x1p.reshape(n, D // 2), slot, cap)
    ys = _experts(xs, p["w_gate"], p["w_up"], p["w_down"])
    y = _combine(ys, slot.T.astype(F32), aff_t.T, x1.reshape(n, D), p["ln2_g"], p["ln2_b"], combine_work)
    return y.reshape(B, L, D)


def kernel(x_prompt, x_sample, rel_bias_table, w_in, b_in, gmlp_ln_g, gmlp_ln_b, gmlp_w_s, gmlp_b_s, w_out,
           ln1_g, ln1_b, w_router, w_gate, w_up, w_down, ln2_g, ln2_b):
    p = _prepare_weights(rel_bias_table, w_in, b_in, gmlp_ln_g, gmlp_ln_b, gmlp_w_s, gmlp_b_s, w_out,
                         ln1_g, ln1_b, w_router, w_gate, w_up, w_down, ln2_g, ln2_b)
    return (_trunk(x_prompt, p), _trunk(x_sample, p))
```

```python
import dataclasses
import functools
import math

import numpy as np
import jax
import jax.numpy as jnp
from jax import lax
from jax.experimental import pallas as pl
from jax.experimental.pallas import tpu as pltpu
from jax.experimental.pallas import tpu_sc as plsc

F32 = jnp.float32
BF16 = jnp.bfloat16
I32 = jnp.int32

D_MODEL = 1024
HEAD_DIM = 64
ATTN_WIDTH = 512
GMLP_WIDTH = 512
N_HEADS = ATTN_WIDTH // HEAD_DIM
N_GROUPS = GMLP_WIDTH // HEAD_DIM
IN_WIDTH = 3 * ATTN_WIDTH + 2 * GMLP_WIDTH
GMLP_CHUNK = 128
BRANCHES = ((128, 1), (512, 4), (2048, 16))
HALF_WINDOW = 64
N_BUCKETS = 32
REL_MAX_DISTANCE = 1024
N_EXPERTS = 16
CAPACITY_FACTOR = 2
D_FF = 2816
ALPHA = 2.0 ** 0.25
LN_EPS = 1e-5
MASK_VALUE = -1e30

LANES = 128
MXU_DIM = 256
VMEM_LIMIT = 56 * 1024 * 1024

Q_TILE = 128
HEAD_GROUP = 4
HG_WIDTH = HEAD_GROUP * HEAD_DIM
ROW_TILE = 512
SCAN_TILE = 256
SLOT_BLOCK = 2048
SC_SLOT_CHUNK = 4096
SC_ROWS = 64
FFN_ROWS = 2048
FFN_COLS = 256


def _params(*sem):
    return pltpu.CompilerParams(dimension_semantics=sem, vmem_limit_bytes=VMEM_LIMIT)


def _layer_norm(y, g, b):
    mu = jnp.mean(y, axis=-1, keepdims=True)
    yc = y - mu
    var = jnp.mean(yc * yc, axis=-1, keepdims=True)
    return yc * lax.rsqrt(var + LN_EPS) * g + b


def _gelu_tanh(x):
    return 0.5 * x * (1.0 + jnp.tanh(math.sqrt(2.0 / math.pi) * (x + 0.044715 * (x * x * x))))


def _mixer_in_kernel(x_ref, w_ref, b_ref, lng_ref, lnb_ref, ws_ref, bs_ref,
                     q_ref, k_ref, v_ref, gm_ref):
    x = x_ref[...].astype(BF16)

    def proj(lo, hi):
        return jnp.dot(x, w_ref[:, lo:hi], preferred_element_type=F32) + b_ref[:, lo:hi]

    a = ATTN_WIDTH
    for s in range(a // LANES):
        q_ref[s] = proj(s * LANES, (s + 1) * LANES) * (HEAD_DIM ** -0.5)
        k_ref[s] = proj(a + s * LANES, a + (s + 1) * LANES)
        v_ref[s] = proj(2 * a + s * LANES, 2 * a + (s + 1) * LANES)
    gu =_gelu_tanh(proj(3 * a, 3 * a + GMLP_WIDTH))
    gv = _gelu_tanh(proj(3 * a + GMLP_WIDTH, IN_WIDTH))
    vn = _layer_norm(gv, lng_ref[...], lnb_ref[...]).astype(BF16)

    rows = x_ref.shape[0]
    low_half = lax.broadcasted_iota(I32, (1, LANES), 1) < HEAD_DIM
    for c in range(rows // GMLP_CHUNK):
        r0 = c * GMLP_CHUNK
        for s in range(GMLP_WIDTH // LANES):
            c0 = s * LANES
            vs = vn[r0:r0 + GMLP_CHUNK, c0:c0 + LANES]
            zero = jnp.zeros_like(vs)
            rhs = jnp.concatenate([jnp.where(low_half, vs, zero), jnp.where(low_half, zero, vs)], axis=0)
            vm = jnp.dot(ws_ref[s], rhs, preferred_element_type=F32) + bs_ref[:, c0:c0 + LANES]
            gm_ref[r0:r0 + GMLP_CHUNK, c0:c0 + LANES] = (gu[r0:r0 + GMLP_CHUNK, c0:c0 + LANES] * vm).astype(BF16)


def _mixer_in(x, w_in, b_in, ln_g, ln_b, ws_pairs, bs_full):
    B, L, D = x.shape
    tm = min(ROW_TILE, L)
    row = lambda w: pl.BlockSpec((None, tm, w), lambda b, i: (b, i, 0))
    full = lambda shape: pl.BlockSpec(shape, lambda b, i: (0,) * len(shape))
    n_slab = ATTN_WIDTH // LANES
    slab = pl.BlockSpec((None, n_slab, tm, LANES), lambda b, i: (b, 0, i, 0))
    slab_shape = jax.ShapeDtypeStruct((B, n_slab, L, LANES), F32)
    return pl.pallas_call(
        _mixer_in_kernel,
        grid=(B, L // tm),
        in_specs=[row(D), full(w_in.shape), full(b_in.shape), full(ln_g.shape), full(ln_b.shape),
                  full(ws_pairs.shape), full(bs_full.shape)],
        out_specs=[slab, slab, slab, row(GMLP_WIDTH)],
        out_shape=[slab_shape] * 3 + [jax.ShapeDtypeStruct((B, L, GMLP_WIDTH), BF16)],
        compiler_params=_params("parallel", "parallel"),
        name="mixer_in",
    )(x, w_in, b_in, ln_g, ln_b, ws_pairs, bs_full)


def _attn_kernel(b16_ref, b4_ref, b1_ref, q_ref, k_ref, v_ref, out_ref, o_acc, m_acc, l_acc, *, seq_len):
    span = out_ref.shape[0]
    span_idx = pl.program_id(2)
    n_slab = HG_WIDTH // LANES
    lane = lax.broadcasted_iota(I32, (1, HG_WIDTH), 1)
    head_masks = [(lane >= h * HEAD_DIM) & (lane < (h + 1) * HEAD_DIM) for h in range(HEAD_GROUP)]

    def rows(ref, start, size, stride):
        idx = pl.ds(start, size) if stride == 1 else pl.ds(start, size, stride=stride)
        return jnp.concatenate([ref[s, idx, :] for s in range(n_slab)], axis=1)

    def put(ref, start, stride, val):
        idx = pl.ds(start, Q_TILE) if stride == 1 else pl.ds(start, Q_TILE, stride=stride)
        for s in range(n_slab):
            ref[s, idx, :] = val[:, s * LANES:(s + 1) * LANES]

    def per_head(cols):
        out = jnp.zeros((Q_TILE, HG_WIDTH), F32)
        for h in range(HEAD_GROUP):
            out = jnp.where(head_masks[h], cols[h * Q_TILE:(h + 1) * Q_TILE], out)
        return out

    branches = ((16, b16_ref), (4, b4_ref), (1, b1_ref))
    for bi, (d, bias_ref) in enumerate(branches):
        lr = seq_len // d
        tk = min(2 * Q_TILE, lr)
        tiles_total = lr // Q_TILE
        tiles_per_residue = span // d // Q_TILE
        first, final = bi == 0, bi == len(branches) - 1

        def tile(idx, carry, d=d, bias_ref=bias_ref, lr=lr, tk=tk, tiles_total=tiles_total,
                 tiles_per_residue=tiles_per_residue, first=first, final=final):
            r = idx // tiles_per_residue
            jt = idx % tiles_per_residue
            jg = span_idx * tiles_per_residue + jt
            start = jnp.clip(jg * Q_TILE - HALF_WINDOW, 0, lr - tk)
            variant = jnp.where(jg == 0, 0, jnp.where(jg == tiles_total - 1, 2, 1))
            q_row = r + d * (jt * Q_TILE)
            q = rows(q_ref, q_row, Q_TILE, d).astype(BF16)
            kw = rows(k_ref, r + d * start, tk, d).astype(BF16)
            vw = rows(v_ref, r + d * start, tk, d).astype(BF16)
            qs = jnp.concatenate([jnp.where(hm, q, jnp.zeros_like(q)) for hm in head_masks], axis=0)
            s = lax.dot_general(qs, kw, (((1,), (1,)), ((), ())), preferred_element_type=F32)
            s = s + bias_ref[variant]
            m = jnp.max(s, axis=1, keepdims=True)
            p = jnp.exp(s - m)
            l = jnp.sum(p, axis=1, keepdims=True)
            pv = jnp.dot(p.astype(BF16), vw, preferred_element_type=F32)
            o_t = jnp.zeros((Q_TILE, HG_WIDTH), F32)
            for h in range(HEAD_GROUP):
                o_t = jnp.where(head_masks[h], pv[h * Q_TILE:(h + 1) * Q_TILE], o_t)
            m_t, l_t = per_head(m), per_head(l)
            if not first:
                m_old = rows(m_acc, q_row, Q_TILE, d)
                m_new = jnp.maximum(m_old, m_t)
                a_old, a_t = jnp.exp(m_old - m_new), jnp.exp(m_t - m_new)
                l_t = a_old * rows(l_acc, q_row, Q_TILE, d) + a_t * l_t
                o_t = a_old * rows(o_acc, q_row, Q_TILE, d) + a_t * o_t
                m_t = m_new
            if final:
                out_ref[pl.ds(pl.multiple_of(q_row, Q_TILE), Q_TILE), :] = (o_t / l_t).astype(BF16)
            else:
                put(o_acc, q_row, d, o_t)
                put(m_acc, q_row, d, m_t)
                put(l_acc, q_row, d, l_t)
            return carry

        lax.fori_loop(0, span // Q_TILE, tile, 0)


def _t5_bucket(rel):
    half = N_BUCKETS // 2
    ret = np.where(rel > 0, half, 0)
    n = np.abs(rel)
    max_exact = half // 2
    large = max_exact + (np.log(np.maximum(n, 1) / max_exact) / np.log(REL_MAX_DISTANCE / max_exact)
                         * (half - max_exact)).astype(np.int32)
    large = np.minimum(large, half - 1)
    return (ret + np.where(n < max_exact, n, large)).astype(np.int32)


def _bias_tables(rel_table, dilation, tk):
    buckets, valids = [], []
    for delta in (0, -HALF_WINDOW, Q_TILE - tk):
        off = delta + np.arange(tk)[None, :] - np.arange(Q_TILE)[:, None]
        valids.append(np.abs(off) <= HALF_WINDOW)
        buckets.append(_t5_bucket(np.clip(off, -HALF_WINDOW, HALF_WINDOW) * dilation))
    bucket = jnp.asarray(np.stack(buckets), I32)
    onehot = (bucket[..., None] == jnp.arange(N_BUCKETS, dtype=I32)).astype(F32)
    bias = jnp.einsum("vqkb,bh->hvqk", onehot, rel_table, precision=lax.Precision.HIGHEST)
    bias = jnp.where(jnp.asarray(np.stack(valids))[None], bias, MASK_VALUE)
    n_hg = N_HEADS // HEAD_GROUP
    bias = bias.reshape(n_hg, HEAD_GROUP, 3, Q_TILE, tk).transpose(0, 2, 1, 3, 4)
    return bias.reshape(n_hg, 3, HEAD_GROUP * Q_TILE, tk)


def _attention(q, k, v, rel_table):
    B, n_slab_total, L, _ = q.shape
    span = min(L, 2048)
    n_hg = N_HEADS // HEAD_GROUP
    slabs = HG_WIDTH // LANES
    biases = [_bias_tables(rel_table, d, min(2 * Q_TILE, L // d)) for d in (16, 4, 1)]
    once = pl.Buffered(1)
    bias_spec = lambda t: pl.BlockSpec((None,) + t.shape[1:], lambda g, b, s: (g, 0, 0, 0), pipeline_mode=once)
    seq = pl.BlockSpec((None, slabs, L, LANES), lambda g, b, s: (b, g, 0, 0), pipeline_mode=once)
    return pl.pallas_call(
        functools.partial(_attn_kernel, seq_len=L),
        grid=(n_hg, B, L // span),
        in_specs=[bias_spec(t) for t in biases]
                 + [pl.BlockSpec((None, slabs, span, LANES), lambda g, b, s: (b, g, s, 0)), seq, seq],
        out_specs=pl.BlockSpec((None, span, HG_WIDTH), lambda g, b, s: (b, s, g)),
        out_shape=jax.ShapeDtypeStruct((B, L, ATTN_WIDTH), BF16),
        scratch_shapes=[pltpu.VMEM((slabs, span, LANES), F32)] * 3,
        compiler_params=_params("parallel", "parallel", "arbitrary"),
        name="attention",
    )(*biases, q, k, v)


def _mixer_out_kernel(x_ref, attn_ref, gm_ref, wo_ref, g_ref, b_ref, wrh_ref, wrl_ref,
                      x1_ref, x1p_ref, aff_ref):
    mix = jnp.dot(attn_ref[...], wo_ref[0:ATTN_WIDTH, :], preferred_element_type=F32)
    mix = mix + jnp.dot(gm_ref[...], wo_ref[ATTN_WIDTH:, :], preferred_element_type=F32)
    x1 = _layer_norm(ALPHA * x_ref[...] + mix, g_ref[...], b_ref[...])
    x1_ref[...] = x1
    hi = x1.astype(BF16)
    x1p_ref[...] = _pack_bf16_pairs(x1)
    lo = (x1 - hi.astype(F32)).astype(BF16)
    nt = (((1,), (1,)), ((), ()))
    logits = (lax.dot_general(wrh_ref[...], hi, nt, preferred_element_type=F32)
              + lax.dot_general(wrl_ref[...], hi, nt, preferred_element_type=F32)
              + lax.dot_general(wrh_ref[...], lo, nt, preferred_element_type=F32))
    m = jnp.max(logits, axis=0, keepdims=True)
    e = jnp.exp(logits - m)
    aff_ref[...] = e / jnp.sum(e, axis=0, keepdims=True)


def _mixer_out(x, attn, gm, w_out, ln_g, ln_b, wr_hi, wr_lo):
    B, L, D = x.shape
    tm = min(ROW_TILE, L)
    per_seq = L // tm
    row = lambda w: pl.BlockSpec((None, tm, w), lambda b, i: (b, i, 0))
    full = lambda shape: pl.BlockSpec(shape, lambda b, i: (0,) * len(shape))
    return pl.pallas_call(
        _mixer_out_kernel,
        grid=(B, per_seq),
        in_specs=[row(D), row(ATTN_WIDTH), row(GMLP_WIDTH)]
                 + [full(w_out.shape), full(ln_g.shape), full(ln_b.shape), full(wr_hi.shape), full(wr_lo.shape)],
        out_specs=[row(D), row(D // 2), pl.BlockSpec((N_EXPERTS, tm), lambda b, i: (0, b * per_seq + i))],
        out_shape=[jax.ShapeDtypeStruct((B, L, D), F32), jax.ShapeDtypeStruct((B, L, D // 2), I32),
                   jax.ShapeDtypeStruct((N_EXPERTS, B * L), F32)],
        compiler_params=_params("parallel", "parallel"),
        name="mixer_out",
    )(x, attn, gm, w_out, ln_g, ln_b, wr_hi, wr_lo)


def _threshold_kernel(aff_ref, tau_ref, need_ref, *, cap):
    bits = lax.bitcast_convert_type(aff_ref[...], I32)

    def step(i, tau):
        cand = tau | jnp.left_shift(jnp.int32(1), 30 - i)
        cnt = jnp.sum((bits >= cand).astype(F32), axis=1, keepdims=True)
        return jnp.where(cnt >= cap, cand, tau)

    tau = lax.fori_loop(0, 31, step, jnp.zeros((bits.shape[0], 1), I32))
    above = jnp.sum((bits > tau).astype(F32), axis=1, keepdims=True)
    tau_ref[...] = jnp.broadcast_to(tau, tau_ref.shape)
    need_ref[...] = jnp.broadcast_to(cap - above, need_ref.shape)


def _slot_kernel(aff_ref, tau_ref, need_ref, tri_ref, low_ref, slot_ref, rank_ref, span_ref, carry_sel, carry_eq):
    @pl.when(pl.program_id(0) == 0)
    def _():
        carry_sel[...] = jnp.zeros_like(carry_sel)
        carry_eq[...] = jnp.zeros_like(carry_eq)

    tau = tau_ref[:, 0:1]
    need = need_ref[:, 0:1]
    c_sel = carry_sel[:, 0:1]
    c_eq = carry_eq[:, 0:1]
    tri = tri_ref[...]
    for s in range(aff_ref.shape[1] // SCAN_TILE):
        sl = slice(s * SCAN_TILE, (s + 1) * SCAN_TILE)
        bits = lax.bitcast_convert_type(aff_ref[:, sl], I32)
        eq = (bits == tau).astype(F32)
        eq_incl = jnp.dot(eq.astype(BF16), tri, preferred_element_type=F32)
        tie_taken = (eq_incl - eq + c_eq) < need
        sel = jnp.where((bits > tau) | ((bits == tau) & tie_taken), 1.0, 0.0)
        sel_b = sel.astype(BF16)
        incl = jnp.dot(sel_b, tri, preferred_element_type=F32)
        before = (incl - sel + c_sel).astype(I32)
        slot_ref[:, sl] = jnp.where(sel > 0.0, before, -1)
        per_token = jnp.sum(sel, axis=0, keepdims=True)
        first = jnp.sum(incl, axis=0, keepdims=True) - per_token + jnp.sum(c_sel, axis=0, keepdims=True)
        lower = jnp.dot(low_ref[...], sel_b, preferred_element_type=F32)
        rank_ref[:, sl] = (first + lower).astype(I32)
        span_ref[0:1, sl] = first
        span_ref[1:2, sl] = first + per_token
        c_sel = c_sel + incl[:, SCAN_TILE - 1:SCAN_TILE]
        c_eq = c_eq + eq_incl[:, SCAN_TILE - 1:SCAN_TILE]
    carry_sel[...] = jnp.broadcast_to(c_sel, carry_sel.shape)
    carry_eq[...] = jnp.broadcast_to(c_eq, carry_eq.shape)


def _select(aff_t, cap):
    E, n = aff_t.shape
    stat = jax.ShapeDtypeStruct((E, LANES), I32)
    tau, need = pl.pallas_call(
        functools.partial(_threshold_kernel, cap=float(cap)),
        out_shape=[stat, jax.ShapeDtypeStruct((E, LANES), F32)],
        compiler_params=pltpu.CompilerParams(vmem_limit_bytes=VMEM_LIMIT),
        name="threshold",
    )(aff_t)
    tri = jnp.asarray(np.triu(np.ones((SCAN_TILE, SCAN_TILE), np.float32)), BF16)
    low = jnp.asarray(np.tril(np.ones((E, E), np.float32), -1), BF16)
    tb = min(SLOT_BLOCK, n)
    blk = pl.BlockSpec((E, tb), lambda i: (0, i))
    const = lambda shape: pl.BlockSpec(shape, lambda i: (0, 0))
    return pl.pallas_call(
        _slot_kernel,
        grid=(n // tb,),
        in_specs=[blk, const((E, LANES)), const((E, LANES)), const((SCAN_TILE, SCAN_TILE)), const((E, E))],
        out_specs=[blk, blk, pl.BlockSpec((2, tb), lambda i: (0, i))],
        out_shape=[jax.ShapeDtypeStruct((E, n), I32)] * 2 + [jax.ShapeDtypeStruct((2, n), F32)],
        scratch_shapes=[pltpu.VMEM((E, LANES), F32), pltpu.VMEM((E, LANES), F32)],
        compiler_params=_params("arbitrary"),
        name="slots",
    )(aff_t, tau, need, tri, low)


def _work_list(span, total_rows, t_chunk, t_rows):
    n = span.shape[1]
    nch, ntl = n // t_chunk, total_rows // t_rows
    kmax = N_EXPERTS * t_chunk // t_rows + 1
    base = span[0, ::t_chunk].astype(I32)
    end = jnp.concatenate([base[1:], jnp.full((1,), total_rows, I32)])
    first = jnp.minimum(base // t_rows, ntl - 1)
    last = jnp.where(end > base, (end - 1) // t_rows, first)
    tile = first[:, None] + jnp.arange(kmax, dtype=I32)
    valid = (tile <= last[:, None]).reshape(-1)
    tile = jnp.minimum(tile, ntl - 1).reshape(-1)
    chunk = jnp.repeat(jnp.arange(nch, dtype=I32), kmax)
    length = nch + ntl
    count = jnp.sum(valid.astype(I32))
    idx = jnp.nonzero(valid, size=length, fill_value=0)[0].astype(I32)
    ar = jnp.arange(length, dtype=I32)
    live = ar < count
    idx = jnp.where(live, idx, idx[count - 1])
    c, t = chunk[idx], tile[idx]
    prev = jnp.concatenate([jnp.full((1,), -1, I32), c[:-1]])
    nxt = jnp.concatenate([c[1:], jnp.full((1,), -1, I32)])
    is_first = live & ((ar == 0) | (c != prev))
    is_last = live & ((ar == count - 1) | (c != nxt))
    return c, t, live.astype(I32) + 2 * is_first.astype(I32) + 4 * is_last.astype(I32)


def _pack_bf16_pairs(x):
    bits = lax.bitcast_convert_type(x.astype(BF16).astype(F32), I32)
    half = bits.shape[1] // 2
    return lax.shift_right_logical(bits[:, :half], 16) | bits[:, half:]


def _unpack_bf16_pairs(w):
    left = lax.bitcast_convert_type(lax.shift_left(w, 16), F32)
    right = lax.bitcast_convert_type(w & jnp.int32(-65536), F32)
    return jnp.concatenate([left, right], axis=1).astype(BF16)


def _sc_layout(n_experts, cap):
    info = plsc.get_sparse_core_info()
    per_expert = info.num_cores * info.num_subcores // n_experts
    return info.num_cores, info.num_lanes, per_expert, cap // per_expert


def _sc_worker(n_cores, per_expert):
    wid = lax.axis_index("s") * n_cores + lax.axis_index("c")
    return wid // per_expert, wid % per_expert


def _sc_invert(slot_hbm, payload_hbms, slot_v, payload_vs, e, part, share, n, chunk, lanes, store):
    @pl.loop(0, n // chunk)
    def _(ci):
        pltpu.sync_copy(slot_hbm.at[e, pl.ds(ci * chunk, chunk)], slot_v)
        for src, dst in zip(payload_hbms, payload_vs):
            pltpu.sync_copy(src.at[e, pl.ds(ci * chunk, chunk)], dst)

        @pl.loop(0, chunk // lanes)
        def _(i):
            local = slot_v[pl.ds(i * lanes, lanes)] - part * share
            mine = (local >= 0) & (local < share)
            tok = ci * chunk + i * lanes + lax.iota(I32, lanes)
            store(local, mine, tok, [v[pl.ds(i * lanes, lanes)] for v in payload_vs])


_SC_PARAMS = dataclasses.replace(pltpu.CompilerParams(), needs_layout_passes=False)


def _dispatch(x1p, slot, aff_t, cap):
    n, width = x1p.shape
    E = slot.shape[0]
    n_cores, lanes, per_expert, share = _sc_layout(E, cap)
    chunk = min(n, SC_SLOT_CHUNK)
    mesh = plsc.VectorSubcoreMesh(core_axis_name="c", subcore_axis_name="s")

    @functools.partial(
        pl.kernel, mesh=mesh, compiler_params=_SC_PARAMS,
        out_type=[jax.ShapeDtypeStruct((E * cap, width), I32), jax.ShapeDtypeStruct((E * cap,), F32)],
        scratch_types=[pltpu.VMEM((chunk,), I32), pltpu.VMEM((chunk,), F32), pltpu.VMEM((share,), I32),
                       pltpu.VMEM((share,), F32), pltpu.VMEM((SC_ROWS, width), I32), pltpu.SemaphoreType.DMA],
    )
    def gather(x_hbm, slot_hbm, aff_hbm, out_hbm, gate_hbm, slot_v, aff_v, idx_v, gate_v, rows_v, sem):
        e, part = _sc_worker(n_cores, per_expert)

        def store(local, mine, tok, payloads):
            plsc.store_scatter(idx_v, [local], tok, mask=mine)
            plsc.store_scatter(gate_v, [local], payloads[0], mask=mine)

        _sc_invert(slot_hbm, [aff_hbm], slot_v, [aff_v], e, part, share, n, chunk, lanes, store)
        base = e * cap + part * share
        pltpu.sync_copy(gate_v, gate_hbm.at[pl.ds(base, share)])

        @pl.loop(0, share // SC_ROWS)
        def _(j):
            pltpu.async_copy(x_hbm.at[idx_v.at[pl.ds(j * SC_ROWS, SC_ROWS)]], rows_v, sem).wait()
            pltpu.sync_copy(rows_v, out_hbm.at[pl.ds(base + j * SC_ROWS, SC_ROWS)])

    xs, gates = gather(x1p, slot, aff_t)
    return xs.reshape(E, cap, width), gates.reshape(E, cap // LANES, LANES)


def _to_token_order(ysp, slot, rank):
    E, cap, width = ysp.shape
    n = slot.shape[1]
    n_cores, lanes, per_expert, share = _sc_layout(E, cap)
    chunk = min(n, SC_SLOT_CHUNK)
    mesh = plsc.VectorSubcoreMesh(core_axis_name="c", subcore_axis_name="s")

    @functools.partial(
        pl.kernel, mesh=mesh, compiler_params=_SC_PARAMS,
        out_type=jax.ShapeDtypeStruct((E * cap, width), I32),
        scratch_types=[pltpu.VMEM((chunk,), I32), pltpu.VMEM((chunk,), I32),
                       pltpu.VMEM((share // SC_ROWS, SC_ROWS), I32),
                       pltpu.VMEM((SC_ROWS, width), I32), pltpu.SemaphoreType.DMA],
    )
    def scatter(y_hbm, slot_hbm, rank_hbm, z_hbm, slot_v, rank_v, dest_v, rows_v, sem):
        e, part = _sc_worker(n_cores, per_expert)

        def store(local, mine, tok, payloads):
            plsc.store_scatter(dest_v, [local // SC_ROWS, local % SC_ROWS], payloads[0], mask=mine)

        _sc_invert(slot_hbm, [rank_hbm], slot_v, [rank_v], e, part, share, n, chunk, lanes, store)
        base = e * cap + part * share

        @pl.loop(0, share // SC_ROWS)
        def _(j):
            pltpu.sync_copy(y_hbm.at[pl.ds(base + j * SC_ROWS, SC_ROWS)], rows_v)
            pltpu.async_copy(rows_v, z_hbm.at[dest_v.at[j]], sem).wait()

    return scatter(ysp.reshape(E * cap, width), slot, rank)


def _ffn_kernel(x_ref, gate_ref, wg_ref, wu_ref, wd_ref, y_ref, acc_ref, xb_ref):
    f = pl.program_id(2)

    @pl.when(f == 0)
    def _():
        acc_ref[...] = jnp.zeros_like(acc_ref)
        xb_ref[...] = _unpack_bf16_pairs(x_ref[...])

    x = xb_ref[...]
    g = jnp.dot(x, wg_ref[...], preferred_element_type=F32)
    u = jnp.dot(x, wu_ref[...], preferred_element_type=F32)
    h = (g * jax.nn.sigmoid(g) * u).astype(BF16)
    acc_ref[...] += jnp.dot(h, wd_ref[...], preferred_element_type=F32)

    @pl.when(f == pl.num_programs(2) - 1)
    def _():
        gates = gate_ref[...]
        pad = jnp.zeros((LANES - gates.shape[0], LANES), F32)
        cols = jnp.concatenate([gates, pad], axis=0).T
        for j in range(gates.shape[0]):
            rows = slice(j * LANES, (j + 1) * LANES)
            y_ref[rows, :] = _pack_bf16_pairs(acc_ref[rows, :] * cols[:, j:j + 1])


def _experts(xs, gates, w_gate, w_up, w_down):
    E, cap, packed = xs.shape
    D = 2 * packed
    tm = min(FFN_ROWS, cap)
    tf = FFN_COLS
    return pl.pallas_call(
        _ffn_kernel,
        grid=(E, cap // tm, D_FF // tf),
        in_specs=[pl.BlockSpec((None, tm, packed), lambda e, m, f: (e, m, 0)),
                  pl.BlockSpec((None, tm // LANES, LANES), lambda e, m, f: (e, m, 0)),
                  pl.BlockSpec((None, D, tf), lambda e, m, f: (e, 0, f)),
                  pl.BlockSpec((None, D, tf), lambda e, m, f: (e, 0, f)),
                  pl.BlockSpec((None, tf, D), lambda e, m, f: (e, f, 0))],
        out_specs=pl.BlockSpec((None, tm, packed), lambda e, m, f: (e, m, 0)),
        out_shape=jax.ShapeDtypeStruct((E, cap, packed), I32),
        scratch_shapes=[pltpu.VMEM((tm, D), F32), pltpu.VMEM((tm, D), BF16)],
        compiler_params=_params("parallel", "parallel", "arbitrary"),
        name="experts",
    )(xs, gates, w_gate, w_up, w_down)


def _combine_kernel(wc_ref, wt_ref, wf_ref, z_ref, span_ref, x1_ref, g_ref, b_ref, o_ref):
    w = pl.program_id(0)
    flags = wf_ref[w]

    @pl.when((flags & 2) == 2)
    def _():
        o_ref[...] = jnp.zeros_like(o_ref)

    @pl.when((flags & 1) == 1)
    def _():
        tr, tc = z_ref.shape[0], o_ref.shape[0]
        row = (lax.broadcasted_iota(I32, (tc, tr), 1) + wt_ref[w] * tr).astype(F32)
        owns = ((row >= span_ref[:, 0:1]) & (row < span_ref[:, 1:2])).astype(BF16)
        o_ref[...] += jnp.dot(owns, _unpack_bf16_pairs(z_ref[...]), preferred_element_type=F32)

    @pl.when((flags & 4) == 4)
    def _():
        o_ref[...] = _layer_norm(ALPHA * x1_ref[...] + o_ref[...], g_ref[...], b_ref[...])


def _combine(z, span_tm, x1, ln_g, ln_b, work):
    rows, packed = z.shape
    n, D = x1.shape
    length = work[0].shape[0]
    chunk = lambda w, wc, wt, wf: (wc[w], 0)
    const = lambda w, wc, wt, wf: (0, 0)
    return pl.pallas_call(
        _combine_kernel,
        grid_spec=pltpu.PrefetchScalarGridSpec(
            num_scalar_prefetch=3,
            grid=(length,),
            in_specs=[pl.BlockSpec((ROW_TILE, packed), lambda w, wc, wt, wf: (wt[w], 0)),
                      pl.BlockSpec((ROW_TILE, 2), chunk),
                      pl.BlockSpec((ROW_TILE, D), chunk),
                      pl.BlockSpec((1, D), const), pl.BlockSpec((1, D), const)],
            out_specs=pl.BlockSpec((ROW_TILE, D), chunk),
        ),
        out_shape=jax.ShapeDtypeStruct((n, D), F32),
        compiler_params=_params("arbitrary"),
        name="combine",
    )(*work, z, span_tm, x1, ln_g, ln_b)


def _prepare_weights(rel_bias_table, w_in, b_in, gmlp_ln_g, gmlp_ln_b, gmlp_w_s, gmlp_b_s, w_out,
                     ln1_g, ln1_b, w_router, w_gate, w_up, w_down, ln2_g, ln2_b):
    row = lambda t: t[0].reshape(1, -1).astype(F32)
    ws = gmlp_w_s[0].astype(BF16)
    ws_pairs = jnp.concatenate([ws[0::2], ws[1::2]], axis=-1)
    bs_full = jnp.repeat(gmlp_b_s[0].T, HEAD_DIM, axis=1).astype(F32)
    wr_t = w_router[0].T.astype(F32)
    wr_hi = wr_t.astype(BF16)
    wr_lo = (wr_t - wr_hi.astype(F32)).astype(BF16)
    return dict(
        rel=rel_bias_table.astype(F32), w_in=w_in[0].astype(BF16), b_in=row(b_in),
        gln_g=row(gmlp_ln_g), gln_b=row(gmlp_ln_b), ws_pairs=ws_pairs, bs_full=bs_full,
        w_out=w_out[0].astype(BF16), ln1_g=row(ln1_g), ln1_b=row(ln1_b), wr_hi=wr_hi, wr_lo=wr_lo,
        w_gate=w_gate[0].astype(BF16), w_up=w_up[0].astype(BF16), w_down=w_down[0].astype(BF16),
        ln2_g=row(ln2_g), ln2_b=row(ln2_b))


def _trunk(x, p):
    B, L, D = x.shape
    n = B * L
    cap = CAPACITY_FACTOR * n // N_EXPERTS
    q, k, v, gm = _mixer_in(x, p["w_in"], p["b_in"], p["gln_g"], p["gln_b"], p["ws_pairs"], p["bs_full"])
    attn = _attention(q, k, v, p["rel"])
    x1, x1p, aff_t = _mixer_out(x, attn, gm, p["w_out"], p["ln1_g"], p["ln1_b"], p["wr_hi"], p["wr_lo"])
    slot, rank, span = _select(aff_t, cap)
    xs, gates = _dispatch(x1p.reshape(n, D // 2), slot, aff_t, cap)
    ys = _experts(xs, gates, p["w_gate"], p["w_up"], p["w_down"])
    z = _to_token_order(ys, slot, rank)
    work = _work_list(span, N_EXPERTS * cap, ROW_TILE, ROW_TILE)
    y = _combine(z, span.T, x1.reshape(n, D), p["ln2_g"], p["ln2_b"], work)
    return y.reshape(B, L, D)


def kernel(x_prompt, x_sample, rel_bias_table, w_in, b_in, gmlp_ln_g, gmlp_ln_b, gmlp_w_s, gmlp_b_s, w_out,
           ln1_g, ln1_b, w_router, w_gate, w_up, w_down, ln2_g, ln2_b):
    p = _prepare_weights(rel_bias_table, w_in, b_in, gmlp_ln_g, gmlp_ln_b, gmlp_w_s, gmlp_b_s, w_out,
                         ln1_g, ln1_b, w_router, w_gate, w_up, w_down, ln2_g, ln2_b)
    return (_trunk(x_prompt, p), _trunk(x_sample, p))
```

```python
import dataclasses
import functools
import math

import numpy as np
import jax
import jax.numpy as jnp
from jax import lax
from jax.experimental import pallas as pl
from jax.experimental.pallas import tpu as pltpu
from jax.experimental.pallas import tpu_sc as plsc

F32 = jnp.float32
BF16 = jnp.bfloat16
I32 = jnp.int32

D_MODEL = 1024
HEAD_DIM = 64
ATTN_WIDTH = 512
GMLP_WIDTH = 512
N_HEADS = ATTN_WIDTH // HEAD_DIM
N_GROUPS = GMLP_WIDTH // HEAD_DIM
IN_WIDTH = 3 * ATTN_WIDTH + 2 * GMLP_WIDTH
GMLP_CHUNK = 128
BRANCHES = ((128, 1), (512, 4), (2048, 16))
HALF_WINDOW = 64
N_BUCKETS = 32
REL_MAX_DISTANCE = 1024
N_EXPERTS = 16
CAPACITY_FACTOR = 2
D_FF = 2816
ALPHA = 2.0 ** 0.25
LN_EPS = 1e-5
MASK_VALUE = -1e30

LANES = 128
MXU_DIM = 256
VMEM_LIMIT = 56 * 1024 * 1024

Q_TILE = 128
HEAD_GROUP = 4
HG_WIDTH = HEAD_GROUP * HEAD_DIM
ATTN_UNROLL = 4
ROW_TILE = 512
SCAN_TILE = 256
SLOT_BLOCK = 2048
SC_SLOT_CHUNK = 4096
SC_ROWS = 64
FFN_ROWS = 2048
FFN_COLS = 256


def _params(*sem):
    return pltpu.CompilerParams(dimension_semantics=sem, vmem_limit_bytes=VMEM_LIMIT)


def _layer_norm(y, g, b):
    mu = jnp.mean(y, axis=-1, keepdims=True)
    yc = y - mu
    var = jnp.mean(yc * yc, axis=-1, keepdims=True)
    return yc * lax.rsqrt(var + LN_EPS) * g + b


def _gelu_tanh(x):
    return 0.5 * x * (1.0 + jnp.tanh(math.sqrt(2.0 / math.pi) * (x + 0.044715 * (x * x * x))))


def _mixer_in_kernel(x_ref, w_ref, b_ref, lng_ref, lnb_ref, ws_ref, bs_ref,
                     q_ref, k_ref, v_ref, gm_ref):
    x = x_ref[...].astype(BF16)

    def proj(lo, hi):
        return jnp.dot(x, w_ref[:, lo:hi], preferred_element_type=F32) + b_ref[:, lo:hi]

    a = ATTN_WIDTH
    for i, ref in enumerate((q_ref, k_ref, v_ref)):
        t = proj(i * a, (i + 1) * a)
        if i == 0:
            t = t * (HEAD_DIM ** -0.5)
        for s in range(a // LANES):
            ref[s] = t[:, s * LANES:(s + 1) * LANES]
    gu =_gelu_tanh(proj(3 * a, 3 * a + GMLP_WIDTH))
    gv = _gelu_tanh(proj(3 * a + GMLP_WIDTH, IN_WIDTH))
    vn = _layer_norm(gv, lng_ref[...], lnb_ref[...]).astype(BF16)

    rows = x_ref.shape[0]
    low_half = lax.broadcasted_iota(I32, (1, LANES), 1) < HEAD_DIM
    for c in range(rows // GMLP_CHUNK):
        r0 = c * GMLP_CHUNK
        for s in range(GMLP_WIDTH // LANES):
            c0 = s * LANES
            vs = vn[r0:r0 + GMLP_CHUNK, c0:c0 + LANES]
            zero = jnp.zeros_like(vs)
            rhs = jnp.concatenate([jnp.where(low_half, vs, zero), jnp.where(low_half, zero, vs)], axis=0)
            vm = jnp.dot(ws_ref[s], rhs, preferred_element_type=F32) + bs_ref[:, c0:c0 + LANES]
            gm_ref[r0:r0 + GMLP_CHUNK, c0:c0 + LANES] = (gu[r0:r0 + GMLP_CHUNK, c0:c0 + LANES] * vm).astype(BF16)


def _mixer_in(x, w_in, b_in, ln_g, ln_b, ws_pairs, bs_full):
    B, L, D = x.shape
    tm = min(ROW_TILE, L)
    row = lambda w: pl.BlockSpec((None, tm, w), lambda b, i: (b, i, 0))
    full = lambda shape: pl.BlockSpec(shape, lambda b, i: (0,) * len(shape))
    n_slab = ATTN_WIDTH // LANES
    slab = pl.BlockSpec((None, n_slab, tm, LANES), lambda b, i: (b, 0, i, 0))
    slab_shape = jax.ShapeDtypeStruct((B, n_slab, L, LANES), F32)
    return pl.pallas_call(
        _mixer_in_kernel,
        grid=(B, L // tm),
        in_specs=[row(D), full(w_in.shape), full(b_in.shape), full(ln_g.shape), full(ln_b.shape),
                  full(ws_pairs.shape), full(bs_full.shape)],
        out_specs=[slab, slab, slab, row(GMLP_WIDTH)],
        out_shape=[slab_shape] * 3 + [jax.ShapeDtypeStruct((B, L, GMLP_WIDTH), BF16)],
        compiler_params=_params("parallel", "parallel"),
        name="mixer_in",
    )(x, w_in, b_in, ln_g, ln_b, ws_pairs, bs_full)


def _attn_kernel(b16_ref, b4_ref, b1_ref, q_ref, k_ref, v_ref, out_ref, o_acc, m_acc, l_acc, *, seq_len):
    span = out_ref.shape[0]
    span_idx = pl.program_id(2)
    n_slab = HG_WIDTH // LANES
    lane = lax.broadcasted_iota(I32, (1, HG_WIDTH), 1)
    head_masks = [(lane >= h * HEAD_DIM) & (lane < (h + 1) * HEAD_DIM) for h in range(HEAD_GROUP)]

    def rows(ref, start, size, stride):
        idx = pl.ds(start, size) if stride == 1 else pl.ds(start, size, stride=stride)
        return jnp.concatenate([ref[s, idx, :] for s in range(n_slab)], axis=1)

    def put(ref, start, stride, val):
        idx = pl.ds(start, Q_TILE) if stride == 1 else pl.ds(start, Q_TILE, stride=stride)
        for s in range(n_slab):
            ref[s, idx, :] = val[:, s * LANES:(s + 1) * LANES]

    def per_head(cols):
        out = jnp.zeros((Q_TILE, HG_WIDTH), F32)
        for h in range(HEAD_GROUP):
            out = jnp.where(head_masks[h], cols[h * Q_TILE:(h + 1) * Q_TILE], out)
        return out

    branches = ((16, b16_ref), (4, b4_ref), (1, b1_ref))
    for bi, (d, bias_ref) in enumerate(branches):
        lr = seq_len // d
        tk = min(2 * Q_TILE, lr)
        tiles_total = lr // Q_TILE
        tiles_per_residue = span // d // Q_TILE
        first, final = bi == 0, bi == len(branches) - 1

        def tile(idx, carry, d=d, bias_ref=bias_ref, lr=lr, tk=tk, tiles_total=tiles_total,
                 tiles_per_residue=tiles_per_residue, first=first, final=final):
            r = idx // tiles_per_residue
            jt = idx % tiles_per_residue
            jg = span_idx * tiles_per_residue + jt
            start = jnp.clip(jg * Q_TILE - HALF_WINDOW, 0, lr - tk)
            variant = jnp.where(jg == 0, 0, jnp.where(jg == tiles_total - 1, 2, 1))
            q_row = r + d * (jt * Q_TILE)
            q = rows(q_ref, q_row, Q_TILE, d).astype(BF16)
            kw = rows(k_ref, r + d * start, tk, d).astype(BF16)
            vw = rows(v_ref, r + d * start, tk, d).astype(BF16)
            qs = jnp.concatenate([jnp.where(hm, q, jnp.zeros_like(q)) for hm in head_masks], axis=0)
            s = lax.dot_general(qs, kw, (((1,), (1,)), ((), ())), preferred_element_type=F32)
            s = s + bias_ref[variant]
            m = jnp.max(s, axis=1, keepdims=True)
            p = jnp.exp(s - m)
            l = jnp.sum(p, axis=1, keepdims=True)
            pv = jnp.dot(p.astype(BF16), vw, preferred_element_type=F32)
            o_t = jnp.zeros((Q_TILE, HG_WIDTH), F32)
            for h in range(HEAD_GROUP):
                o_t = jnp.where(head_masks[h], pv[h * Q_TILE:(h + 1) * Q_TILE], o_t)
            m_t, l_t = per_head(m), per_head(l)
            if not first:
                m_old = rows(m_acc, q_row, Q_TILE, d)
                m_new = jnp.maximum(m_old, m_t)
                a_old, a_t = jnp.exp(m_old - m_new), jnp.exp(m_t - m_new)
                l_t = a_old * rows(l_acc, q_row, Q_TILE, d) + a_t * l_t
                o_t = a_old * rows(o_acc, q_row, Q_TILE, d) + a_t * o_t
                m_t = m_new
            if final:
                out_ref[pl.ds(pl.multiple_of(q_row, Q_TILE), Q_TILE), :] = (o_t / l_t).astype(BF16)
            else:
                put(o_acc, q_row, d, o_t)
                put(m_acc, q_row, d, m_t)
                put(l_acc, q_row, d, l_t)
            return carry

        lax.fori_loop(0, span // Q_TILE, tile, 0, unroll=ATTN_UNROLL)


def _t5_bucket(rel):
    half = N_BUCKETS // 2
    ret = np.where(rel > 0, half, 0)
    n = np.abs(rel)
    max_exact = half // 2
    large = max_exact + (np.log(np.maximum(n, 1) / max_exact) / np.log(REL_MAX_DISTANCE / max_exact)
                         * (half - max_exact)).astype(np.int32)
    large = np.minimum(large, half - 1)
    return (ret + np.where(n < max_exact, n, large)).astype(np.int32)


def _bias_tables(rel_table, dilation, tk):
    buckets, valids = [], []
    for delta in (0, -HALF_WINDOW, Q_TILE - tk):
        off = delta + np.arange(tk)[None, :] - np.arange(Q_TILE)[:, None]
        valids.append(np.abs(off) <= HALF_WINDOW)
        buckets.append(_t5_bucket(np.clip(off, -HALF_WINDOW, HALF_WINDOW) * dilation))
    bucket = jnp.asarray(np.stack(buckets), I32)
    onehot = (bucket[..., None] == jnp.arange(N_BUCKETS, dtype=I32)).astype(F32)
    bias = jnp.einsum("vqkb,bh->hvqk", onehot, rel_table, precision=lax.Precision.HIGHEST)
    bias = jnp.where(jnp.asarray(np.stack(valids))[None], bias, MASK_VALUE)
    n_hg = N_HEADS // HEAD_GROUP
    bias = bias.reshape(n_hg, HEAD_GROUP, 3, Q_TILE, tk).transpose(0, 2, 1, 3, 4)
    return bias.reshape(n_hg, 3, HEAD_GROUP * Q_TILE, tk)


def _attention(q, k, v, rel_table):
    B, n_slab_total, L, _ = q.shape
    span = min(L, 2048)
    n_hg = N_HEADS // HEAD_GROUP
    slabs = HG_WIDTH // LANES
    biases = [_bias_tables(rel_table, d, min(2 * Q_TILE, L // d)) for d in (16, 4, 1)]
    once = pl.Buffered(1)
    bias_spec = lambda t: pl.BlockSpec((None,) + t.shape[1:], lambda g, b, s: (g, 0, 0, 0), pipeline_mode=once)
    seq = pl.BlockSpec((None, slabs, L, LANES), lambda g, b, s: (b, g, 0, 0), pipeline_mode=once)
    return pl.pallas_call(
        functools.partial(_attn_kernel, seq_len=L),
        grid=(n_hg, B, L // span),
        in_specs=[bias_spec(t) for t in biases]
                 + [pl.BlockSpec((None, slabs, span, LANES), lambda g, b, s: (b, g, s, 0)), seq, seq],
        out_specs=pl.BlockSpec((None, span, HG_WIDTH), lambda g, b, s: (b, s, g)),
        out_shape=jax.ShapeDtypeStruct((B, L, ATTN_WIDTH), BF16),
        scratch_shapes=[pltpu.VMEM((slabs, span, LANES), F32)] * 3,
        compiler_params=_params("parallel", "parallel", "arbitrary"),
        name="attention",
    )(*biases, q, k, v)


def _mixer_out_kernel(x_ref, attn_ref, gm_ref, wo_ref, g_ref, b_ref, wrh_ref, wrl_ref,
                      x1_ref, x1p_ref, aff_ref):
    mix = jnp.dot(attn_ref[...], wo_ref[0:ATTN_WIDTH, :], preferred_element_type=F32)
    mix = mix + jnp.dot(gm_ref[...], wo_ref[ATTN_WIDTH:, :], preferred_element_type=F32)
    x1 = _layer_norm(ALPHA * x_ref[...] + mix, g_ref[...], b_ref[...])
    x1_ref[...] = x1
    hi = x1.astype(BF16)
    x1p_ref[...] = _pack_bf16_pairs(x1)
    lo = (x1 - hi.astype(F32)).astype(BF16)
    nt = (((1,), (1,)), ((), ()))
    logits = (lax.dot_general(wrh_ref[...], hi, nt, preferred_element_type=F32)
              + lax.dot_general(wrl_ref[...], hi, nt, preferred_element_type=F32)
              + lax.dot_general(wrh_ref[...], lo, nt, preferred_element_type=F32))
    m = jnp.max(logits, axis=0, keepdims=True)
    e = jnp.exp(logits - m)
    aff_ref[...] = e / jnp.sum(e, axis=0, keepdims=True)


def _mixer_out(x, attn, gm, w_out, ln_g, ln_b, wr_hi, wr_lo):
    B, L, D = x.shape
    tm = min(ROW_TILE, L)
    per_seq = L // tm
    row = lambda w: pl.BlockSpec((None, tm, w), lambda b, i: (b, i, 0))
    full = lambda shape: pl.BlockSpec(shape, lambda b, i: (0,) * len(shape))
    return pl.pallas_call(
        _mixer_out_kernel,
        grid=(B, per_seq),
        in_specs=[row(D), row(ATTN_WIDTH), row(GMLP_WIDTH)]
                 + [full(w_out.shape), full(ln_g.shape), full(ln_b.shape), full(wr_hi.shape), full(wr_lo.shape)],
        out_specs=[row(D), row(D // 2), pl.BlockSpec((N_EXPERTS, tm), lambda b, i: (0, b * per_seq + i))],
        out_shape=[jax.ShapeDtypeStruct((B, L, D), F32), jax.ShapeDtypeStruct((B, L, D // 2), I32),
                   jax.ShapeDtypeStruct((N_EXPERTS, B * L), F32)],
        compiler_params=_params("parallel", "parallel"),
        name="mixer_out",
    )(x, attn, gm, w_out, ln_g, ln_b, wr_hi, wr_lo)


def _threshold_kernel(aff_ref, tau_ref, need_ref, *, cap):
    bits = lax.bitcast_convert_type(aff_ref[...], I32)

    def step(i, tau):
        cand = tau | jnp.left_shift(jnp.int32(1), 30 - i)
        cnt = jnp.sum((bits >= cand).astype(F32), axis=1, keepdims=True)
        return jnp.where(cnt >= cap, cand, tau)

    tau = lax.fori_loop(0, 31, step, jnp.zeros((bits.shape[0], 1), I32))
    above = jnp.sum((bits > tau).astype(F32), axis=1, keepdims=True)
    tau_ref[...] = jnp.broadcast_to(tau, tau_ref.shape)
    need_ref[...] = jnp.broadcast_to(cap - above, need_ref.shape)


def _slot_kernel(aff_ref, tau_ref, need_ref, tri_ref, low_ref, slot_ref, rank_ref, span_ref, carry_sel, carry_eq):
    @pl.when(pl.program_id(0) == 0)
    def _():
        carry_sel[...] = jnp.zeros_like(carry_sel)
        carry_eq[...] = jnp.zeros_like(carry_eq)

    tau = tau_ref[:, 0:1]
    need = need_ref[:, 0:1]
    c_sel = carry_sel[:, 0:1]
    c_eq = carry_eq[:, 0:1]
    tri = tri_ref[...]
    for s in range(aff_ref.shape[1] // SCAN_TILE):
        sl = slice(s * SCAN_TILE, (s + 1) * SCAN_TILE)
        bits = lax.bitcast_convert_type(aff_ref[:, sl], I32)
        eq = (bits == tau).astype(F32)
        eq_incl = jnp.dot(eq.astype(BF16), tri, preferred_element_type=F32)
        tie_taken = (eq_incl - eq + c_eq) < need
        sel = jnp.where((bits > tau) | ((bits == tau) & tie_taken), 1.0, 0.0)
        sel_b = sel.astype(BF16)
        incl = jnp.dot(sel_b, tri, preferred_element_type=F32)
        before = (incl - sel + c_sel).astype(I32)
        slot_ref[:, sl] = jnp.where(sel > 0.0, before, -1)
        per_token = jnp.sum(sel, axis=0, keepdims=True)
        first = jnp.sum(incl, axis=0, keepdims=True) - per_token + jnp.sum(c_sel, axis=0, keepdims=True)
        lower = jnp.dot(low_ref[...], sel_b, preferred_element_type=F32)
        rank_ref[:, sl] = (first + lower).astype(I32)
        span_ref[0:1, sl] = first
        span_ref[1:2, sl] = first + per_token
        c_sel = c_sel + incl[:, SCAN_TILE - 1:SCAN_TILE]
        c_eq = c_eq + eq_incl[:, SCAN_TILE - 1:SCAN_TILE]
    carry_sel[...] = jnp.broadcast_to(c_sel, carry_sel.shape)
    carry_eq[...] = jnp.broadcast_to(c_eq, carry_eq.shape)


def _select(aff_t, cap):
    E, n = aff_t.shape
    stat = jax.ShapeDtypeStruct((E, LANES), I32)
    tau, need = pl.pallas_call(
        functools.partial(_threshold_kernel, cap=float(cap)),
        out_shape=[stat, jax.ShapeDtypeStruct((E, LANES), F32)],
        compiler_params=pltpu.CompilerParams(vmem_limit_bytes=VMEM_LIMIT),
        name="threshold",
    )(aff_t)
    tri = jnp.asarray(np.triu(np.ones((SCAN_TILE, SCAN_TILE), np.float32)), BF16)
    low = jnp.asarray(np.tril(np.ones((E, E), np.float32), -1), BF16)
    tb = min(SLOT_BLOCK, n)
    blk = pl.BlockSpec((E, tb), lambda i: (0, i))
    const = lambda shape: pl.BlockSpec(shape, lambda i: (0, 0))
    return pl.pallas_call(
        _slot_kernel,
        grid=(n // tb,),
        in_specs=[blk, const((E, LANES)), const((E, LANES)), const((SCAN_TILE, SCAN_TILE)), const((E, E))],
        out_specs=[blk, blk, pl.BlockSpec((2, tb), lambda i: (0, i))],
        out_shape=[jax.ShapeDtypeStruct((E, n), I32)] * 2 + [jax.ShapeDtypeStruct((2, n), F32)],
        scratch_shapes=[pltpu.VMEM((E, LANES), F32), pltpu.VMEM((E, LANES), F32)],
        compiler_params=_params("arbitrary"),
        name="slots",
    )(aff_t, tau, need, tri, low)


def _work_list(span, total_rows, t_chunk, t_rows):
    n = span.shape[1]
    nch, ntl = n // t_chunk, total_rows // t_rows
    kmax = N_EXPERTS * t_chunk // t_rows + 1
    base = span[0, ::t_chunk].astype(I32)
    end = jnp.concatenate([base[1:], jnp.full((1,), total_rows, I32)])
    first = jnp.minimum(base // t_rows, ntl - 1)
    last = jnp.where(end > base, (end - 1) // t_rows, first)
    tile = first[:, None] + jnp.arange(kmax, dtype=I32)
    valid = (tile <= last[:, None]).reshape(-1)
    tile = jnp.minimum(tile, ntl - 1).reshape(-1)
    chunk = jnp.repeat(jnp.arange(nch, dtype=I32), kmax)
    length = nch + ntl
    count = jnp.sum(valid.astype(I32))
    idx = jnp.nonzero(valid, size=length, fill_value=0)[0].astype(I32)
    ar = jnp.arange(length, dtype=I32)
    live = ar < count
    idx = jnp.where(live, idx, idx[count - 1])
    c, t = chunk[idx], tile[idx]
    prev = jnp.concatenate([jnp.full((1,), -1, I32), c[:-1]])
    nxt = jnp.concatenate([c[1:], jnp.full((1,), -1, I32)])
    is_first = live & ((ar == 0) | (c != prev))
    is_last = live & ((ar == count - 1) | (c != nxt))
    return c, t, live.astype(I32) + 2 * is_first.astype(I32) + 4 * is_last.astype(I32)


def _pack_bf16_pairs(x):
    bits = lax.bitcast_convert_type(x.astype(BF16).astype(F32), I32)
    half = bits.shape[1] // 2
    return lax.shift_right_logical(bits[:, :half], 16) | bits[:, half:]


def _unpack_bf16_pairs(w):
    left = lax.bitcast_convert_type(lax.shift_left(w, 16), F32)
    right = lax.bitcast_convert_type(w & jnp.int32(-65536), F32)
    return jnp.concatenate([left, right], axis=1).astype(BF16)


def _sc_layout(n_experts, cap):
    info = plsc.get_sparse_core_info()
    per_expert = info.num_cores * info.num_subcores // n_experts
    return info.num_cores, info.num_lanes, per_expert, cap // per_expert


def _sc_worker(n_cores, per_expert):
    wid = lax.axis_index("s") * n_cores + lax.axis_index("c")
    return wid // per_expert, wid % per_expert


def _sc_invert(slot_hbm, payload_hbms, slot_v, payload_vs, e, part, share, n, chunk, lanes, store):
    @pl.loop(0, n // chunk)
    def _(ci):
        pltpu.sync_copy(slot_hbm.at[e, pl.ds(ci * chunk, chunk)], slot_v)
        for src, dst in zip(payload_hbms, payload_vs):
            pltpu.sync_copy(src.at[e, pl.ds(ci * chunk, chunk)], dst)

        @pl.loop(0, chunk // lanes)
        def _(i):
            local = slot_v[pl.ds(i * lanes, lanes)] - part * share
            mine = (local >= 0) & (local < share)
            tok = ci * chunk + i * lanes + lax.iota(I32, lanes)
            store(local, mine, tok, [v[pl.ds(i * lanes, lanes)] for v in payload_vs])


_SC_PARAMS = dataclasses.replace(pltpu.CompilerParams(), needs_layout_passes=False)


def _dispatch(x1p, slot, aff_t, cap):
    n, width = x1p.shape
    E = slot.shape[0]
    n_cores, lanes, per_expert, share = _sc_layout(E, cap)
    chunk = min(n, SC_SLOT_CHUNK)
    mesh = plsc.VectorSubcoreMesh(core_axis_name="c", subcore_axis_name="s")

    @functools.partial(
        pl.kernel, mesh=mesh, compiler_params=_SC_PARAMS,
        out_type=[jax.ShapeDtypeStruct((E * cap, width), I32), jax.ShapeDtypeStruct((E * cap,), F32)],
        scratch_types=[pltpu.VMEM((chunk,), I32), pltpu.VMEM((chunk,), F32), pltpu.VMEM((share,), I32),
                       pltpu.VMEM((share,), F32), pltpu.VMEM((SC_ROWS, width), I32), pltpu.SemaphoreType.DMA],
    )
    def gather(x_hbm, slot_hbm, aff_hbm, out_hbm, gate_hbm, slot_v, aff_v, idx_v, gate_v, rows_v, sem):
        e, part = _sc_worker(n_cores, per_expert)

        def store(local, mine, tok, payloads):
            plsc.store_scatter(idx_v, [local], tok, mask=mine)
            plsc.store_scatter(gate_v, [local], payloads[0], mask=mine)

        _sc_invert(slot_hbm, [aff_hbm], slot_v, [aff_v], e, part, share, n, chunk, lanes, store)
        base = e * cap + part * share
        pltpu.sync_copy(gate_v, gate_hbm.at[pl.ds(base, share)])

        @pl.loop(0, share // SC_ROWS)
        def _(j):
            pltpu.async_copy(x_hbm.at[idx_v.at[pl.ds(j * SC_ROWS, SC_ROWS)]], rows_v, sem).wait()
            pltpu.sync_copy(rows_v, out_hbm.at[pl.ds(base + j * SC_ROWS, SC_ROWS)])

    xs, gates = gather(x1p, slot, aff_t)
    return xs.reshape(E, cap, width), gates.reshape(E, cap // LANES, LANES)


def _to_token_order(ysp, slot, rank):
    E, cap, width = ysp.shape
    n = slot.shape[1]
    n_cores, lanes, per_expert, share = _sc_layout(E, cap)
    chunk = min(n, SC_SLOT_CHUNK)
    mesh = plsc.VectorSubcoreMesh(core_axis_name="c", subcore_axis_name="s")

    @functools.partial(
        pl.kernel, mesh=mesh, compiler_params=_SC_PARAMS,
        out_type=jax.ShapeDtypeStruct((E * cap, width), I32),
        scratch_types=[pltpu.VMEM((chunk,), I32), pltpu.VMEM((chunk,), I32),
                       pltpu.VMEM((share // SC_ROWS, SC_ROWS), I32),
                       pltpu.VMEM((SC_ROWS, width), I32), pltpu.SemaphoreType.DMA],
    )
    def scatter(y_hbm, slot_hbm, rank_hbm, z_hbm, slot_v, rank_v, dest_v, rows_v, sem):
        e, part = _sc_worker(n_cores, per_expert)

        def store(local, mine, tok, payloads):
            plsc.store_scatter(dest_v, [local // SC_ROWS, local % SC_ROWS], payloads[0], mask=mine)

        _sc_invert(slot_hbm, [rank_hbm], slot_v, [rank_v], e, part, share, n, chunk, lanes, store)
        base = e * cap + part * share

        @pl.loop(0, share // SC_ROWS)
        def _(j):
            pltpu.sync_copy(y_hbm.at[pl.ds(base + j * SC_ROWS, SC_ROWS)], rows_v)
            pltpu.async_copy(rows_v, z_hbm.at[dest_v.at[j]], sem).wait()

    return scatter(ysp.reshape(E * cap, width), slot, rank)


def _ffn_kernel(x_ref, gate_ref, wg_ref, wu_ref, wd_ref, y_ref, acc_ref, xb_ref):
    f = pl.program_id(2)

    @pl.when(f == 0)
    def _():
        acc_ref[...] = jnp.zeros_like(acc_ref)
        xb_ref[...] = _unpack_bf16_pairs(x_ref[...])

    x = xb_ref[...]
    g = jnp.dot(x, wg_ref[...], preferred_element_type=F32)
    u = jnp.dot(x, wu_ref[...], preferred_element_type=F32)
    h = (g * jax.nn.sigmoid(g) * u).astype(BF16)
    acc_ref[...] += jnp.dot(h, wd_ref[...], preferred_element_type=F32)

    @pl.when(f == pl.num_programs(2) - 1)
    def _():
        gates = gate_ref[...]
        pad = jnp.zeros((LANES - gates.shape[0], LANES), F32)
        cols = jnp.concatenate([gates, pad], axis=0).T
        for j in range(gates.shape[0]):
            rows = slice(j * LANES, (j + 1) * LANES)
            y_ref[rows, :] = _pack_bf16_pairs(acc_ref[rows, :] * cols[:, j:j + 1])


def _experts(xs, gates, w_gate, w_up, w_down):
    E, cap, packed = xs.shape
    D = 2 * packed
    tm = min(FFN_ROWS, cap)
    tf = FFN_COLS
    return pl.pallas_call(
        _ffn_kernel,
        grid=(E, cap // tm, D_FF // tf),
        in_specs=[pl.BlockSpec((None, tm, packed), lambda e, m, f: (e, m, 0)),
                  pl.BlockSpec((None, tm // LANES, LANES), lambda e, m, f: (e, m, 0)),
                  pl.BlockSpec((None, D, tf), lambda e, m, f: (e, 0, f)),
                  pl.BlockSpec((None, D, tf), lambda e, m, f: (e, 0, f)),
                  pl.BlockSpec((None, tf, D), lambda e, m, f: (e, f, 0))],
        out_specs=pl.BlockSpec((None, tm, packed), lambda e, m, f: (e, m, 0)),
        out_shape=jax.ShapeDtypeStruct((E, cap, packed), I32),
        scratch_shapes=[pltpu.VMEM((tm, D), F32), pltpu.VMEM((tm, D), BF16)],
        compiler_params=_params("parallel", "parallel", "arbitrary"),
        name="experts",
    )(xs, gates, w_gate, w_up, w_down)


def _combine_kernel(wc_ref, wt_ref, wf_ref, z_ref, span_ref, x1_ref, g_ref, b_ref, o_ref):
    w = pl.program_id(0)
    flags = wf_ref[w]

    @pl.when((flags & 2) == 2)
    def _():
        o_ref[...] = jnp.zeros_like(o_ref)

    @pl.when((flags & 1) == 1)
    def _():
        tr, tc = z_ref.shape[0], o_ref.shape[0]
        row = (lax.broadcasted_iota(I32, (tc, tr), 1) + wt_ref[w] * tr).astype(F32)
        owns = ((row >= span_ref[:, 0:1]) & (row < span_ref[:, 1:2])).astype(BF16)
        o_ref[...] += jnp.dot(owns, _unpack_bf16_pairs(z_ref[...]), preferred_element_type=F32)

    @pl.when((flags & 4) == 4)
    def _():
        o_ref[...] = _layer_norm(ALPHA * x1_ref[...] + o_ref[...], g_ref[...], b_ref[...])


def _combine(z, span_tm, x1, ln_g, ln_b, work):
    rows, packed = z.shape
    n, D = x1.shape
    length = work[0].shape[0]
    chunk = lambda w, wc, wt, wf: (wc[w], 0)
    const = lambda w, wc, wt, wf: (0, 0)
    return pl.pallas_call(
        _combine_kernel,
        grid_spec=pltpu.PrefetchScalarGridSpec(
            num_scalar_prefetch=3,
            grid=(length,),
            in_specs=[pl.BlockSpec((ROW_TILE, packed), lambda w, wc, wt, wf: (wt[w], 0)),
                      pl.BlockSpec((ROW_TILE, 2), chunk),
                      pl.BlockSpec((ROW_TILE, D), chunk),
                      pl.BlockSpec((1, D), const), pl.BlockSpec((1, D), const)],
            out_specs=pl.BlockSpec((ROW_TILE, D), chunk),
        ),
        out_shape=jax.ShapeDtypeStruct((n, D), F32),
        compiler_params=_params("arbitrary"),
        name="combine",
    )(*work, z, span_tm, x1, ln_g, ln_b)


def _prepare_weights(rel_bias_table, w_in, b_in, gmlp_ln_g, gmlp_ln_b, gmlp_w_s, gmlp_b_s, w_out,
                     ln1_g, ln1_b, w_router, w_gate, w_up, w_down, ln2_g, ln2_b):
    row = lambda t: t[0].reshape(1, -1).astype(F32)
    ws = gmlp_w_s[0].astype(BF16)
    ws_pairs = jnp.concatenate([ws[0::2], ws[1::2]], axis=-1)
    bs_full = jnp.repeat(gmlp_b_s[0].T, HEAD_DIM, axis=1).astype(F32)
    wr_t = w_router[0].T.astype(F32)
    wr_hi = wr_t.astype(BF16)
    wr_lo = (wr_t - wr_hi.astype(F32)).astype(BF16)
    return dict(
        rel=rel_bias_table.astype(F32), w_in=w_in[0].astype(BF16), b_in=row(b_in),
        gln_g=row(gmlp_ln_g), gln_b=row(gmlp_ln_b), ws_pairs=ws_pairs, bs_full=bs_full,
        w_out=w_out[0].astype(BF16), ln1_g=row(ln1_g), ln1_b=row(ln1_b), wr_hi=wr_hi, wr_lo=wr_lo,
        w_gate=w_gate[0].astype(BF16), w_up=w_up[0].astype(BF16), w_down=w_down[0].astype(BF16),
        ln2_g=row(ln2_g), ln2_b=row(ln2_b))


def _trunk(x, p):
    B, L, D = x.shape
    n = B * L
    cap = CAPACITY_FACTOR * n // N_EXPERTS
    q, k, v, gm = _mixer_in(x, p["w_in"], p["b_in"], p["gln_g"], p["gln_b"], p["ws_pairs"], p["bs_full"])
    attn = _attention(q, k, v, p["rel"])
    x1, x1p, aff_t = _mixer_out(x, attn, gm, p["w_out"], p["ln1_g"], p["ln1_b"], p["wr_hi"], p["wr_lo"])
    slot, rank, span = _select(aff_t, cap)
    xs, gates = _dispatch(x1p.reshape(n, D // 2), slot, aff_t, cap)
    ys = _experts(xs, gates, p["w_gate"], p["w_up"], p["w_down"])
    z = _to_token_order(ys, slot, rank)
    work = _work_list(span, N_EXPERTS * cap, ROW_TILE, ROW_TILE)
    y = _combine(z, span.T, x1.reshape(n, D), p["ln2_g"], p["ln2_b"], work)
    return y.reshape(B, L, D)


def kernel(x_prompt, x_sample, rel_bias_table, w_in, b_in, gmlp_ln_g, gmlp_ln_b, gmlp_w_s, gmlp_b_s, w_out,
           ln1_g, ln1_b, w_router, w_gate, w_up, w_down, ln2_g, ln2_b):
    p = _prepare_weights(rel_bias_table, w_in, b_in, gmlp_ln_g, gmlp_ln_b, gmlp_w_s, gmlp_b_s, w_out,
                         ln1_g, ln1_b, w_router, w_gate, w_up, w_down, ln2_g, ln2_b)
    return (_trunk(x_prompt, p), _trunk(x_sample, p))
```

```python
import dataclasses
import functools
import math

import numpy as np
import jax
import jax.numpy as jnp
from jax import lax
from jax.experimental import pallas as pl
from jax.experimental.pallas import tpu as pltpu
from jax.experimental.pallas import tpu_sc as plsc

F32 = jnp.float32
BF16 = jnp.bfloat16
I32 = jnp.int32

D_MODEL = 1024
HEAD_DIM = 64
ATTN_WIDTH = 512
GMLP_WIDTH = 512
N_HEADS = ATTN_WIDTH // HEAD_DIM
N_GROUPS = GMLP_WIDTH // HEAD_DIM
IN_WIDTH = 3 * ATTN_WIDTH + 2 * GMLP_WIDTH
GMLP_CHUNK = 128
BRANCHES = ((128, 1), (512, 4), (2048, 16))
HALF_WINDOW = 64
N_BUCKETS = 32
REL_MAX_DISTANCE = 1024
N_EXPERTS = 16
CAPACITY_FACTOR = 2
D_FF = 2816
ALPHA = 2.0 ** 0.25
LN_EPS = 1e-5
MASK_VALUE = -1e30

LANES = 128
MXU_DIM = 256
VMEM_LIMIT = 56 * 1024 * 1024

Q_TILE = 128
HEAD_GROUP = 4
HG_WIDTH = HEAD_GROUP * HEAD_DIM
ATTN_UNROLL = 4
ROW_TILE = 512
SCAN_TILE = 256
SLOT_BLOCK = 2048
SC_SLOT_CHUNK = 4096
SC_ROWS = 64
COMBINE_CHUNK = 1024
COMBINE_ROWS = 512
FFN_ROWS = 1024
FFN_COLS = 1408


def _params(*sem):
    return pltpu.CompilerParams(dimension_semantics=sem, vmem_limit_bytes=VMEM_LIMIT)


def _layer_norm(y, g, b):
    mu = jnp.mean(y, axis=-1, keepdims=True)
    yc = y - mu
    var = jnp.mean(yc * yc, axis=-1, keepdims=True)
    return yc * lax.rsqrt(var + LN_EPS) * g + b


def _gelu_tanh(x):
    return 0.5 * x * (1.0 + jnp.tanh(math.sqrt(2.0 / math.pi) * (x + 0.044715 * (x * x * x))))


def _mixer_in_kernel(x_ref, w_ref, b_ref, lng_ref, lnb_ref, ws_ref, bs_ref,
                     q_ref, k_ref, v_ref, gm_ref):
    x = x_ref[...].astype(BF16)

    def proj(lo, hi):
        return jnp.dot(x, w_ref[:, lo:hi], preferred_element_type=F32) + b_ref[:, lo:hi]

    a = ATTN_WIDTH
    for i, ref in enumerate((q_ref, k_ref, v_ref)):
        t = proj(i * a, (i + 1) * a)
        if i == 0:
            t = t * (HEAD_DIM ** -0.5)
        for s in range(a // LANES):
            ref[s] = t[:, s * LANES:(s + 1) * LANES]
    gu =_gelu_tanh(proj(3 * a, 3 * a + GMLP_WIDTH))
    gv = _gelu_tanh(proj(3 * a + GMLP_WIDTH, IN_WIDTH))
    vn = _layer_norm(gv, lng_ref[...], lnb_ref[...]).astype(BF16)

    rows = x_ref.shape[0]
    low_half = lax.broadcasted_iota(I32, (1, LANES), 1) < HEAD_DIM
    for c in range(rows // GMLP_CHUNK):
        r0 = c * GMLP_CHUNK
        for s in range(GMLP_WIDTH // LANES):
            c0 = s * LANES
            vs = vn[r0:r0 + GMLP_CHUNK, c0:c0 + LANES]
            zero = jnp.zeros_like(vs)
            rhs = jnp.concatenate([jnp.where(low_half, vs, zero), jnp.where(low_half, zero, vs)], axis=0)
            vm = jnp.dot(ws_ref[s], rhs, preferred_element_type=F32) + bs_ref[:, c0:c0 + LANES]
            gm_ref[r0:r0 + GMLP_CHUNK, c0:c0 + LANES] = (gu[r0:r0 + GMLP_CHUNK, c0:c0 + LANES] * vm).astype(BF16)


def _mixer_in(x, w_in, b_in, ln_g, ln_b, ws_pairs, bs_full):
    B, L, D = x.shape
    tm = min(ROW_TILE, L)
    row = lambda w: pl.BlockSpec((None, tm, w), lambda b, i: (b, i, 0))
    full = lambda shape: pl.BlockSpec(shape, lambda b, i: (0,) * len(shape))
    n_slab = ATTN_WIDTH // LANES
    slab = pl.BlockSpec((None, n_slab, tm, LANES), lambda b, i: (b, 0, i, 0))
    slab_shape = jax.ShapeDtypeStruct((B, n_slab, L, LANES), F32)
    return pl.pallas_call(
        _mixer_in_kernel,
        grid=(B, L // tm),
        in_specs=[row(D), full(w_in.shape), full(b_in.shape), full(ln_g.shape), full(ln_b.shape),
                  full(ws_pairs.shape), full(bs_full.shape)],
        out_specs=[slab, slab, slab, row(GMLP_WIDTH)],
        out_shape=[slab_shape] * 3 + [jax.ShapeDtypeStruct((B, L, GMLP_WIDTH), BF16)],
        compiler_params=_params("parallel", "parallel"),
        name="mixer_in",
    )(x, w_in, b_in, ln_g, ln_b, ws_pairs, bs_full)


def _attn_kernel(b16_ref, b4_ref, b1_ref, q_ref, k_ref, v_ref, out_ref, o_acc, m_acc, l_acc, *, seq_len):
    span = out_ref.shape[0]
    span_idx = pl.program_id(2)
    n_slab = HG_WIDTH // LANES
    lane = lax.broadcasted_iota(I32, (1, HG_WIDTH), 1)
    head_masks = [(lane >= h * HEAD_DIM) & (lane < (h + 1) * HEAD_DIM) for h in range(HEAD_GROUP)]

    def rows(ref, start, size, stride):
        idx = pl.ds(start, size) if stride == 1 else pl.ds(start, size, stride=stride)
        return jnp.concatenate([ref[s, idx, :] for s in range(n_slab)], axis=1)

    def put(ref, start, stride, val):
        idx = pl.ds(start, Q_TILE) if stride == 1 else pl.ds(start, Q_TILE, stride=stride)
        for s in range(n_slab):
            ref[s, idx, :] = val[:, s * LANES:(s + 1) * LANES]

    def per_head(cols):
        out = jnp.zeros((Q_TILE, HG_WIDTH), F32)
        for h in range(HEAD_GROUP):
            out = jnp.where(head_masks[h], cols[h * Q_TILE:(h + 1) * Q_TILE], out)
        return out

    branches = ((16, b16_ref), (4, b4_ref), (1, b1_ref))
    for bi, (d, bias_ref) in enumerate(branches):
        lr = seq_len // d
        tk = min(2 * Q_TILE, lr)
        tiles_total = lr // Q_TILE
        tiles_per_residue = span // d // Q_TILE
        first, final = bi == 0, bi == len(branches) - 1

        def tile(idx, carry, d=d, bias_ref=bias_ref, lr=lr, tk=tk, tiles_total=tiles_total,
                 tiles_per_residue=tiles_per_residue, first=first, final=final):
            r = idx // tiles_per_residue
            jt = idx % tiles_per_residue
            jg = span_idx * tiles_per_residue + jt
            start = jnp.clip(jg * Q_TILE - HALF_WINDOW, 0, lr - tk)
            variant = jnp.where(jg == 0, 0, jnp.where(jg == tiles_total - 1, 2, 1))
            q_row = r + d * (jt * Q_TILE)
            q = rows(q_ref, q_row, Q_TILE, d).astype(BF16)
            kw = rows(k_ref, r + d * start, tk, d).astype(BF16)
            vw = rows(v_ref, r + d * start, tk, d).astype(BF16)
            qs = jnp.concatenate([jnp.where(hm, q, jnp.zeros_like(q)) for hm in head_masks], axis=0)
            s = lax.dot_general(qs, kw, (((1,), (1,)), ((), ())), preferred_element_type=F32)
            s = s + bias_ref[variant]
            m = jnp.max(s, axis=1, keepdims=True)
            p = jnp.exp(s - m)
            l = jnp.sum(p, axis=1, keepdims=True)
            pv = jnp.dot(p.astype(BF16), vw, preferred_element_type=F32)
            o_t = jnp.zeros((Q_TILE, HG_WIDTH), F32)
            for h in range(HEAD_GROUP):
                o_t = jnp.where(head_masks[h], pv[h * Q_TILE:(h + 1) * Q_TILE], o_t)
            m_t, l_t = per_head(m), per_head(l)
            if not first:
                m_old = rows(m_acc, q_row, Q_TILE, d)
                m_new = jnp.maximum(m_old, m_t)
                a_old, a_t = jnp.exp(m_old - m_new), jnp.exp(m_t - m_new)
                l_t = a_old * rows(l_acc, q_row, Q_TILE, d) + a_t * l_t
                o_t = a_old * rows(o_acc, q_row, Q_TILE, d) + a_t * o_t
                m_t = m_new
            if final:
                out_ref[pl.ds(pl.multiple_of(q_row, Q_TILE), Q_TILE), :] = (o_t / l_t).astype(BF16)
            else:
                put(o_acc, q_row, d, o_t)
                put(m_acc, q_row, d, m_t)
                put(l_acc, q_row, d, l_t)
            return carry

        lax.fori_loop(0, span // Q_TILE, tile, 0, unroll=ATTN_UNROLL)


def _t5_bucket(rel):
    half = N_BUCKETS // 2
    ret = np.where(rel > 0, half, 0)
    n = np.abs(rel)
    max_exact = half // 2
    large = max_exact + (np.log(np.maximum(n, 1) / max_exact) / np.log(REL_MAX_DISTANCE / max_exact)
                         * (half - max_exact)).astype(np.int32)
    large = np.minimum(large, half - 1)
    return (ret + np.where(n < max_exact, n, large)).astype(np.int32)


def _bias_tables(rel_table, dilation, tk):
    buckets, valids = [], []
    for delta in (0, -HALF_WINDOW, Q_TILE - tk):
        off = delta + np.arange(tk)[None, :] - np.arange(Q_TILE)[:, None]
        valids.append(np.abs(off) <= HALF_WINDOW)
        buckets.append(_t5_bucket(np.clip(off, -HALF_WINDOW, HALF_WINDOW) * dilation))
    bucket = jnp.asarray(np.stack(buckets), I32)
    onehot = (bucket[..., None] == jnp.arange(N_BUCKETS, dtype=I32)).astype(F32)
    bias = jnp.einsum("vqkb,bh->hvqk", onehot, rel_table, precision=lax.Precision.HIGHEST)
    bias = jnp.where(jnp.asarray(np.stack(valids))[None], bias, MASK_VALUE)
    n_hg = N_HEADS // HEAD_GROUP
    bias = bias.reshape(n_hg, HEAD_GROUP, 3, Q_TILE, tk).transpose(0, 2, 1, 3, 4)
    return bias.reshape(n_hg, 3, HEAD_GROUP * Q_TILE, tk)


def _attention(q, k, v, rel_table):
    B, n_slab_total, L, _ = q.shape
    span = min(L, 2048)
    n_hg = N_HEADS // HEAD_GROUP
    slabs = HG_WIDTH // LANES
    biases = [_bias_tables(rel_table, d, min(2 * Q_TILE, L // d)) for d in (16, 4, 1)]
    once = pl.Buffered(1)
    bias_spec = lambda t: pl.BlockSpec((None,) + t.shape[1:], lambda g, b, s: (g, 0, 0, 0), pipeline_mode=once)
    seq = pl.BlockSpec((None, slabs, L, LANES), lambda g, b, s: (b, g, 0, 0), pipeline_mode=once)
    return pl.pallas_call(
        functools.partial(_attn_kernel, seq_len=L),
        grid=(n_hg, B, L // span),
        in_specs=[bias_spec(t) for t in biases]
                 + [pl.BlockSpec((None, slabs, span, LANES), lambda g, b, s: (b, g, s, 0)), seq, seq],
        out_specs=pl.BlockSpec((None, span, HG_WIDTH), lambda g, b, s: (b, s, g)),
        out_shape=jax.ShapeDtypeStruct((B, L, ATTN_WIDTH), BF16),
        scratch_shapes=[pltpu.VMEM((slabs, span, LANES), F32)] * 3,
        compiler_params=_params("parallel", "parallel", "arbitrary"),
        name="attention",
    )(*biases, q, k, v)


def _mixer_out_kernel(x_ref, attn_ref, gm_ref, wo_ref, g_ref, b_ref, wrh_ref, wrl_ref,
                      x1_ref, x1p_ref, aff_ref):
    mix = jnp.dot(attn_ref[...], wo_ref[0:ATTN_WIDTH, :], preferred_element_type=F32)
    mix = mix + jnp.dot(gm_ref[...], wo_ref[ATTN_WIDTH:, :], preferred_element_type=F32)
    x1 = _layer_norm(ALPHA * x_ref[...] + mix, g_ref[...], b_ref[...])
    x1_ref[...] = x1
    hi = x1.astype(BF16)
    x1p_ref[...] = _pack_bf16_pairs(x1)
    lo = (x1 - hi.astype(F32)).astype(BF16)
    nt = (((1,), (1,)), ((), ()))
    logits = (lax.dot_general(wrh_ref[...], hi, nt, preferred_element_type=F32)
              + lax.dot_general(wrl_ref[...], hi, nt, preferred_element_type=F32)
              + lax.dot_general(wrh_ref[...], lo, nt, preferred_element_type=F32))
    m = jnp.max(logits, axis=0, keepdims=True)
    e = jnp.exp(logits - m)
    aff_ref[...] = e / jnp.sum(e, axis=0, keepdims=True)


def _mixer_out(x, attn, gm, w_out, ln_g, ln_b, wr_hi, wr_lo):
    B, L, D = x.shape
    tm = min(ROW_TILE, L)
    per_seq = L // tm
    row = lambda w: pl.BlockSpec((None, tm, w), lambda b, i: (b, i, 0))
    full = lambda shape: pl.BlockSpec(shape, lambda b, i: (0,) * len(shape))
    return pl.pallas_call(
        _mixer_out_kernel,
        grid=(B, per_seq),
        in_specs=[row(D), row(ATTN_WIDTH), row(GMLP_WIDTH)]
                 + [full(w_out.shape), full(ln_g.shape), full(ln_b.shape), full(wr_hi.shape), full(wr_lo.shape)],
        out_specs=[row(D), row(D // 2), pl.BlockSpec((N_EXPERTS, tm), lambda b, i: (0, b * per_seq + i))],
        out_shape=[jax.ShapeDtypeStruct((B, L, D), F32), jax.ShapeDtypeStruct((B, L, D // 2), I32),
                   jax.ShapeDtypeStruct((N_EXPERTS, B * L), F32)],
        compiler_params=_params("parallel", "parallel"),
        name="mixer_out",
    )(x, attn, gm, w_out, ln_g, ln_b, wr_hi, wr_lo)


def _threshold_kernel(aff_ref, tau_ref, need_ref, *, cap):
    bits = lax.bitcast_convert_type(aff_ref[...], I32)

    def step(i, tau):
        cand = tau | jnp.left_shift(jnp.int32(1), 30 - i)
        cnt = jnp.sum((bits >= cand).astype(F32), axis=1, keepdims=True)
        return jnp.where(cnt >= cap, cand, tau)

    tau = lax.fori_loop(0, 31, step, jnp.zeros((bits.shape[0], 1), I32))
    above = jnp.sum((bits > tau).astype(F32), axis=1, keepdims=True)
    tau_ref[...] = jnp.broadcast_to(tau, tau_ref.shape)
    need_ref[...] = jnp.broadcast_to(cap - above, need_ref.shape)


def _slot_kernel(aff_ref, tau_ref, need_ref, tri_ref, low_ref, slot_ref, rank_ref, span_ref, carry_sel, carry_eq):
    @pl.when(pl.program_id(0) == 0)
    def _():
        carry_sel[...] = jnp.zeros_like(carry_sel)
        carry_eq[...] = jnp.zeros_like(carry_eq)

    tau = tau_ref[:, 0:1]
    need = need_ref[:, 0:1]
    c_sel = carry_sel[:, 0:1]
    c_eq = carry_eq[:, 0:1]
    tri = tri_ref[...]
    for s in range(aff_ref.shape[1] // SCAN_TILE):
        sl = slice(s * SCAN_TILE, (s + 1) * SCAN_TILE)
        bits = lax.bitcast_convert_type(aff_ref[:, sl], I32)
        eq = (bits == tau).astype(F32)
        eq_incl = jnp.dot(eq.astype(BF16), tri, preferred_element_type=F32)
        tie_taken = (eq_incl - eq + c_eq) < need
        sel = jnp.where((bits > tau) | ((bits == tau) & tie_taken), 1.0, 0.0)
        sel_b = sel.astype(BF16)
        incl = jnp.dot(sel_b, tri, preferred_element_type=F32)
        before = (incl - sel + c_sel).astype(I32)
        slot_ref[:, sl] = jnp.where(sel > 0.0, before, -1)
        per_token = jnp.sum(sel, axis=0, keepdims=True)
        first = jnp.sum(incl, axis=0, keepdims=True) - per_token + jnp.sum(c_sel, axis=0, keepdims=True)
        lower = jnp.dot(low_ref[...], sel_b, preferred_element_type=F32)
        rank_ref[:, sl] = (first + lower).astype(I32)
        span_ref[0:1, sl] = first
        span_ref[1:2, sl] = first + per_token
        c_sel = c_sel + incl[:, SCAN_TILE - 1:SCAN_TILE]
        c_eq = c_eq + eq_incl[:, SCAN_TILE - 1:SCAN_TILE]
    carry_sel[...] = jnp.broadcast_to(c_sel, carry_sel.shape)
    carry_eq[...] = jnp.broadcast_to(c_eq, carry_eq.shape)


def _select(aff_t, cap):
    E, n = aff_t.shape
    stat = jax.ShapeDtypeStruct((E, LANES), I32)
    tau, need = pl.pallas_call(
        functools.partial(_threshold_kernel, cap=float(cap)),
        out_shape=[stat, jax.ShapeDtypeStruct((E, LANES), F32)],
        compiler_params=pltpu.CompilerParams(vmem_limit_bytes=VMEM_LIMIT),
        name="threshold",
    )(aff_t)
    tri = jnp.asarray(np.triu(np.ones((SCAN_TILE, SCAN_TILE), np.float32)), BF16)
    low = jnp.asarray(np.tril(np.ones((E, E), np.float32), -1), BF16)
    tb = min(SLOT_BLOCK, n)
    blk = pl.BlockSpec((E, tb), lambda i: (0, i))
    const = lambda shape: pl.BlockSpec(shape, lambda i: (0, 0))
    return pl.pallas_call(
        _slot_kernel,
        grid=(n // tb,),
        in_specs=[blk, const((E, LANES)), const((E, LANES)), const((SCAN_TILE, SCAN_TILE)), const((E, E))],
        out_specs=[blk, blk, pl.BlockSpec((2, tb), lambda i: (0, i))],
        out_shape=[jax.ShapeDtypeStruct((E, n), I32)] * 2 + [jax.ShapeDtypeStruct((2, n), F32)],
        scratch_shapes=[pltpu.VMEM((E, LANES), F32), pltpu.VMEM((E, LANES), F32)],
        compiler_params=_params("arbitrary"),
        name="slots",
    )(aff_t, tau, need, tri, low)


def _work_list(span, total_rows, t_chunk, t_rows):
    n = span.shape[1]
    nch, ntl = n // t_chunk, total_rows // t_rows
    kmax = N_EXPERTS * t_chunk // t_rows + 1
    base = span[0, ::t_chunk].astype(I32)
    end = jnp.concatenate([base[1:], jnp.full((1,), total_rows, I32)])
    first = jnp.minimum(base // t_rows, ntl - 1)
    last = jnp.where(end > base, (end - 1) // t_rows, first)
    tile = first[:, None] + jnp.arange(kmax, dtype=I32)
    valid = (tile <= last[:, None]).reshape(-1)
    tile = jnp.minimum(tile, ntl - 1).reshape(-1)
    chunk = jnp.repeat(jnp.arange(nch, dtype=I32), kmax)
    length = nch + ntl
    count = jnp.sum(valid.astype(I32))
    idx = jnp.nonzero(valid, size=length, fill_value=0)[0].astype(I32)
    ar = jnp.arange(length, dtype=I32)
    live = ar < count
    idx = jnp.where(live, idx, idx[count - 1])
    c, t = chunk[idx], tile[idx]
    prev = jnp.concatenate([jnp.full((1,), -1, I32), c[:-1]])
    nxt = jnp.concatenate([c[1:], jnp.full((1,), -1, I32)])
    is_first = live & ((ar == 0) | (c != prev))
    is_last = live & ((ar == count - 1) | (c != nxt))
    return c, t, live.astype(I32) + 2 * is_first.astype(I32) + 4 * is_last.astype(I32)


def _pack_bf16_pairs(x):
    bits = lax.bitcast_convert_type(x.astype(BF16).astype(F32), I32)
    half = bits.shape[1] // 2
    return lax.shift_right_logical(bits[:, :half], 16) | bits[:, half:]


def _unpack_bf16_pairs(w):
    left = lax.bitcast_convert_type(lax.shift_left(w, 16), F32)
    right = lax.bitcast_convert_type(w & jnp.int32(-65536), F32)
    return jnp.concatenate([left, right], axis=1).astype(BF16)


def _sc_layout(n_experts, cap):
    info = plsc.get_sparse_core_info()
    per_expert = info.num_cores * info.num_subcores // n_experts
    return info.num_cores, info.num_lanes, per_expert, cap // per_expert


def _sc_worker(n_cores, per_expert):
    wid = lax.axis_index("s") * n_cores + lax.axis_index("c")
    return wid // per_expert, wid % per_expert


def _sc_invert(slot_hbm, payload_hbms, slot_v, payload_vs, e, part, share, n, chunk, lanes, store):
    @pl.loop(0, n // chunk)
    def _(ci):
        pltpu.sync_copy(slot_hbm.at[e, pl.ds(ci * chunk, chunk)], slot_v)
        for src, dst in zip(payload_hbms, payload_vs):
            pltpu.sync_copy(src.at[e, pl.ds(ci * chunk, chunk)], dst)

        @pl.loop(0, chunk // lanes)
        def _(i):
            local = slot_v[pl.ds(i * lanes, lanes)] - part * share
            mine = (local >= 0) & (local < share)
            tok = ci * chunk + i * lanes + lax.iota(I32, lanes)
            store(local, mine, tok, [v[pl.ds(i * lanes, lanes)] for v in payload_vs])


_SC_PARAMS = dataclasses.replace(pltpu.CompilerParams(), needs_layout_passes=False)


def _dispatch(x1p, slot, aff_t, cap):
    n, width = x1p.shape
    E = slot.shape[0]
    n_cores, lanes, per_expert, share = _sc_layout(E, cap)
    chunk = min(n, SC_SLOT_CHUNK)
    mesh = plsc.VectorSubcoreMesh(core_axis_name="c", subcore_axis_name="s")

    @functools.partial(
        pl.kernel, mesh=mesh, compiler_params=_SC_PARAMS,
        out_type=[jax.ShapeDtypeStruct((E * cap, width), I32), jax.ShapeDtypeStruct((E * cap,), F32)],
        scratch_types=[pltpu.VMEM((chunk,), I32), pltpu.VMEM((chunk,), F32), pltpu.VMEM((share,), I32),
                       pltpu.VMEM((share,), F32), pltpu.VMEM((SC_ROWS, width), I32), pltpu.SemaphoreType.DMA],
    )
    def gather(x_hbm, slot_hbm, aff_hbm, out_hbm, gate_hbm, slot_v, aff_v, idx_v, gate_v, rows_v, sem):
        e, part = _sc_worker(n_cores, per_expert)

        def store(local, mine, tok, payloads):
            plsc.store_scatter(idx_v, [local], tok, mask=mine)
            plsc.store_scatter(gate_v, [local], payloads[0], mask=mine)

        _sc_invert(slot_hbm, [aff_hbm], slot_v, [aff_v], e, part, share, n, chunk, lanes, store)
        base = e * cap + part * share
        pltpu.sync_copy(gate_v, gate_hbm.at[pl.ds(base, share)])

        @pl.loop(0, share // SC_ROWS)
        def _(j):
            pltpu.async_copy(x_hbm.at[idx_v.at[pl.ds(j * SC_ROWS, SC_ROWS)]], rows_v, sem).wait()
            pltpu.sync_copy(rows_v, out_hbm.at[pl.ds(base + j * SC_ROWS, SC_ROWS)])

    xs, gates = gather(x1p, slot, aff_t)
    return xs.reshape(E, cap, width), gates.reshape(E, cap // LANES, LANES)


def _to_token_order(ysp, slot, rank):
    E, cap, width = ysp.shape
    n = slot.shape[1]
    n_cores, lanes, per_expert, share = _sc_layout(E, cap)
    chunk = min(n, SC_SLOT_CHUNK)
    mesh = plsc.VectorSubcoreMesh(core_axis_name="c", subcore_axis_name="s")

    @functools.partial(
        pl.kernel, mesh=mesh, compiler_params=_SC_PARAMS,
        out_type=jax.ShapeDtypeStruct((E * cap, width), I32),
        scratch_types=[pltpu.VMEM((chunk,), I32), pltpu.VMEM((chunk,), I32),
                       pltpu.VMEM((share // SC_ROWS, SC_ROWS), I32),
                       pltpu.VMEM((SC_ROWS, width), I32), pltpu.SemaphoreType.DMA],
    )
    def scatter(y_hbm, slot_hbm, rank_hbm, z_hbm, slot_v, rank_v, dest_v, rows_v, sem):
        e, part = _sc_worker(n_cores, per_expert)

        def store(local, mine, tok, payloads):
            plsc.store_scatter(dest_v, [local // SC_ROWS, local % SC_ROWS], payloads[0], mask=mine)

        _sc_invert(slot_hbm, [rank_hbm], slot_v, [rank_v], e, part, share, n, chunk, lanes, store)
        base = e * cap + part * share

        @pl.loop(0, share // SC_ROWS)
        def _(j):
            pltpu.sync_copy(y_hbm.at[pl.ds(base + j * SC_ROWS, SC_ROWS)], rows_v)
            pltpu.async_copy(rows_v, z_hbm.at[dest_v.at[j]], sem).wait()

    return scatter(ysp.reshape(E * cap, width), slot, rank)


def _ffn_kernel(x_ref, gate_ref, wg_ref, wu_ref, wd_ref, y_ref, acc_ref, xb_ref):
    f = pl.program_id(2)

    @pl.when(f == 0)
    def _():
        acc_ref[...] = jnp.zeros_like(acc_ref)
        xb_ref[...] = _unpack_bf16_pairs(x_ref[...])

    x = xb_ref[...]
    g = jnp.dot(x, wg_ref[...], preferred_element_type=F32)
    u = jnp.dot(x, wu_ref[...], preferred_element_type=F32)
    h = (g * jax.nn.sigmoid(g) * u).astype(BF16)
    acc_ref[...] += jnp.dot(h, wd_ref[...], preferred_element_type=F32)

    @pl.when(f == pl.num_programs(2) - 1)
    def _():
        gates = gate_ref[...]
        pad = jnp.zeros((LANES - gates.shape[0], LANES), F32)
        cols = jnp.concatenate([gates, pad], axis=0).T
        for j in range(gates.shape[0]):
            rows = slice(j * LANES, (j + 1) * LANES)
            y_ref[rows, :] = _pack_bf16_pairs(acc_ref[rows, :] * cols[:, j:j + 1])


def _experts(xs, gates, w_gate, w_up, w_down):
    E, cap, packed = xs.shape
    D = 2 * packed
    tm = min(FFN_ROWS, cap)
    tf = FFN_COLS
    return pl.pallas_call(
        _ffn_kernel,
        grid=(E, cap // tm, D_FF // tf),
        in_specs=[pl.BlockSpec((None, tm, packed), lambda e, m, f: (e, m, 0)),
                  pl.BlockSpec((None, tm // LANES, LANES), lambda e, m, f: (e, m, 0)),
                  pl.BlockSpec((None, D, tf), lambda e, m, f: (e, 0, f)),
                  pl.BlockSpec((None, D, tf), lambda e, m, f: (e, 0, f)),
                  pl.BlockSpec((None, tf, D), lambda e, m, f: (e, f, 0))],
        out_specs=pl.BlockSpec((None, tm, packed), lambda e, m, f: (e, m, 0)),
        out_shape=jax.ShapeDtypeStruct((E, cap, packed), I32),
        scratch_shapes=[pltpu.VMEM((tm, D), F32), pltpu.VMEM((tm, D), BF16)],
        compiler_params=_params("parallel", "parallel", "arbitrary"),
        name="experts",
    )(xs, gates, w_gate, w_up, w_down)


def _combine_kernel(wc_ref, wt_ref, wf_ref, z_ref, span_ref, x1_ref, g_ref, b_ref, o_ref):
    w = pl.program_id(0)
    flags = wf_ref[w]

    @pl.when((flags & 2) == 2)
    def _():
        o_ref[...] = jnp.zeros_like(o_ref)

    @pl.when((flags & 1) == 1)
    def _():
        tr, tc = z_ref.shape[0], o_ref.shape[0]
        row = (lax.broadcasted_iota(I32, (tc, tr), 1) + wt_ref[w] * tr).astype(F32)
        owns = ((row >= span_ref[:, 0:1]) & (row < span_ref[:, 1:2])).astype(BF16)
        o_ref[...] += jnp.dot(owns, _unpack_bf16_pairs(z_ref[...]), preferred_element_type=F32)

    @pl.when((flags & 4) == 4)
    def _():
        o_ref[...] = _layer_norm(ALPHA * x1_ref[...] + o_ref[...], g_ref[...], b_ref[...])


def _combine(z, span_tm, x1, ln_g, ln_b, work):
    rows, packed = z.shape
    n, D = x1.shape
    length = work[0].shape[0]
    chunk = lambda w, wc, wt, wf: (wc[w], 0)
    const = lambda w, wc, wt, wf: (0, 0)
    return pl.pallas_call(
        _combine_kernel,
        grid_spec=pltpu.PrefetchScalarGridSpec(
            num_scalar_prefetch=3,
            grid=(length,),
            in_specs=[pl.BlockSpec((COMBINE_ROWS, packed), lambda w, wc, wt, wf: (wt[w], 0)),
                      pl.BlockSpec((COMBINE_CHUNK, 2), chunk),
                      pl.BlockSpec((COMBINE_CHUNK, D), chunk),
                      pl.BlockSpec((1, D), const), pl.BlockSpec((1, D), const)],
            out_specs=pl.BlockSpec((COMBINE_CHUNK, D), chunk),
        ),
        out_shape=jax.ShapeDtypeStruct((n, D), F32),
        compiler_params=_params("arbitrary"),
        name="combine",
    )(*work, z, span_tm, x1, ln_g, ln_b)


def _prepare_weights(rel_bias_table, w_in, b_in, gmlp_ln_g, gmlp_ln_b, gmlp_w_s, gmlp_b_s, w_out,
                     ln1_g, ln1_b, w_router, w_gate, w_up, w_down, ln2_g, ln2_b):
    row = lambda t: t[0].reshape(1, -1).astype(F32)
    ws = gmlp_w_s[0].astype(BF16)
    ws_pairs = jnp.concatenate([ws[0::2], ws[1::2]], axis=-1)
    bs_full = jnp.repeat(gmlp_b_s[0].T, HEAD_DIM, axis=1).astype(F32)
    wr_t = w_router[0].T.astype(F32)
    wr_hi = wr_t.astype(BF16)
    wr_lo = (wr_t - wr_hi.astype(F32)).astype(BF16)
    return dict(
        rel=rel_bias_table.astype(F32), w_in=w_in[0].astype(BF16), b_in=row(b_in),
        gln_g=row(gmlp_ln_g), gln_b=row(gmlp_ln_b), ws_pairs=ws_pairs, bs_full=bs_full,
        w_out=w_out[0].astype(BF16), ln1_g=row(ln1_g), ln1_b=row(ln1_b), wr_hi=wr_hi, wr_lo=wr_lo,
        w_gate=w_gate[0].astype(BF16), w_up=w_up[0].astype(BF16), w_down=w_down[0].astype(BF16),
        ln2_g=row(ln2_g), ln2_b=row(ln2_b))


def _trunk(x, p):
    B, L, D = x.shape
    n = B * L
    cap = CAPACITY_FACTOR * n // N_EXPERTS
    q, k, v, gm = _mixer_in(x, p["w_in"], p["b_in"], p["gln_g"], p["gln_b"], p["ws_pairs"], p["bs_full"])
    attn = _attention(q, k, v, p["rel"])
    x1, x1p, aff_t = _mixer_out(x, attn, gm, p["w_out"], p["ln1_g"], p["ln1_b"], p["wr_hi"], p["wr_lo"])
    slot, rank, span = _select(aff_t, cap)
    xs, gates = _dispatch(x1p.reshape(n, D // 2), slot, aff_t, cap)
    ys = _experts(xs, gates, p["w_gate"], p["w_up"], p["w_down"])
    z = _to_token_order(ys, slot, rank)
    work = _work_list(span, N_EXPERTS * cap, COMBINE_CHUNK, COMBINE_ROWS)
    y = _combine(z, span.T, x1.reshape(n, D), p["ln2_g"], p["ln2_b"], work)
    return y.reshape(B, L, D)


def kernel(x_prompt, x_sample, rel_bias_table, w_in, b_in, gmlp_ln_g, gmlp_ln_b, gmlp_w_s, gmlp_b_s, w_out,
           ln1_g, ln1_b, w_router, w_gate, w_up, w_down, ln2_g, ln2_b):
    p = _prepare_weights(rel_bias_table, w_in, b_in, gmlp_ln_g, gmlp_ln_b, gmlp_w_s, gmlp_b_s, w_out,
                         ln1_g, ln1_b, w_router, w_gate, w_up, w_down, ln2_g, ln2_b)
    return (_trunk(x_prompt, p), _trunk(x_sample, p))
```

```python
import dataclasses
import functools
import math

import numpy as np
import jax
import jax.numpy as jnp
from jax import lax
from jax.experimental import pallas as pl
from jax.experimental.pallas import tpu as pltpu
from jax.experimental.pallas import tpu_sc as plsc

F32 = jnp.float32
BF16 = jnp.bfloat16
I32 = jnp.int32

D_MODEL = 1024
HEAD_DIM = 64
ATTN_WIDTH = 512
GMLP_WIDTH = 512
N_HEADS = ATTN_WIDTH // HEAD_DIM
N_GROUPS = GMLP_WIDTH // HEAD_DIM
IN_WIDTH = 3 * ATTN_WIDTH + 2 * GMLP_WIDTH
GMLP_CHUNK = 128
BRANCHES = ((128, 1), (512, 4), (2048, 16))
HALF_WINDOW = 64
N_BUCKETS = 32
REL_MAX_DISTANCE = 1024
N_EXPERTS = 16
CAPACITY_FACTOR = 2
D_FF = 2816
ALPHA = 2.0 ** 0.25
LN_EPS = 1e-5
MASK_VALUE = -1e30

LANES = 128
MXU_DIM = 256
VMEM_LIMIT = 56 * 1024 * 1024

Q_TILE = 128
HEAD_GROUP = 4
HG_WIDTH = HEAD_GROUP * HEAD_DIM
ATTN_UNROLL = 4
ROW_TILE = 512
SCAN_TILE = 256
SLOT_BLOCK = 2048
SC_SLOT_CHUNK = 4096
SC_ROWS = 64
COMBINE_CHUNK = 512
COMBINE_ROWS = 512
FFN_ROWS = 1024
FFN_COLS = 256


def _params(*sem):
    return pltpu.CompilerParams(dimension_semantics=sem, vmem_limit_bytes=VMEM_LIMIT)


def _layer_norm(y, g, b):
    mu = jnp.mean(y, axis=-1, keepdims=True)
    yc = y - mu
    var = jnp.mean(yc * yc, axis=-1, keepdims=True)
    return yc * lax.rsqrt(var + LN_EPS) * g + b


def _gelu_tanh(x):
    return 0.5 * x * (1.0 + jnp.tanh(math.sqrt(2.0 / math.pi) * (x + 0.044715 * (x * x * x))))


def _mixer_in_kernel(x_ref, w_ref, b_ref, lng_ref, lnb_ref, ws_ref, bs_ref,
                     q_ref, k_ref, v_ref, gm_ref):
    x = x_ref[...].astype(BF16)

    def proj(lo, hi):
        return jnp.dot(x, w_ref[:, lo:hi], preferred_element_type=F32) + b_ref[:, lo:hi]

    a = ATTN_WIDTH
    for i, ref in enumerate((q_ref, k_ref, v_ref)):
        t = proj(i * a, (i + 1) * a)
        if i == 0:
            t = t * (HEAD_DIM ** -0.5)
        for s in range(a // LANES):
            ref[s] = t[:, s * LANES:(s + 1) * LANES]
    gu =_gelu_tanh(proj(3 * a, 3 * a + GMLP_WIDTH))
    gv = _gelu_tanh(proj(3 * a + GMLP_WIDTH, IN_WIDTH))
    vn = _layer_norm(gv, lng_ref[...], lnb_ref[...]).astype(BF16)

    rows = x_ref.shape[0]
    low_half = lax.broadcasted_iota(I32, (1, LANES), 1) < HEAD_DIM
    for c in range(rows // GMLP_CHUNK):
        r0 = c * GMLP_CHUNK
        for s in range(GMLP_WIDTH // LANES):
            c0 = s * LANES
            vs = vn[r0:r0 + GMLP_CHUNK, c0:c0 + LANES]
            zero = jnp.zeros_like(vs)
            rhs = jnp.concatenate([jnp.where(low_half, vs, zero), jnp.where(low_half, zero, vs)], axis=0)
            vm = jnp.dot(ws_ref[s], rhs, preferred_element_type=F32) + bs_ref[:, c0:c0 + LANES]
            gm_ref[r0:r0 + GMLP_CHUNK, c0:c0 + LANES] = (gu[r0:r0 + GMLP_CHUNK, c0:c0 + LANES] * vm).astype(BF16)


def _mixer_in(x, w_in, b_in, ln_g, ln_b, ws_pairs, bs_full):
    B, L, D = x.shape
    tm = min(ROW_TILE, L)
    row = lambda w: pl.BlockSpec((None, tm, w), lambda b, i: (b, i, 0))
    full = lambda shape: pl.BlockSpec(shape, lambda b, i: (0,) * len(shape))
    n_slab = ATTN_WIDTH // LANES
    slab = pl.BlockSpec((None, n_slab, tm, LANES), lambda b, i: (b, 0, i, 0))
    slab_shape = jax.ShapeDtypeStruct((B, n_slab, L, LANES), F32)
    return pl.pallas_call(
        _mixer_in_kernel,
        grid=(B, L // tm),
        in_specs=[row(D), full(w_in.shape), full(b_in.shape), full(ln_g.shape), full(ln_b.shape),
                  full(ws_pairs.shape), full(bs_full.shape)],
        out_specs=[slab, slab, slab, row(GMLP_WIDTH)],
        out_shape=[slab_shape] * 3 + [jax.ShapeDtypeStruct((B, L, GMLP_WIDTH), BF16)],
        compiler_params=_params("parallel", "parallel"),
        name="mixer_in",
    )(x, w_in, b_in, ln_g, ln_b, ws_pairs, bs_full)


def _attn_kernel(b16_ref, b4_ref, b1_ref, q_ref, k_ref, v_ref, out_ref, o_acc, m_acc, l_acc, *, seq_len):
    span = out_ref.shape[0]
    span_idx = pl.program_id(2)
    n_slab = HG_WIDTH // LANES
    lane = lax.broadcasted_iota(I32, (1, HG_WIDTH), 1)
    head_masks = [(lane >= h * HEAD_DIM) & (lane < (h + 1) * HEAD_DIM) for h in range(HEAD_GROUP)]

    def rows(ref, start, size, stride):
        idx = pl.ds(start, size) if stride == 1 else pl.ds(start, size, stride=stride)
        return jnp.concatenate([ref[s, idx, :] for s in range(n_slab)], axis=1)

    def put(ref, start, stride, val):
        idx = pl.ds(start, Q_TILE) if stride == 1 else pl.ds(start, Q_TILE, stride=stride)
        for s in range(n_slab):
            ref[s, idx, :] = val[:, s * LANES:(s + 1) * LANES]

    def per_head(cols):
        out = jnp.zeros((Q_TILE, HG_WIDTH), F32)
        for h in range(HEAD_GROUP):
            out = jnp.where(head_masks[h], cols[h * Q_TILE:(h + 1) * Q_TILE], out)
        return out

    branches = ((16, b16_ref), (4, b4_ref), (1, b1_ref))
    for bi, (d, bias_ref) in enumerate(branches):
        lr = seq_len // d
        tk = min(2 * Q_TILE, lr)
        tiles_total = lr // Q_TILE
        tiles_per_residue = span // d // Q_TILE
        first, final = bi == 0, bi == len(branches) - 1

        def tile(idx, carry, d=d, bias_ref=bias_ref, lr=lr, tk=tk, tiles_total=tiles_total,
                 tiles_per_residue=tiles_per_residue, first=first, final=final):
            r = idx // tiles_per_residue
            jt = idx % tiles_per_residue
            jg = span_idx * tiles_per_residue + jt
            start = jnp.clip(jg * Q_TILE - HALF_WINDOW, 0, lr - tk)
            variant = jnp.where(jg == 0, 0, jnp.where(jg == tiles_total - 1, 2, 1))
            q_row = r + d * (jt * Q_TILE)
            q = rows(q_ref, q_row, Q_TILE, d).astype(BF16)
            kw = rows(k_ref, r + d * start, tk, d).astype(BF16)
            vw = rows(v_ref, r + d * start, tk, d).astype(BF16)
            qs = jnp.concatenate([jnp.where(hm, q, jnp.zeros_like(q)) for hm in head_masks], axis=0)
            s = lax.dot_general(qs, kw, (((1,), (1,)), ((), ())), preferred_element_type=F32)
            s = s + bias_ref[variant]
            m = jnp.max(s, axis=1, keepdims=True)
            p = jnp.exp(s - m)
            l = jnp.sum(p, axis=1, keepdims=True)
            pv = jnp.dot(p.astype(BF16), vw, preferred_element_type=F32)
            o_t = jnp.zeros((Q_TILE, HG_WIDTH), F32)
            for h in range(HEAD_GROUP):
                o_t = jnp.where(head_masks[h], pv[h * Q_TILE:(h + 1) * Q_TILE], o_t)
            m_t, l_t = per_head(m), per_head(l)
            if not first:
                m_old = rows(m_acc, q_row, Q_TILE, d)
                m_new = jnp.maximum(m_old, m_t)
                a_old, a_t = jnp.exp(m_old - m_new), jnp.exp(m_t - m_new)
                l_t = a_old * rows(l_acc, q_row, Q_TILE, d) + a_t * l_t
                o_t = a_old * rows(o_acc, q_row, Q_TILE, d) + a_t * o_t
                m_t = m_new
            if final:
                out_ref[pl.ds(pl.multiple_of(q_row, Q_TILE), Q_TILE), :] = (o_t / l_t).astype(BF16)
            else:
                put(o_acc, q_row, d, o_t)
                put(m_acc, q_row, d, m_t)
                put(l_acc, q_row, d, l_t)
            return carry

        lax.fori_loop(0, span // Q_TILE, tile, 0, unroll=ATTN_UNROLL)


def _t5_bucket(rel):
    half = N_BUCKETS // 2
    ret = np.where(rel > 0, half, 0)
    n = np.abs(rel)
    max_exact = half // 2
    large = max_exact + (np.log(np.maximum(n, 1) / max_exact) / np.log(REL_MAX_DISTANCE / max_exact)
                         * (half - max_exact)).astype(np.int32)
    large = np.minimum(large, half - 1)
    return (ret + np.where(n < max_exact, n, large)).astype(np.int32)


def _bias_tables(rel_table, dilation, tk):
    buckets, valids = [], []
    for delta in (0, -HALF_WINDOW, Q_TILE - tk):
        off = delta + np.arange(tk)[None, :] - np.arange(Q_TILE)[:, None]
        valids.append(np.abs(off) <= HALF_WINDOW)
        buckets.append(_t5_bucket(np.clip(off, -HALF_WINDOW, HALF_WINDOW) * dilation))
    bucket = jnp.asarray(np.stack(buckets), I32)
    onehot = (bucket[..., None] == jnp.arange(N_BUCKETS, dtype=I32)).astype(F32)
    bias = jnp.einsum("vqkb,bh->hvqk", onehot, rel_table, precision=lax.Precision.HIGHEST)
    bias = jnp.where(jnp.asarray(np.stack(valids))[None], bias, MASK_VALUE)
    n_hg = N_HEADS // HEAD_GROUP
    bias = bias.reshape(n_hg, HEAD_GROUP, 3, Q_TILE, tk).transpose(0, 2, 1, 3, 4)
    return bias.reshape(n_hg, 3, HEAD_GROUP * Q_TILE, tk)


def _attention(q, k, v, rel_table):
    B, n_slab_total, L, _ = q.shape
    span = min(L, 2048)
    n_hg = N_HEADS // HEAD_GROUP
    slabs = HG_WIDTH // LANES
    biases = [_bias_tables(rel_table, d, min(2 * Q_TILE, L // d)) for d in (16, 4, 1)]
    once = pl.Buffered(1)
    bias_spec = lambda t: pl.BlockSpec((None,) + t.shape[1:], lambda g, b, s: (g, 0, 0, 0), pipeline_mode=once)
    seq = pl.BlockSpec((None, slabs, L, LANES), lambda g, b, s: (b, g, 0, 0), pipeline_mode=once)
    return pl.pallas_call(
        functools.partial(_attn_kernel, seq_len=L),
        grid=(n_hg, B, L // span),
        in_specs=[bias_spec(t) for t in biases]
                 + [pl.BlockSpec((None, slabs, span, LANES), lambda g, b, s: (b, g, s, 0)), seq, seq],
        out_specs=pl.BlockSpec((None, span, HG_WIDTH), lambda g, b, s: (b, s, g)),
        out_shape=jax.ShapeDtypeStruct((B, L, ATTN_WIDTH), BF16),
        scratch_shapes=[pltpu.VMEM((slabs, span, LANES), F32)] * 3,
        compiler_params=_params("parallel", "parallel", "arbitrary"),
        name="attention",
    )(*biases, q, k, v)


def _mixer_out_kernel(x_ref, attn_ref, gm_ref, wo_ref, g_ref, b_ref, wrh_ref, wrl_ref,
                      x1_ref, x1p_ref, aff_ref):
    mix = jnp.dot(attn_ref[...], wo_ref[0:ATTN_WIDTH, :], preferred_element_type=F32)
    mix = mix + jnp.dot(gm_ref[...], wo_ref[ATTN_WIDTH:, :], preferred_element_type=F32)
    x1 = _layer_norm(ALPHA * x_ref[...] + mix, g_ref[...], b_ref[...])
    x1_ref[...] = x1
    hi = x1.astype(BF16)
    x1p_ref[...] = _pack_bf16_pairs(x1)
    lo = (x1 - hi.astype(F32)).astype(BF16)
    nt = (((1,), (1,)), ((), ()))
    logits = (lax.dot_general(wrh_ref[...], hi, nt, preferred_element_type=F32)
              + lax.dot_general(wrl_ref[...], hi, nt, preferred_element_type=F32)
              + lax.dot_general(wrh_ref[...], lo, nt, preferred_element_type=F32))
    m = jnp.max(logits, axis=0, keepdims=True)
    e = jnp.exp(logits - m)
    aff_ref[...] = e / jnp.sum(e, axis=0, keepdims=True)


def _mixer_out(x, attn, gm, w_out, ln_g, ln_b, wr_hi, wr_lo):
    B, L, D = x.shape
    tm = min(ROW_TILE, L)
    per_seq = L // tm
    row = lambda w: pl.BlockSpec((None, tm, w), lambda b, i: (b, i, 0))
    full = lambda shape: pl.BlockSpec(shape, lambda b, i: (0,) * len(shape))
    return pl.pallas_call(
        _mixer_out_kernel,
        grid=(B, per_seq),
        in_specs=[row(D), row(ATTN_WIDTH), row(GMLP_WIDTH)]
                 + [full(w_out.shape), full(ln_g.shape), full(ln_b.shape), full(wr_hi.shape), full(wr_lo.shape)],
        out_specs=[row(D), row(D // 2), pl.BlockSpec((N_EXPERTS, tm), lambda b, i: (0, b * per_seq + i))],
        out_shape=[jax.ShapeDtypeStruct((B, L, D), F32), jax.ShapeDtypeStruct((B, L, D // 2), I32),
                   jax.ShapeDtypeStruct((N_EXPERTS, B * L), F32)],
        compiler_params=_params("parallel", "parallel"),
        name="mixer_out",
    )(x, attn, gm, w_out, ln_g, ln_b, wr_hi, wr_lo)


def _threshold_kernel(aff_ref, tau_ref, need_ref, *, cap):
    bits = lax.bitcast_convert_type(aff_ref[...], I32)

    def step(i, tau):
        cand = tau | jnp.left_shift(jnp.int32(1), 30 - i)
        cnt = jnp.sum((bits >= cand).astype(F32), axis=1, keepdims=True)
        return jnp.where(cnt >= cap, cand, tau)

    tau = lax.fori_loop(0, 31, step, jnp.zeros((bits.shape[0], 1), I32))
    above = jnp.sum((bits > tau).astype(F32), axis=1, keepdims=True)
    tau_ref[...] = jnp.broadcast_to(tau, tau_ref.shape)
    need_ref[...] = jnp.broadcast_to(cap - above, need_ref.shape)


def _slot_kernel(aff_ref, tau_ref, need_ref, tri_ref, low_ref, slot_ref, rank_ref, span_ref, carry_sel, carry_eq):
    @pl.when(pl.program_id(0) == 0)
    def _():
        carry_sel[...] = jnp.zeros_like(carry_sel)
        carry_eq[...] = jnp.zeros_like(carry_eq)

    tau = tau_ref[:, 0:1]
    need = need_ref[:, 0:1]
    c_sel = carry_sel[:, 0:1]
    c_eq = carry_eq[:, 0:1]
    tri = tri_ref[...]
    for s in range(aff_ref.shape[1] // SCAN_TILE):
        sl = slice(s * SCAN_TILE, (s + 1) * SCAN_TILE)
        bits = lax.bitcast_convert_type(aff_ref[:, sl], I32)
        eq = (bits == tau).astype(F32)
        eq_incl = jnp.dot(eq.astype(BF16), tri, preferred_element_type=F32)
        tie_taken = (eq_incl - eq + c_eq) < need
        sel = jnp.where((bits > tau) | ((bits == tau) & tie_taken), 1.0, 0.0)
        sel_b = sel.astype(BF16)
        incl = jnp.dot(sel_b, tri, preferred_element_type=F32)
        before = (incl - sel + c_sel).astype(I32)
        slot_ref[:, sl] = jnp.where(sel > 0.0, before, -1)
        per_token = jnp.sum(sel, axis=0, keepdims=True)
        first = jnp.sum(incl, axis=0, keepdims=True) - per_token + jnp.sum(c_sel, axis=0, keepdims=True)
        lower = jnp.dot(low_ref[...], sel_b, preferred_element_type=F32)
        rank_ref[:, sl] = (first + lower).astype(I32)
        span_ref[0:1, sl] = first
        span_ref[1:2, sl] = first + per_token
        c_sel = c_sel + incl[:, SCAN_TILE - 1:SCAN_TILE]
        c_eq = c_eq + eq_incl[:, SCAN_TILE - 1:SCAN_TILE]
    carry_sel[...] = jnp.broadcast_to(c_sel, carry_sel.shape)
    carry_eq[...] = jnp.broadcast_to(c_eq, carry_eq.shape)


def _select(aff_t, cap):
    E, n = aff_t.shape
    stat = jax.ShapeDtypeStruct((E, LANES), I32)
    tau, need = pl.pallas_call(
        functools.partial(_threshold_kernel, cap=float(cap)),
        out_shape=[stat, jax.ShapeDtypeStruct((E, LANES), F32)],
        compiler_params=pltpu.CompilerParams(vmem_limit_bytes=VMEM_LIMIT),
        name="threshold",
    )(aff_t)
    tri = jnp.asarray(np.triu(np.ones((SCAN_TILE, SCAN_TILE), np.float32)), BF16)
    low = jnp.asarray(np.tril(np.ones((E, E), np.float32), -1), BF16)
    tb = min(SLOT_BLOCK, n)
    blk = pl.BlockSpec((E, tb), lambda i: (0, i))
    const = lambda shape: pl.BlockSpec(shape, lambda i: (0, 0))
    return pl.pallas_call(
        _slot_kernel,
        grid=(n // tb,),
        in_specs=[blk, const((E, LANES)), const((E, LANES)), const((SCAN_TILE, SCAN_TILE)), const((E, E))],
        out_specs=[blk, blk, pl.BlockSpec((2, tb), lambda i: (0, i))],
        out_shape=[jax.ShapeDtypeStruct((E, n), I32)] * 2 + [jax.ShapeDtypeStruct((2, n), F32)],
        scratch_shapes=[pltpu.VMEM((E, LANES), F32), pltpu.VMEM((E, LANES), F32)],
        compiler_params=_params("arbitrary"),
        name="slots",
    )(aff_t, tau, need, tri, low)


def _work_list(span, total_rows, t_chunk, t_rows):
    n = span.shape[1]
    nch, ntl = n // t_chunk, total_rows // t_rows
    kmax = N_EXPERTS * t_chunk // t_rows + 1
    base = span[0, ::t_chunk].astype(I32)
    end = jnp.concatenate([base[1:], jnp.full((1,), total_rows, I32)])
    first = jnp.minimum(base // t_rows, ntl - 1)
    last = jnp.where(end > base, (end - 1) // t_rows, first)
    tile = first[:, None] + jnp.arange(kmax, dtype=I32)
    valid = (tile <= last[:, None]).reshape(-1)
    tile = jnp.minimum(tile, ntl - 1).reshape(-1)
    chunk = jnp.repeat(jnp.arange(nch, dtype=I32), kmax)
    length = nch + ntl
    count = jnp.sum(valid.astype(I32))
    idx = jnp.nonzero(valid, size=length, fill_value=0)[0].astype(I32)
    ar = jnp.arange(length, dtype=I32)
    live = ar < count
    idx = jnp.where(live, idx, idx[count - 1])
    c, t = chunk[idx], tile[idx]
    prev = jnp.concatenate([jnp.full((1,), -1, I32), c[:-1]])
    nxt = jnp.concatenate([c[1:], jnp.full((1,), -1, I32)])
    is_first = live & ((ar == 0) | (c != prev))
    is_last = live & ((ar == count - 1) | (c != nxt))
    return c, t, live.astype(I32) + 2 * is_first.astype(I32) + 4 * is_last.astype(I32)


def _pack_bf16_pairs(x):
    bits = lax.bitcast_convert_type(x.astype(BF16).astype(F32), I32)
    half = bits.shape[1] // 2
    return lax.shift_right_logical(bits[:, :half], 16) | bits[:, half:]


def _unpack_bf16_pairs(w):
    left = lax.bitcast_convert_type(lax.shift_left(w, 16), F32)
    right = lax.bitcast_convert_type(w & jnp.int32(-65536), F32)
    return jnp.concatenate([left, right], axis=1).astype(BF16)


def _sc_layout(n_experts, cap):
    info = plsc.get_sparse_core_info()
    per_expert = info.num_cores * info.num_subcores // n_experts
    return info.num_cores, info.num_lanes, per_expert, cap // per_expert


def _sc_worker(n_cores, per_expert):
    wid = lax.axis_index("s") * n_cores + lax.axis_index("c")
    return wid // per_expert, wid % per_expert


def _sc_invert(slot_hbm, payload_hbms, slot_v, payload_vs, e, part, share, n, chunk, lanes, store):
    @pl.loop(0, n // chunk)
    def _(ci):
        pltpu.sync_copy(slot_hbm.at[e, pl.ds(ci * chunk, chunk)], slot_v)
        for src, dst in zip(payload_hbms, payload_vs):
            pltpu.sync_copy(src.at[e, pl.ds(ci * chunk, chunk)], dst)

        @pl.loop(0, chunk // lanes)
        def _(i):
            local = slot_v[pl.ds(i * lanes, lanes)] - part * share
            mine = (local >= 0) & (local < share)
            tok = ci * chunk + i * lanes + lax.iota(I32, lanes)
            store(local, mine, tok, [v[pl.ds(i * lanes, lanes)] for v in payload_vs])


_SC_PARAMS = dataclasses.replace(pltpu.CompilerParams(), needs_layout_passes=False)


def _dispatch(x1p, slot, aff_t, cap):
    n, width = x1p.shape
    E = slot.shape[0]
    n_cores, lanes, per_expert, share = _sc_layout(E, cap)
    chunk = min(n, SC_SLOT_CHUNK)
    mesh = plsc.VectorSubcoreMesh(core_axis_name="c", subcore_axis_name="s")

    @functools.partial(
        pl.kernel, mesh=mesh, compiler_params=_SC_PARAMS,
        out_type=[jax.ShapeDtypeStruct((E * cap, width), I32), jax.ShapeDtypeStruct((E * cap,), F32)],
        scratch_types=[pltpu.VMEM((chunk,), I32), pltpu.VMEM((chunk,), F32), pltpu.VMEM((share,), I32),
                       pltpu.VMEM((share,), F32), pltpu.VMEM((SC_ROWS, width), I32), pltpu.SemaphoreType.DMA],
    )
    def gather(x_hbm, slot_hbm, aff_hbm, out_hbm, gate_hbm, slot_v, aff_v, idx_v, gate_v, rows_v, sem):
        e, part = _sc_worker(n_cores, per_expert)

        def store(local, mine, tok, payloads):
            plsc.store_scatter(idx_v, [local], tok, mask=mine)
            plsc.store_scatter(gate_v, [local], payloads[0], mask=mine)

        _sc_invert(slot_hbm, [aff_hbm], slot_v, [aff_v], e, part, share, n, chunk, lanes, store)
        base = e * cap + part * share
        pltpu.sync_copy(gate_v, gate_hbm.at[pl.ds(base, share)])

        @pl.loop(0, share // SC_ROWS)
        def _(j):
            pltpu.async_copy(x_hbm.at[idx_v.at[pl.ds(j * SC_ROWS, SC_ROWS)]], rows_v, sem).wait()
            pltpu.sync_copy(rows_v, out_hbm.at[pl.ds(base + j * SC_ROWS, SC_ROWS)])

    xs, gates = gather(x1p, slot, aff_t)
    return xs.reshape(E, cap, width), gates.reshape(E, cap // LANES, LANES)


def _to_token_order(ysp, slot, rank):
    E, cap, width = ysp.shape
    n = slot.shape[1]
    n_cores, lanes, per_expert, share = _sc_layout(E, cap)
    chunk = min(n, SC_SLOT_CHUNK)
    mesh = plsc.VectorSubcoreMesh(core_axis_name="c", subcore_axis_name="s")

    @functools.partial(
        pl.kernel, mesh=mesh, compiler_params=_SC_PARAMS,
        out_type=jax.ShapeDtypeStruct((E * cap, width), I32),
        scratch_types=[pltpu.VMEM((chunk,), I32), pltpu.VMEM((chunk,), I32),
                       pltpu.VMEM((share // SC_ROWS, SC_ROWS), I32),
                       pltpu.VMEM((SC_ROWS, width), I32), pltpu.SemaphoreType.DMA],
    )
    def scatter(y_hbm, slot_hbm, rank_hbm, z_hbm, slot_v, rank_v, dest_v, rows_v, sem):
        e, part = _sc_worker(n_cores, per_expert)

        def store(local, mine, tok, payloads):
            plsc.store_scatter(dest_v, [local // SC_ROWS, local % SC_ROWS], payloads[0], mask=mine)

        _sc_invert(slot_hbm, [rank_hbm], slot_v, [rank_v], e, part, share, n, chunk, lanes, store)
        base = e * cap + part * share

        @pl.loop(0, share // SC_ROWS)
        def _(j):
            pltpu.sync_copy(y_hbm.at[pl.ds(base + j * SC_ROWS, SC_ROWS)], rows_v)
            pltpu.async_copy(rows_v, z_hbm.at[dest_v.at[j]], sem).wait()

    return scatter(ysp.reshape(E * cap, width), slot, rank)


def _ffn_kernel(x_ref, gate_ref, wg_ref, wu_ref, wd_ref, y_ref, acc_ref):
    x = _unpack_bf16_pairs(x_ref[...])
    for c in range(D_FF // FFN_COLS):
        cols = slice(c * FFN_COLS, (c + 1) * FFN_COLS)
        g = jnp.dot(x, wg_ref[:, cols], preferred_element_type=F32)
        u = jnp.dot(x, wu_ref[:, cols], preferred_element_type=F32)
        h = (g * jax.nn.sigmoid(g) * u).astype(BF16)
        part = jnp.dot(h, wd_ref[cols, :], preferred_element_type=F32)
        if c == 0:
            acc_ref[...] = part
        else:
            acc_ref[...] += part

    gates = gate_ref[...]
    pad = jnp.zeros((LANES - gates.shape[0], LANES), F32)
    scale = jnp.concatenate([gates, pad], axis=0).T
    for j in range(gates.shape[0]):
        rows = slice(j * LANES, (j + 1) * LANES)
        y_ref[rows, :] = _pack_bf16_pairs(acc_ref[rows, :] * scale[:, j:j + 1])


def _experts(xs, gates, w_gate, w_up, w_down):
    E, cap, packed = xs.shape
    D = 2 * packed
    tm = min(FFN_ROWS, cap)
    weight = lambda shape: pl.BlockSpec((None,) + shape, lambda e, m: (e, 0, 0), pipeline_mode=pl.Buffered(1))
    return pl.pallas_call(
        _ffn_kernel,
        grid=(E, cap // tm),
        in_specs=[pl.BlockSpec((None, tm, packed), lambda e, m: (e, m, 0)),
                  pl.BlockSpec((None, tm // LANES, LANES), lambda e, m: (e, m, 0)),
                  weight((D, D_FF)), weight((D, D_FF)), weight((D_FF, D))],
        out_specs=pl.BlockSpec((None, tm, packed), lambda e, m: (e, m, 0)),
        out_shape=jax.ShapeDtypeStruct((E, cap, packed), I32),
        scratch_shapes=[pltpu.VMEM((tm, D), F32)],
        compiler_params=_params("parallel", "arbitrary"),
        name="experts",
    )(xs, gates, w_gate, w_up, w_down)


def _combine_kernel(wc_ref, wt_ref, wf_ref, z_ref, span_ref, x1_ref, g_ref, b_ref, o_ref):
    w = pl.program_id(0)
    flags = wf_ref[w]

    @pl.when((flags & 2) == 2)
    def _():
        o_ref[...] = jnp.zeros_like(o_ref)

    @pl.when((flags & 1) == 1)
    def _():
        tr, tc = z_ref.shape[0], o_ref.shape[0]
        row = (lax.broadcasted_iota(I32, (tc, tr), 1) + wt_ref[w] * tr).astype(F32)
        owns = ((row >= span_ref[:, 0:1]) & (row < span_ref[:, 1:2])).astype(BF16)
        o_ref[...] += jnp.dot(owns, _unpack_bf16_pairs(z_ref[...]), preferred_element_type=F32)

    @pl.when((flags & 4) == 4)
    def _():
        o_ref[...] = _layer_norm(ALPHA * x1_ref[...] + o_ref[...], g_ref[...], b_ref[...])


def _combine(z, span_tm, x1, ln_g, ln_b, work):
    rows, packed = z.shape
    n, D = x1.shape
    length = work[0].shape[0]
    chunk = lambda w, wc, wt, wf: (wc[w], 0)
    const = lambda w, wc, wt, wf: (0, 0)
    return pl.pallas_call(
        _combine_kernel,
        grid_spec=pltpu.PrefetchScalarGridSpec(
            num_scalar_prefetch=3,
            grid=(length,),
            in_specs=[pl.BlockSpec((COMBINE_ROWS, packed), lambda w, wc, wt, wf: (wt[w], 0)),
                      pl.BlockSpec((COMBINE_CHUNK, 2), chunk),
                      pl.BlockSpec((COMBINE_CHUNK, D), chunk),
                      pl.BlockSpec((1, D), const), pl.BlockSpec((1, D), const)],
            out_specs=pl.BlockSpec((COMBINE_CHUNK, D), chunk),
        ),
        out_shape=jax.ShapeDtypeStruct((n, D), F32),
        compiler_params=_params("arbitrary"),
        name="combine",
    )(*work, z, span_tm, x1, ln_g, ln_b)


def _prepare_weights(rel_bias_table, w_in, b_in, gmlp_ln_g, gmlp_ln_b, gmlp_w_s, gmlp_b_s, w_out,
                     ln1_g, ln1_b, w_router, w_gate, w_up, w_down, ln2_g, ln2_b):
    row = lambda t: t[0].reshape(1, -1).astype(F32)
    ws = gmlp_w_s[0].astype(BF16)
    ws_pairs = jnp.concatenate([ws[0::2], ws[1::2]], axis=-1)
    bs_full = jnp.repeat(gmlp_b_s[0].T, HEAD_DIM, axis=1).astype(F32)
    wr_t = w_router[0].T.astype(F32)
    wr_hi = wr_t.astype(BF16)
    wr_lo = (wr_t - wr_hi.astype(F32)).astype(BF16)
    return dict(
        rel=rel_bias_table.astype(F32), w_in=w_in[0].astype(BF16), b_in=row(b_in),
        gln_g=row(gmlp_ln_g), gln_b=row(gmlp_ln_b), ws_pairs=ws_pairs, bs_full=bs_full,
        w_out=w_out[0].astype(BF16), ln1_g=row(ln1_g), ln1_b=row(ln1_b), wr_hi=wr_hi, wr_lo=wr_lo,
        w_gate=w_gate[0].astype(BF16), w_up=w_up[0].astype(BF16), w_down=w_down[0].astype(BF16),
        ln2_g=row(ln2_g), ln2_b=row(ln2_b))


def _trunk(x, p):
    B, L, D = x.shape
    n = B * L
    cap = CAPACITY_FACTOR * n // N_EXPERTS
    q, k, v, gm = _mixer_in(x, p["w_in"], p["b_in"], p["gln_g"], p["gln_b"], p["ws_pairs"], p["bs_full"])
    attn = _attention(q, k, v, p["rel"])
    x1, x1p, aff_t = _mixer_out(x, attn, gm, p["w_out"], p["ln1_g"], p["ln1_b"], p["wr_hi"], p["wr_lo"])
    slot, rank, span = _select(aff_t, cap)
    xs, gates = _dispatch(x1p.reshape(n, D // 2), slot, aff_t, cap)
    ys = _experts(xs, gates, p["w_gate"], p["w_up"], p["w_down"])
    z = _to_token_order(ys, slot, rank)
    work = _work_list(span, N_EXPERTS * cap, COMBINE_CHUNK, COMBINE_ROWS)
    y = _combine(z, span.T, x1.reshape(n, D), p["ln2_g"], p["ln2_b"], work)
    return y.reshape(B, L, D)


def kernel(x_prompt, x_sample, rel_bias_table, w_in, b_in, gmlp_ln_g, gmlp_ln_b, gmlp_w_s, gmlp_b_s, w_out,
           ln1_g, ln1_b, w_router, w_gate, w_up, w_down, ln2_g, ln2_b):
    p = _prepare_weights(rel_bias_table, w_in, b_in, gmlp_ln_g, gmlp_ln_b, gmlp_w_s, gmlp_b_s, w_out,
                         ln1_g, ln1_b, w_router, w_gate, w_up, w_down, ln2_g, ln2_b)
    return (_trunk(x_prompt, p), _trunk(x_sample, p))
```

```python
import dataclasses
import functools
import math

import numpy as np
import jax
import jax.numpy as jnp
from jax import lax
from jax.experimental import pallas as pl
from jax.experimental.pallas import tpu as pltpu
from jax.experimental.pallas import tpu_sc as plsc

F32 = jnp.float32
BF16 = jnp.bfloat16
I32 = jnp.int32

HEAD_DIM = 64
ATTN_WIDTH = 512
GMLP_WIDTH = 512
N_HEADS = ATTN_WIDTH // HEAD_DIM
IN_WIDTH = 3 * ATTN_WIDTH + 2 * GMLP_WIDTH
GMLP_CHUNK = 128
DILATIONS = (16, 4, 1)
HALF_WINDOW = 64
N_BUCKETS = 32
REL_MAX_DISTANCE = 1024
N_EXPERTS = 16
CAPACITY_FACTOR = 2
D_FF = 2816
ALPHA = 2.0 ** 0.25
LN_EPS = 1e-5
MASK_VALUE = -1e30

LANES = 128
VMEM_LIMIT = 56 * 1024 * 1024

Q_TILE = 128
HEAD_GROUP = 4
HG_WIDTH = HEAD_GROUP * HEAD_DIM
ATTN_UNROLL = 4
ATTN_SPAN = 2048
ROW_TILE = 512
SCAN_TILE = 256
SLOT_BLOCK = 2048
SC_SLOT_CHUNK = 4096
SC_ROWS = 64
COMBINE_CHUNK = 512
COMBINE_ROWS = 512
FFN_ROWS = 1024
FFN_COLS = 256


def _params(*sem):
    return pltpu.CompilerParams(dimension_semantics=sem, vmem_limit_bytes=VMEM_LIMIT)


def _layer_norm(y, g, b):
    mu = jnp.mean(y, axis=-1, keepdims=True)
    yc = y - mu
    var = jnp.mean(yc * yc, axis=-1, keepdims=True)
    return yc * lax.rsqrt(var + LN_EPS) * g + b


def _gelu_tanh(x):
    return 0.5 * x * (1.0 + jnp.tanh(math.sqrt(2.0 / math.pi) * (x + 0.044715 * (x * x * x))))


def _mixer_in_kernel(x_ref, w_ref, b_ref, lng_ref, lnb_ref, ws_ref, bs_ref,
                     q_ref, k_ref, v_ref, gm_ref):
    x = x_ref[...].astype(BF16)

    def proj(lo, hi):
        return jnp.dot(x, w_ref[:, lo:hi], preferred_element_type=F32) + b_ref[:, lo:hi]

    a = ATTN_WIDTH
    for i, ref in enumerate((q_ref, k_ref, v_ref)):
        t = proj(i * a, (i + 1) * a)
        if i == 0:
            t = t * (HEAD_DIM ** -0.5)
        for s in range(a // LANES):
            ref[s] = t[:, s * LANES:(s + 1) * LANES]
    gu = _gelu_tanh(proj(3 * a, 3 * a + GMLP_WIDTH))
    gv = _gelu_tanh(proj(3 * a + GMLP_WIDTH, IN_WIDTH))
    vn = _layer_norm(gv, lng_ref[...], lnb_ref[...]).astype(BF16)

    rows = x_ref.shape[0]
    low_half = lax.broadcasted_iota(I32, (1, LANES), 1) < HEAD_DIM
    for c in range(rows // GMLP_CHUNK):
        r0 = c * GMLP_CHUNK
        for s in range(GMLP_WIDTH // LANES):
            c0 = s * LANES
            vs = vn[r0:r0 + GMLP_CHUNK, c0:c0 + LANES]
            zero = jnp.zeros_like(vs)
            rhs = jnp.concatenate([jnp.where(low_half, vs, zero), jnp.where(low_half, zero, vs)], axis=0)
            vm = jnp.dot(ws_ref[s], rhs, preferred_element_type=F32) + bs_ref[:, c0:c0 + LANES]
            gm_ref[r0:r0 + GMLP_CHUNK, c0:c0 + LANES] = (gu[r0:r0 + GMLP_CHUNK, c0:c0 + LANES] * vm).astype(BF16)


def _mixer_in(x, w_in, b_in, ln_g, ln_b, ws_pairs, bs_full):
    B, L, D = x.shape
    tm = min(ROW_TILE, L)
    row = lambda w: pl.BlockSpec((None, tm, w), lambda b, i: (b, i, 0))
    full = lambda shape: pl.BlockSpec(shape, lambda b, i: (0,) * len(shape))
    n_slab = ATTN_WIDTH // LANES
    slab = pl.BlockSpec((None, n_slab, tm, LANES), lambda b, i: (b, 0, i, 0))
    slab_shape = jax.ShapeDtypeStruct((B, n_slab, L, LANES), F32)
    return pl.pallas_call(
        _mixer_in_kernel,
        grid=(B, L // tm),
        in_specs=[row(D), full(w_in.shape), full(b_in.shape), full(ln_g.shape), full(ln_b.shape),
                  full(ws_pairs.shape), full(bs_full.shape)],
        out_specs=[slab, slab, slab, row(GMLP_WIDTH)],
        out_shape=[slab_shape] * 3 + [jax.ShapeDtypeStruct((B, L, GMLP_WIDTH), BF16)],
        compiler_params=_params("parallel", "parallel"),
        name="mixer_in",
    )(x, w_in, b_in, ln_g, ln_b, ws_pairs, bs_full)


def _attn_kernel(b16_ref, b4_ref, b1_ref, q_ref, k_ref, v_ref, out_ref, o_acc, m_acc, l_acc, *, seq_len):
    span = out_ref.shape[0]
    span_idx = pl.program_id(2)
    n_slab = HG_WIDTH // LANES
    lane = lax.broadcasted_iota(I32, (1, HG_WIDTH), 1)
    head_masks = [(lane >= h * HEAD_DIM) & (lane < (h + 1) * HEAD_DIM) for h in range(HEAD_GROUP)]

    def rows(ref, start, size, stride):
        idx = pl.ds(start, size) if stride == 1 else pl.ds(start, size, stride=stride)
        return jnp.concatenate([ref[s, idx, :] for s in range(n_slab)], axis=1)

    def put(ref, start, stride, val):
        idx = pl.ds(start, Q_TILE) if stride == 1 else pl.ds(start, Q_TILE, stride=stride)
        for s in range(n_slab):
            ref[s, idx, :] = val[:, s * LANES:(s + 1) * LANES]

    def per_head(stacked):
        out = jnp.zeros((Q_TILE, HG_WIDTH), F32)
        for h in range(HEAD_GROUP):
            out = jnp.where(head_masks[h], stacked[h * Q_TILE:(h + 1) * Q_TILE], out)
        return out

    branches = tuple(zip(DILATIONS, (b16_ref, b4_ref, b1_ref)))
    for bi, (d, bias_ref) in enumerate(branches):
        lr = seq_len // d
        tk = min(2 * Q_TILE, lr)
        tiles_total = lr // Q_TILE
        tiles_per_residue = span // d // Q_TILE
        first, final = bi == 0, bi == len(branches) - 1

        def tile(idx, carry, d=d, bias_ref=bias_ref, lr=lr, tk=tk, tiles_total=tiles_total,
                 tiles_per_residue=tiles_per_residue, first=first, final=final):
            r = idx // tiles_per_residue
            jt = idx % tiles_per_residue
            jg = span_idx * tiles_per_residue + jt
            start = jnp.clip(jg * Q_TILE - HALF_WINDOW, 0, lr - tk)
            variant = jnp.where(jg == 0, 0, jnp.where(jg == tiles_total - 1, 2, 1))
            q_row = r + d * (jt * Q_TILE)
            q = rows(q_ref, q_row, Q_TILE, d).astype(BF16)
            kw = rows(k_ref, r + d * start, tk, d).astype(BF16)
            vw = rows(v_ref, r + d * start, tk, d).astype(BF16)
            qs = jnp.concatenate([jnp.where(hm, q, jnp.zeros_like(q)) for hm in head_masks], axis=0)
            s = lax.dot_general(qs, kw, (((1,), (1,)), ((), ())), preferred_element_type=F32)
            s = s + bias_ref[variant]
            m = jnp.max(s, axis=1, keepdims=True)
            p = jnp.exp(s - m)
            l = jnp.sum(p, axis=1, keepdims=True)
            pv = jnp.dot(p.astype(BF16), vw, preferred_element_type=F32)
            o_t = per_head(pv)
            m_t, l_t = per_head(m), per_head(l)
            if not first:
                m_old = rows(m_acc, q_row, Q_TILE, d)
                m_new = jnp.maximum(m_old, m_t)
                a_old, a_t = jnp.exp(m_old - m_new), jnp.exp(m_t - m_new)
                l_t = a_old * rows(l_acc, q_row, Q_TILE, d) + a_t * l_t
                o_t = a_old * rows(o_acc, q_row, Q_TILE, d) + a_t * o_t
                m_t = m_new
            if final:
                out_ref[pl.ds(pl.multiple_of(q_row, Q_TILE), Q_TILE), :] = (o_t / l_t).astype(BF16)
            else:
                put(o_acc, q_row, d, o_t)
                put(m_acc, q_row, d, m_t)
                put(l_acc, q_row, d, l_t)
            return carry

        lax.fori_loop(0, span // Q_TILE, tile, 0, unroll=ATTN_UNROLL)


def _t5_bucket(rel):
    half = N_BUCKETS // 2
    ret = np.where(rel > 0, half, 0)
    n = np.abs(rel)
    max_exact = half // 2
    large = max_exact + (np.log(np.maximum(n, 1) / max_exact) / np.log(REL_MAX_DISTANCE / max_exact)
                         * (half - max_exact)).astype(np.int32)
    large = np.minimum(large, half - 1)
    return (ret + np.where(n < max_exact, n, large)).astype(np.int32)


def _bias_tables(rel_table, dilation, tk):
    buckets, valids = [], []
    for delta in (0, -HALF_WINDOW, Q_TILE - tk):
        off = delta + np.arange(tk)[None, :] - np.arange(Q_TILE)[:, None]
        valids.append(np.abs(off) <= HALF_WINDOW)
        buckets.append(_t5_bucket(np.clip(off, -HALF_WINDOW, HALF_WINDOW) * dilation))
    bucket = jnp.asarray(np.stack(buckets), I32)
    onehot = (bucket[..., None] == jnp.arange(N_BUCKETS, dtype=I32)).astype(F32)
    bias = jnp.einsum("vqkb,bh->hvqk", onehot, rel_table, precision=lax.Precision.HIGHEST)
    bias = jnp.where(jnp.asarray(np.stack(valids))[None], bias, MASK_VALUE)
    n_hg = N_HEADS // HEAD_GROUP
    bias = bias.reshape(n_hg, HEAD_GROUP, 3, Q_TILE, tk).transpose(0, 2, 1, 3, 4)
    return bias.reshape(n_hg, 3, HEAD_GROUP * Q_TILE, tk)


def _attention(q, k, v, rel_table):
    B, _, L, _ = q.shape
    span = min(L, ATTN_SPAN)
    assert L % span == 0 and span % (max(DILATIONS) * Q_TILE) == 0, (L, span)
    n_hg = N_HEADS // HEAD_GROUP
    slabs = HG_WIDTH // LANES
    biases = [_bias_tables(rel_table, d, min(2 * Q_TILE, L // d)) for d in DILATIONS]
    once = pl.Buffered(1)
    bias_spec = lambda t: pl.BlockSpec((None,) + t.shape[1:], lambda g, b, s: (g, 0, 0, 0), pipeline_mode=once)
    seq = pl.BlockSpec((None, slabs, L, LANES), lambda g, b, s: (b, g, 0, 0), pipeline_mode=once)
    return pl.pallas_call(
        functools.partial(_attn_kernel, seq_len=L),
        grid=(n_hg, B, L // span),
        in_specs=[bias_spec(t) for t in biases]
                 + [pl.BlockSpec((None, slabs, span, LANES), lambda g, b, s: (b, g, s, 0)), seq, seq],
        out_specs=pl.BlockSpec((None, span, HG_WIDTH), lambda g, b, s: (b, s, g)),
        out_shape=jax.ShapeDtypeStruct((B, L, ATTN_WIDTH), BF16),
        scratch_shapes=[pltpu.VMEM((slabs, span, LANES), F32)] * 3,
        compiler_params=_params("parallel", "parallel", "arbitrary"),
        name="attention",
    )(*biases, q, k, v)


def _mixer_out_kernel(x_ref, attn_ref, gm_ref, wo_ref, g_ref, b_ref, wrh_ref, wrl_ref,
                      x1_ref, x1p_ref, aff_ref):
    mix = jnp.dot(attn_ref[...], wo_ref[0:ATTN_WIDTH, :], preferred_element_type=F32)
    mix = mix + jnp.dot(gm_ref[...], wo_ref[ATTN_WIDTH:, :], preferred_element_type=F32)
    x1 = _layer_norm(ALPHA * x_ref[...] + mix, g_ref[...], b_ref[...])
    x1_ref[...] = x1
    hi = x1.astype(BF16)
    x1p_ref[...] = _pack_bf16_pairs(x1)
    lo = (x1 - hi.astype(F32)).astype(BF16)
    nt = (((1,), (1,)), ((), ()))
    logits = (lax.dot_general(wrh_ref[...], hi, nt, preferred_element_type=F32)
              + lax.dot_general(wrl_ref[...], hi, nt, preferred_element_type=F32)
              + lax.dot_general(wrh_ref[...], lo, nt, preferred_element_type=F32))
    m = jnp.max(logits, axis=0, keepdims=True)
    e = jnp.exp(logits - m)
    aff_ref[...] = e / jnp.sum(e, axis=0, keepdims=True)


def _mixer_out(x, attn, gm, w_out, ln_g, ln_b, wr_hi, wr_lo):
    B, L, D = x.shape
    tm = min(ROW_TILE, L)
    per_seq = L // tm
    row = lambda w: pl.BlockSpec((None, tm, w), lambda b, i: (b, i, 0))
    full = lambda shape: pl.BlockSpec(shape, lambda b, i: (0,) * len(shape))
    return pl.pallas_call(
        _mixer_out_kernel,
        grid=(B, per_seq),
        in_specs=[row(D), row(ATTN_WIDTH), row(GMLP_WIDTH)]
                 + [full(w_out.shape), full(ln_g.shape), full(ln_b.shape), full(wr_hi.shape), full(wr_lo.shape)],
        out_specs=[row(D), row(D // 2), pl.BlockSpec((N_EXPERTS, tm), lambda b, i: (0, b * per_seq + i))],
        out_shape=[jax.ShapeDtypeStruct((B, L, D), F32), jax.ShapeDtypeStruct((B, L, D // 2), I32),
                   jax.ShapeDtypeStruct((N_EXPERTS, B * L), F32)],
        compiler_params=_params("parallel", "parallel"),
        name="mixer_out",
    )(x, attn, gm, w_out, ln_g, ln_b, wr_hi, wr_lo)


def _threshold_kernel(aff_ref, tau_ref, need_ref, *, cap):
    bits = lax.bitcast_convert_type(aff_ref[...], I32)

    def step(i, tau):
        cand = tau | jnp.left_shift(jnp.int32(1), 30 - i)
        cnt = jnp.sum((bits >= cand).astype(F32), axis=1, keepdims=True)
        return jnp.where(cnt >= cap, cand, tau)

    tau = lax.fori_loop(0, 31, step, jnp.zeros((bits.shape[0], 1), I32))
    above = jnp.sum((bits > tau).astype(F32), axis=1, keepdims=True)
    tau_ref[...] = jnp.broadcast_to(tau, tau_ref.shape)
    need_ref[...] = jnp.broadcast_to(cap - above, need_ref.shape)


def _slot_kernel(aff_ref, tau_ref, need_ref, tri_ref, low_ref, slot_ref, rank_ref, span_ref, carry_sel, carry_eq):
    @pl.when(pl.program_id(0) == 0)
    def _():
        carry_sel[...] = jnp.zeros_like(carry_sel)
        carry_eq[...] = jnp.zeros_like(carry_eq)

    tau = tau_ref[:, 0:1]
    need = need_ref[:, 0:1]
    c_sel = carry_sel[:, 0:1]
    c_eq = carry_eq[:, 0:1]
    tri = tri_ref[...]
    for s in range(aff_ref.shape[1] // SCAN_TILE):
        sl = slice(s * SCAN_TILE, (s + 1) * SCAN_TILE)
        bits = lax.bitcast_convert_type(aff_ref[:, sl], I32)
        eq = (bits == tau).astype(F32)
        eq_incl = jnp.dot(eq.astype(BF16), tri, preferred_element_type=F32)
        tie_taken = (eq_incl - eq + c_eq) < need
        sel = jnp.where((bits > tau) | ((bits == tau) & tie_taken), 1.0, 0.0)
        sel_b = sel.astype(BF16)
        incl = jnp.dot(sel_b, tri, preferred_element_type=F32)
        before = (incl - sel + c_sel).astype(I32)
        slot_ref[:, sl] = jnp.where(sel > 0.0, before, -1)
        per_token = jnp.sum(sel, axis=0, keepdims=True)
        first = jnp.sum(incl, axis=0, keepdims=True) - per_token + jnp.sum(c_sel, axis=0, keepdims=True)
        lower = jnp.dot(low_ref[...], sel_b, preferred_element_type=F32)
        rank_ref[:, sl] = (first + lower).astype(I32)
        span_ref[0:1, sl] = first
        span_ref[1:2, sl] = first + per_token
        c_sel = c_sel + incl[:, SCAN_TILE - 1:SCAN_TILE]
        c_eq = c_eq + eq_incl[:, SCAN_TILE - 1:SCAN_TILE]
    carry_sel[...] = jnp.broadcast_to(c_sel, carry_sel.shape)
    carry_eq[...] = jnp.broadcast_to(c_eq, carry_eq.shape)


def _select(aff_t, cap):
    E, n = aff_t.shape
    stat = jax.ShapeDtypeStruct((E, LANES), I32)
    tau, need = pl.pallas_call(
        functools.partial(_threshold_kernel, cap=float(cap)),
        out_shape=[stat, jax.ShapeDtypeStruct((E, LANES), F32)],
        compiler_params=pltpu.CompilerParams(vmem_limit_bytes=VMEM_LIMIT),
        name="threshold",
    )(aff_t)
    tri = jnp.asarray(np.triu(np.ones((SCAN_TILE, SCAN_TILE), np.float32)), BF16)
    low = jnp.asarray(np.tril(np.ones((E, E), np.float32), -1), BF16)
    tb = min(SLOT_BLOCK, n)
    blk = pl.BlockSpec((E, tb), lambda i: (0, i))
    const = lambda shape: pl.BlockSpec(shape, lambda i: (0, 0))
    return pl.pallas_call(
        _slot_kernel,
        grid=(n // tb,),
        in_specs=[blk, const((E, LANES)), const((E, LANES)), const((SCAN_TILE, SCAN_TILE)), const((E, E))],
        out_specs=[blk, blk, pl.BlockSpec((2, tb), lambda i: (0, i))],
        out_shape=[jax.ShapeDtypeStruct((E, n), I32)] * 2 + [jax.ShapeDtypeStruct((2, n), F32)],
        scratch_shapes=[pltpu.VMEM((E, LANES), F32), pltpu.VMEM((E, LANES), F32)],
        compiler_params=_params("arbitrary"),
        name="slots",
    )(aff_t, tau, need, tri, low)


def _work_list(span, total_rows, t_chunk, t_rows):
    n = span.shape[1]
    nch, ntl = n // t_chunk, total_rows // t_rows
    kmax = N_EXPERTS * t_chunk // t_rows + 1
    base = span[0, ::t_chunk].astype(I32)
    end = jnp.concatenate([base[1:], jnp.full((1,), total_rows, I32)])
    first = jnp.minimum(base // t_rows, ntl - 1)
    last = jnp.where(end > base, (end - 1) // t_rows, first)
    tile = first[:, None] + jnp.arange(kmax, dtype=I32)
    valid = (tile <= last[:, None]).reshape(-1)
    tile = jnp.minimum(tile, ntl - 1).reshape(-1)
    chunk = jnp.repeat(jnp.arange(nch, dtype=I32), kmax)
    length = nch + ntl
    count = jnp.sum(valid.astype(I32))
    idx = jnp.nonzero(valid, size=length, fill_value=0)[0].astype(I32)
    ar = jnp.arange(length, dtype=I32)
    live = ar < count
    idx = jnp.where(live, idx, idx[count - 1])
    c, t = chunk[idx], tile[idx]
    prev = jnp.concatenate([jnp.full((1,), -1, I32), c[:-1]])
    nxt = jnp.concatenate([c[1:], jnp.full((1,), -1, I32)])
    is_first = live & ((ar == 0) | (c != prev))
    is_last = live & ((ar == count - 1) | (c != nxt))
    return c, t, live.astype(I32) + 2 * is_first.astype(I32) + 4 * is_last.astype(I32)


def _pack_bf16_pairs(x):
    bits = lax.bitcast_convert_type(x.astype(BF16).astype(F32), I32)
    half = bits.shape[1] // 2
    return lax.shift_right_logical(bits[:, :half], 16) | bits[:, half:]


def _unpack_bf16_pairs(w):
    left = lax.bitcast_convert_type(lax.shift_left(w, 16), F32)
    right = lax.bitcast_convert_type(w & jnp.int32(-65536), F32)
    return jnp.concatenate([left, right], axis=1).astype(BF16)


def _sc_layout(n_experts, cap):
    info = plsc.get_sparse_core_info()
    workers = info.num_cores * info.num_subcores
    per_expert = workers // n_experts
    assert per_expert * n_experts == workers and cap % (per_expert * SC_ROWS) == 0, (workers, n_experts, cap)
    return info.num_cores, info.num_lanes, per_expert, cap // per_expert


def _sc_worker(n_cores, per_expert):
    wid = lax.axis_index("s") * n_cores + lax.axis_index("c")
    return wid // per_expert, wid % per_expert


def _sc_invert(slot_hbm, payload_hbms, slot_v, payload_vs, e, part, share, n, chunk, lanes, store):
    @pl.loop(0, n // chunk)
    def _(ci):
        pltpu.sync_copy(slot_hbm.at[e, pl.ds(ci * chunk, chunk)], slot_v)
        for src, dst in zip(payload_hbms, payload_vs):
            pltpu.sync_copy(src.at[e, pl.ds(ci * chunk, chunk)], dst)

        @pl.loop(0, chunk // lanes)
        def _(i):
            local = slot_v[pl.ds(i * lanes, lanes)] - part * share
            mine = (local >= 0) & (local < share)
            tok = ci * chunk + i * lanes + lax.iota(I32, lanes)
            store(local, mine, tok, [v[pl.ds(i * lanes, lanes)] for v in payload_vs])


_SC_PARAMS = dataclasses.replace(pltpu.CompilerParams(), needs_layout_passes=False)


def _dispatch(x1p, slot, aff_t, cap):
    n, width = x1p.shape
    E = slot.shape[0]
    n_cores, lanes, per_expert, share = _sc_layout(E, cap)
    chunk = min(n, SC_SLOT_CHUNK)
    mesh = plsc.VectorSubcoreMesh(core_axis_name="c", subcore_axis_name="s")

    @functools.partial(
        pl.kernel, mesh=mesh, compiler_params=_SC_PARAMS,
        out_type=[jax.ShapeDtypeStruct((E * cap, width), I32), jax.ShapeDtypeStruct((E * cap,), F32)],
        scratch_types=[pltpu.VMEM((chunk,), I32), pltpu.VMEM((chunk,), F32), pltpu.VMEM((share,), I32),
                       pltpu.VMEM((share,), F32), pltpu.VMEM((SC_ROWS, width), I32), pltpu.SemaphoreType.DMA],
    )
    def gather(x_hbm, slot_hbm, aff_hbm, out_hbm, gate_hbm, slot_v, aff_v, idx_v, gate_v, rows_v, sem):
        e, part = _sc_worker(n_cores, per_expert)

        def store(local, mine, tok, payloads):
            plsc.store_scatter(idx_v, [local], tok, mask=mine)
            plsc.store_scatter(gate_v, [local], payloads[0], mask=mine)

        _sc_invert(slot_hbm, [aff_hbm], slot_v, [aff_v], e, part, share, n, chunk, lanes, store)
        base = e * cap + part * share
        pltpu.sync_copy(gate_v, gate_hbm.at[pl.ds(base, share)])

        @pl.loop(0, share // SC_ROWS)
        def _(j):
            pltpu.async_copy(x_hbm.at[idx_v.at[pl.ds(j * SC_ROWS, SC_ROWS)]], rows_v, sem).wait()
            pltpu.sync_copy(rows_v, out_hbm.at[pl.ds(base + j * SC_ROWS, SC_ROWS)])

    xs, gates = gather(x1p, slot, aff_t)
    return xs.reshape(E, cap, width), gates.reshape(E, cap // LANES, LANES)


def _to_token_order(ysp, slot, rank):
    E, cap, width = ysp.shape
    n = slot.shape[1]
    n_cores, lanes, per_expert, share = _sc_layout(E, cap)
    chunk = min(n, SC_SLOT_CHUNK)
    mesh = plsc.VectorSubcoreMesh(core_axis_name="c", subcore_axis_name="s")

    @functools.partial(
        pl.kernel, mesh=mesh, compiler_params=_SC_PARAMS,
        out_type=jax.ShapeDtypeStruct((E * cap, width), I32),
        scratch_types=[pltpu.VMEM((chunk,), I32), pltpu.VMEM((chunk,), I32),
                       pltpu.VMEM((share // SC_ROWS, SC_ROWS), I32),
                       pltpu.VMEM((SC_ROWS, width), I32), pltpu.SemaphoreType.DMA],
    )
    def scatter(y_hbm, slot_hbm, rank_hbm, z_hbm, slot_v, rank_v, dest_v, rows_v, sem):
        e, part = _sc_worker(n_cores, per_expert)

        def store(local, mine, tok, payloads):
            plsc.store_scatter(dest_v, [local // SC_ROWS, local % SC_ROWS], payloads[0], mask=mine)

        _sc_invert(slot_hbm, [rank_hbm], slot_v, [rank_v], e, part, share, n, chunk, lanes, store)
        base = e * cap + part * share

        @pl.loop(0, share // SC_ROWS)
        def _(j):
            pltpu.sync_copy(y_hbm.at[pl.ds(base + j * SC_ROWS, SC_ROWS)], rows_v)
            pltpu.async_copy(rows_v, z_hbm.at[dest_v.at[j]], sem).wait()

    return scatter(ysp.reshape(E * cap, width), slot, rank)


def _ffn_kernel(x_ref, gate_ref, wg_ref, wu_ref, wd_ref, y_ref, acc_ref):
    x = _unpack_bf16_pairs(x_ref[...])
    for c in range(D_FF // FFN_COLS):
        cols = slice(c * FFN_COLS, (c + 1) * FFN_COLS)
        g = jnp.dot(x, wg_ref[:, cols], preferred_element_type=F32)
        u = jnp.dot(x, wu_ref[:, cols], preferred_element_type=F32)
        h = (g * jax.nn.sigmoid(g) * u).astype(BF16)
        part = jnp.dot(h, wd_ref[cols, :], preferred_element_type=F32)
        if c == 0:
            acc_ref[...] = part
        else:
            acc_ref[...] += part

    gates = gate_ref[...]
    pad = jnp.zeros((LANES - gates.shape[0], LANES), F32)
    scale = jnp.concatenate([gates, pad], axis=0).T
    for j in range(gates.shape[0]):
        rows = slice(j * LANES, (j + 1) * LANES)
        y_ref[rows, :] = _pack_bf16_pairs(acc_ref[rows, :] * scale[:, j:j + 1])


def _experts(xs, gates, w_gate, w_up, w_down):
    E, cap, packed = xs.shape
    D = 2 * packed
    tm = min(FFN_ROWS, cap)
    weight = lambda shape: pl.BlockSpec((None,) + shape, lambda e, m: (e, 0, 0))
    return pl.pallas_call(
        _ffn_kernel,
        grid=(E, cap // tm),
        in_specs=[pl.BlockSpec((None, tm, packed), lambda e, m: (e, m, 0)),
                  pl.BlockSpec((None, tm // LANES, LANES), lambda e, m: (e, m, 0)),
                  weight((D, D_FF)), weight((D, D_FF)), weight((D_FF, D))],
        out_specs=pl.BlockSpec((None, tm, packed), lambda e, m: (e, m, 0)),
        out_shape=jax.ShapeDtypeStruct((E, cap, packed), I32),
        scratch_shapes=[pltpu.VMEM((tm, D), F32)],
        compiler_params=_params("parallel", "arbitrary"),
        name="experts",
    )(xs, gates, w_gate, w_up, w_down)


def _combine_kernel(wc_ref, wt_ref, wf_ref, z_ref, span_ref, x1_ref, g_ref, b_ref, o_ref):
    w = pl.program_id(0)
    flags = wf_ref[w]

    @pl.when((flags & 2) == 2)
    def _():
        o_ref[...] = jnp.zeros_like(o_ref)

    @pl.when((flags & 1) == 1)
    def _():
        tr, tc = z_ref.shape[0], o_ref.shape[0]
        row = (lax.broadcasted_iota(I32, (tc, tr), 1) + wt_ref[w] * tr).astype(F32)
        owns = ((row >= span_ref[:, 0:1]) & (row < span_ref[:, 1:2])).astype(BF16)
        o_ref[...] += jnp.dot(owns, _unpack_bf16_pairs(z_ref[...]), preferred_element_type=F32)

    @pl.when((flags & 4) == 4)
    def _():
        o_ref[...] = _layer_norm(ALPHA * x1_ref[...] + o_ref[...], g_ref[...], b_ref[...])


def _combine(z, span_tm, x1, ln_g, ln_b, work):
    rows, packed = z.shape
    n, D = x1.shape
    length = work[0].shape[0]
    chunk = lambda w, wc, wt, wf: (wc[w], 0)
    const = lambda w, wc, wt, wf: (0, 0)
    return pl.pallas_call(
        _combine_kernel,
        grid_spec=pltpu.PrefetchScalarGridSpec(
            num_scalar_prefetch=3,
            grid=(length,),
            in_specs=[pl.BlockSpec((COMBINE_ROWS, packed), lambda w, wc, wt, wf: (wt[w], 0)),
                      pl.BlockSpec((COMBINE_CHUNK, 2), chunk),
                      pl.BlockSpec((COMBINE_CHUNK, D), chunk),
                      pl.BlockSpec((1, D), const), pl.BlockSpec((1, D), const)],
            out_specs=pl.BlockSpec((COMBINE_CHUNK, D), chunk),
        ),
        out_shape=jax.ShapeDtypeStruct((n, D), F32),
        compiler_params=_params("arbitrary"),
        name="combine",
    )(*work, z, span_tm, x1, ln_g, ln_b)


def _prepare_weights(rel_bias_table, w_in, b_in, gmlp_ln_g, gmlp_ln_b, gmlp_w_s, gmlp_b_s, w_out,
                     ln1_g, ln1_b, w_router, w_gate, w_up, w_down, ln2_g, ln2_b):
    row = lambda t: t[0].reshape(1, -1).astype(F32)
    ws = gmlp_w_s[0].astype(BF16)
    ws_pairs = jnp.concatenate([ws[0::2], ws[1::2]], axis=-1)
    bs_full = jnp.repeat(gmlp_b_s[0].T, HEAD_DIM, axis=1).astype(F32)
    wr_t = w_router[0].T.astype(F32)
    wr_hi = wr_t.astype(BF16)
    wr_lo = (wr_t - wr_hi.astype(F32)).astype(BF16)
    return dict(
        rel=rel_bias_table.astype(F32), w_in=w_in[0].astype(BF16), b_in=row(b_in),
        gln_g=row(gmlp_ln_g), gln_b=row(gmlp_ln_b), ws_pairs=ws_pairs, bs_full=bs_full,
        w_out=w_out[0].astype(BF16), ln1_g=row(ln1_g), ln1_b=row(ln1_b), wr_hi=wr_hi, wr_lo=wr_lo,
        w_gate=w_gate[0].astype(BF16), w_up=w_up[0].astype(BF16), w_down=w_down[0].astype(BF16),
        ln2_g=row(ln2_g), ln2_b=row(ln2_b))


def _trunk(x, p):
    B, L, D = x.shape
    n = B * L
    cap = CAPACITY_FACTOR * n // N_EXPERTS
    q, k, v, gm = _mixer_in(x, p["w_in"], p["b_in"], p["gln_g"], p["gln_b"], p["ws_pairs"], p["bs_full"])
    attn = _attention(q, k, v, p["rel"])
    x1, x1p, aff_t = _mixer_out(x, attn, gm, p["w_out"], p["ln1_g"], p["ln1_b"], p["wr_hi"], p["wr_lo"])
    slot, rank, span = _select(aff_t, cap)
    xs, gates = _dispatch(x1p.reshape(n, D // 2), slot, aff_t, cap)
    ys = _experts(xs, gates, p["w_gate"], p["w_up"], p["w_down"])
    z = _to_token_order(ys, slot, rank)
    work = _work_list(span, N_EXPERTS * cap, COMBINE_CHUNK, COMBINE_ROWS)
    y = _combine(z, span.T, x1.reshape(n, D), p["ln2_g"], p["ln2_b"], work)
    return y.reshape(B, L, D)


def kernel(x_prompt, x_sample, rel_bias_table, w_in, b_in, gmlp_ln_g, gmlp_ln_b, gmlp_w_s, gmlp_b_s, w_out,
           ln1_g, ln1_b, w_router, w_gate, w_up, w_down, ln2_g, ln2_b):
    p = _prepare_weights(rel_bias_table, w_in, b_in, gmlp_ln_g, gmlp_ln_b, gmlp_w_s, gmlp_b_s, w_out,
                         ln1_g, ln1_b, w_router, w_gate, w_up, w_down, ln2_g, ln2_b)
    return (_trunk(x_prompt, p), _trunk(x_sample, p))
```

```python
import dataclasses
import functools
import math

import numpy as np
import jax
import jax.numpy as jnp
from jax import lax
from jax.experimental import pallas as pl
from jax.experimental.pallas import tpu as pltpu
from jax.experimental.pallas import tpu_sc as plsc

F32 = jnp.float32
BF16 = jnp.bfloat16
I32 = jnp.int32

HEAD_DIM = 64
ATTN_WIDTH = 512
GMLP_WIDTH = 512
N_HEADS = ATTN_WIDTH // HEAD_DIM
IN_WIDTH = 3 * ATTN_WIDTH + 2 * GMLP_WIDTH
GMLP_CHUNK = 128
DILATIONS = (16, 4, 1)
HALF_WINDOW = 64
N_BUCKETS = 32
REL_MAX_DISTANCE = 1024
N_EXPERTS = 16
CAPACITY_FACTOR = 2
D_FF = 2816
ALPHA = 2.0 ** 0.25
LN_EPS = 1e-5
MASK_VALUE = -1e30

LANES = 128
VMEM_LIMIT = 56 * 1024 * 1024

Q_TILE = 128
HEAD_GROUP = 4
HG_WIDTH = HEAD_GROUP * HEAD_DIM
ATTN_UNROLL = 4
ATTN_SPAN = 2048
ROW_TILE = 512
SCAN_TILE = 256
SLOT_BLOCK = 2048
SC_SLOT_CHUNK = 4096
SC_ROWS = 64
COMBINE_CHUNK = 512
COMBINE_ROWS = 512
FFN_ROWS = 1024
FFN_COLS = 256


def _params(*sem):
    return pltpu.CompilerParams(dimension_semantics=sem, vmem_limit_bytes=VMEM_LIMIT)


def _layer_norm(y, g, b):
    mu = jnp.mean(y, axis=-1, keepdims=True)
    yc = y - mu
    var = jnp.mean(yc * yc, axis=-1, keepdims=True)
    return yc * lax.rsqrt(var + LN_EPS) * g + b


def _gelu_tanh(x):
    return 0.5 * x * (1.0 + jnp.tanh(math.sqrt(2.0 / math.pi) * (x + 0.044715 * (x * x * x))))


def _mixer_in_kernel(x_ref, w_ref, b_ref, lng_ref, lnb_ref, ws_ref, bs_ref,
                     q_ref, k_ref, v_ref, gm_ref):
    x = x_ref[...].astype(BF16)

    def proj(lo, hi):
        return jnp.dot(x, w_ref[:, lo:hi], preferred_element_type=F32) + b_ref[:, lo:hi]

    a = ATTN_WIDTH
    for i, ref in enumerate((q_ref, k_ref, v_ref)):
        t = proj(i * a, (i + 1) * a)
        if i == 0:
            t = t * (HEAD_DIM ** -0.5)
        for s in range(a // LANES):
            ref[s] = t[:, s * LANES:(s + 1) * LANES]
    gu = _gelu_tanh(proj(3 * a, 3 * a + GMLP_WIDTH))
    gv = _gelu_tanh(proj(3 * a + GMLP_WIDTH, IN_WIDTH))
    vn = _layer_norm(gv, lng_ref[...], lnb_ref[...]).astype(BF16)

    rows = x_ref.shape[0]
    low_half = lax.broadcasted_iota(I32, (1, LANES), 1) < HEAD_DIM
    for c in range(rows // GMLP_CHUNK):
        r0 = c * GMLP_CHUNK
        for s in range(GMLP_WIDTH // LANES):
            c0 = s * LANES
            vs = vn[r0:r0 + GMLP_CHUNK, c0:c0 + LANES]
            zero = jnp.zeros_like(vs)
            rhs = jnp.concatenate([jnp.where(low_half, vs, zero), jnp.where(low_half, zero, vs)], axis=0)
            vm = jnp.dot(ws_ref[s], rhs, preferred_element_type=F32) + bs_ref[:, c0:c0 + LANES]
            gm_ref[r0:r0 + GMLP_CHUNK, c0:c0 + LANES] = (gu[r0:r0 + GMLP_CHUNK, c0:c0 + LANES] * vm).astype(BF16)


def _mixer_in(x, w_in, b_in, ln_g, ln_b, ws_pairs, bs_full):
    B, L, D = x.shape
    tm = min(ROW_TILE, L)
    row = lambda w: pl.BlockSpec((None, tm, w), lambda b, i: (b, i, 0))
    full = lambda shape: pl.BlockSpec(shape, lambda b, i: (0,) * len(shape))
    n_slab = ATTN_WIDTH // LANES
    slab = pl.BlockSpec((None, n_slab, tm, LANES), lambda b, i: (b, 0, i, 0))
    slab_shape = jax.ShapeDtypeStruct((B, n_slab, L, LANES), F32)
    return pl.pallas_call(
        _mixer_in_kernel,
        grid=(B, L // tm),
        in_specs=[row(D), full(w_in.shape), full(b_in.shape), full(ln_g.shape), full(ln_b.shape),
                  full(ws_pairs.shape), full(bs_full.shape)],
        out_specs=[slab, slab, slab, row(GMLP_WIDTH)],
        out_shape=[slab_shape] * 3 + [jax.ShapeDtypeStruct((B, L, GMLP_WIDTH), BF16)],
        compiler_params=_params("parallel", "parallel"),
        name="mixer_in",
    )(x, w_in, b_in, ln_g, ln_b, ws_pairs, bs_full)


def _attn_kernel(b16_ref, b4_ref, b1_ref, q_ref, k_ref, v_ref, out_ref, o_acc, m_acc, l_acc, *, seq_len):
    span = out_ref.shape[0]
    span_idx = pl.program_id(2)
    n_slab = HG_WIDTH // LANES
    lane = lax.broadcasted_iota(I32, (1, HG_WIDTH), 1)
    head_masks = [(lane >= h * HEAD_DIM) & (lane < (h + 1) * HEAD_DIM) for h in range(HEAD_GROUP)]

    def rows(ref, start, size, stride):
        idx = pl.ds(start, size) if stride == 1 else pl.ds(start, size, stride=stride)
        return jnp.concatenate([ref[s, idx, :] for s in range(n_slab)], axis=1)

    def put(ref, start, stride, val):
        idx = pl.ds(start, Q_TILE) if stride == 1 else pl.ds(start, Q_TILE, stride=stride)
        for s in range(n_slab):
            ref[s, idx, :] = val[:, s * LANES:(s + 1) * LANES]

    def per_head(stacked):
        out = jnp.zeros((Q_TILE, HG_WIDTH), F32)
        for h in range(HEAD_GROUP):
            out = jnp.where(head_masks[h], stacked[h * Q_TILE:(h + 1) * Q_TILE], out)
        return out

    branches = tuple(zip(DILATIONS, (b16_ref, b4_ref, b1_ref)))
    for bi, (d, bias_ref) in enumerate(branches):
        lr = seq_len // d
        tk = min(2 * Q_TILE, lr)
        tiles_total = lr // Q_TILE
        tiles_per_residue = span // d // Q_TILE
        first, final = bi == 0, bi == len(branches) - 1

        def tile(idx, carry, d=d, bias_ref=bias_ref, lr=lr, tk=tk, tiles_total=tiles_total,
                 tiles_per_residue=tiles_per_residue, first=first, final=final):
            r = idx // tiles_per_residue
            jt = idx % tiles_per_residue
            jg = span_idx * tiles_per_residue + jt
            start = jnp.clip(jg * Q_TILE - HALF_WINDOW, 0, lr - tk)
            variant = jnp.where(jg == 0, 0, jnp.where(jg == tiles_total - 1, 2, 1))
            q_row = r + d * (jt * Q_TILE)
            q = rows(q_ref, q_row, Q_TILE, d).astype(BF16)
            kw = rows(k_ref, r + d * start, tk, d).astype(BF16)
            vw = rows(v_ref, r + d * start, tk, d).astype(BF16)
            qs = jnp.concatenate([jnp.where(hm, q, jnp.zeros_like(q)) for hm in head_masks], axis=0)
            s = lax.dot_general(qs, kw, (((1,), (1,)), ((), ())), preferred_element_type=F32)
            s = s + bias_ref[variant]
            m = jnp.max(s, axis=1, keepdims=True)
            p = jnp.exp(s - m)
            l = jnp.sum(p, axis=1, keepdims=True)
            pv = jnp.dot(p.astype(BF16), vw, preferred_element_type=F32)
            o_t = per_head(pv)
            m_t, l_t = per_head(m), per_head(l)
            if not first:
                m_old = rows(m_acc, q_row, Q_TILE, d)
                m_new = jnp.maximum(m_old, m_t)
                a_old, a_t = jnp.exp(m_old - m_new), jnp.exp(m_t - m_new)
                l_t = a_old * rows(l_acc, q_row, Q_TILE, d) + a_t * l_t
                o_t = a_old * rows(o_acc, q_row, Q_TILE, d) + a_t * o_t
                m_t = m_new
            if final:
                out_ref[pl.ds(pl.multiple_of(q_row, Q_TILE), Q_TILE), :] = (o_t / l_t).astype(BF16)
            else:
                put(o_acc, q_row, d, o_t)
                put(m_acc, q_row, d, m_t)
                put(l_acc, q_row, d, l_t)
            return carry

        lax.fori_loop(0, span // Q_TILE, tile, 0, unroll=ATTN_UNROLL)


def _t5_bucket(rel):
    half = N_BUCKETS // 2
    ret = np.where(rel > 0, half, 0)
    n = np.abs(rel)
    max_exact = half // 2
    large = max_exact + (np.log(np.maximum(n, 1) / max_exact) / np.log(REL_MAX_DISTANCE / max_exact)
                         * (half - max_exact)).astype(np.int32)
    large = np.minimum(large, half - 1)
    return (ret + np.where(n < max_exact, n, large)).astype(np.int32)


def _bias_tables(rel_table, dilation, tk):
    buckets, valids = [], []
    for delta in (0, -HALF_WINDOW, Q_TILE - tk):
        off = delta + np.arange(tk)[None, :] - np.arange(Q_TILE)[:, None]
        valids.append(np.abs(off) <= HALF_WINDOW)
        buckets.append(_t5_bucket(np.clip(off, -HALF_WINDOW, HALF_WINDOW) * dilation))
    bucket = jnp.asarray(np.stack(buckets), I32)
    onehot = (bucket[..., None] == jnp.arange(N_BUCKETS, dtype=I32)).astype(F32)
    bias = jnp.einsum("vqkb,bh->hvqk", onehot, rel_table, precision=lax.Precision.HIGHEST)
    bias = jnp.where(jnp.asarray(np.stack(valids))[None], bias, MASK_VALUE)
    n_hg = N_HEADS // HEAD_GROUP
    bias = bias.reshape(n_hg, HEAD_GROUP, 3, Q_TILE, tk).transpose(0, 2, 1, 3, 4)
    return bias.reshape(n_hg, 3, HEAD_GROUP * Q_TILE, tk)


def _attention(q, k, v, rel_table):
    B, _, L, _ = q.shape
    span = min(L, ATTN_SPAN)
    assert L % span == 0 and span % (max(DILATIONS) * Q_TILE) == 0, (L, span)
    n_hg = N_HEADS // HEAD_GROUP
    slabs = HG_WIDTH // LANES
    biases = [_bias_tables(rel_table, d, min(2 * Q_TILE, L // d)) for d in DILATIONS]
    once = pl.Buffered(1)
    bias_spec = lambda t: pl.BlockSpec((None,) + t.shape[1:], lambda g, b, s: (g, 0, 0, 0), pipeline_mode=once)
    seq = pl.BlockSpec((None, slabs, L, LANES), lambda g, b, s: (b, g, 0, 0))
    return pl.pallas_call(
        functools.partial(_attn_kernel, seq_len=L),
        grid=(n_hg, B, L // span),
        in_specs=[bias_spec(t) for t in biases]
                 + [pl.BlockSpec((None, slabs, span, LANES), lambda g, b, s: (b, g, s, 0)), seq, seq],
        out_specs=pl.BlockSpec((None, span, HG_WIDTH), lambda g, b, s: (b, s, g)),
        out_shape=jax.ShapeDtypeStruct((B, L, ATTN_WIDTH), BF16),
        scratch_shapes=[pltpu.VMEM((slabs, span, LANES), F32)] * 3,
        compiler_params=_params("parallel", "parallel", "arbitrary"),
        name="attention",
    )(*biases, q, k, v)


def _mixer_out_kernel(x_ref, attn_ref, gm_ref, wo_ref, g_ref, b_ref, wrh_ref, wrl_ref,
                      x1_ref, x1p_ref, aff_ref):
    mix = jnp.dot(attn_ref[...], wo_ref[0:ATTN_WIDTH, :], preferred_element_type=F32)
    mix = mix + jnp.dot(gm_ref[...], wo_ref[ATTN_WIDTH:, :], preferred_element_type=F32)
    x1 = _layer_norm(ALPHA * x_ref[...] + mix, g_ref[...], b_ref[...])
    x1_ref[...] = x1
    hi = x1.astype(BF16)
    x1p_ref[...] = _pack_bf16_pairs(x1)
    lo = (x1 - hi.astype(F32)).astype(BF16)
    nt = (((1,), (1,)), ((), ()))
    logits = (lax.dot_general(wrh_ref[...], hi, nt, preferred_element_type=F32)
              + lax.dot_general(wrl_ref[...], hi, nt, preferred_element_type=F32)
              + lax.dot_general(wrh_ref[...], lo, nt, preferred_element_type=F32))
    m = jnp.max(logits, axis=0, keepdims=True)
    e = jnp.exp(logits - m)
    aff_ref[...] = e / jnp.sum(e, axis=0, keepdims=True)


def _mixer_out(x, attn, gm, w_out, ln_g, ln_b, wr_hi, wr_lo):
    B, L, D = x.shape
    tm = min(ROW_TILE, L)
    per_seq = L // tm
    row = lambda w: pl.BlockSpec((None, tm, w), lambda b, i: (b, i, 0))
    full = lambda shape: pl.BlockSpec(shape, lambda b, i: (0,) * len(shape))
    return pl.pallas_call(
        _mixer_out_kernel,
        grid=(B, per_seq),
        in_specs=[row(D), row(ATTN_WIDTH), row(GMLP_WIDTH)]
                 + [full(w_out.shape), full(ln_g.shape), full(ln_b.shape), full(wr_hi.shape), full(wr_lo.shape)],
        out_specs=[row(D), row(D // 2), pl.BlockSpec((N_EXPERTS, tm), lambda b, i: (0, b * per_seq + i))],
        out_shape=[jax.ShapeDtypeStruct((B, L, D), F32), jax.ShapeDtypeStruct((B, L, D // 2), I32),
                   jax.ShapeDtypeStruct((N_EXPERTS, B * L), F32)],
        compiler_params=_params("parallel", "parallel"),
        name="mixer_out",
    )(x, attn, gm, w_out, ln_g, ln_b, wr_hi, wr_lo)


def _threshold_kernel(aff_ref, tau_ref, need_ref, *, cap):
    bits = lax.bitcast_convert_type(aff_ref[...], I32)

    def step(i, tau):
        cand = tau | jnp.left_shift(jnp.int32(1), 30 - i)
        cnt = jnp.sum((bits >= cand).astype(F32), axis=1, keepdims=True)
        return jnp.where(cnt >= cap, cand, tau)

    tau = lax.fori_loop(0, 31, step, jnp.zeros((bits.shape[0], 1), I32))
    above = jnp.sum((bits > tau).astype(F32), axis=1, keepdims=True)
    tau_ref[...] = jnp.broadcast_to(tau, tau_ref.shape)
    need_ref[...] = jnp.broadcast_to(cap - above, need_ref.shape)


def _slot_kernel(aff_ref, tau_ref, need_ref, tri_ref, low_ref, slot_ref, rank_ref, span_ref, carry_sel, carry_eq):
    @pl.when(pl.program_id(0) == 0)
    def _():
        carry_sel[...] = jnp.zeros_like(carry_sel)
        carry_eq[...] = jnp.zeros_like(carry_eq)

    tau = tau_ref[:, 0:1]
    need = need_ref[:, 0:1]
    c_sel = carry_sel[:, 0:1]
    c_eq = carry_eq[:, 0:1]
    tri = tri_ref[...]
    for s in range(aff_ref.shape[1] // SCAN_TILE):
        sl = slice(s * SCAN_TILE, (s + 1) * SCAN_TILE)
        bits = lax.bitcast_convert_type(aff_ref[:, sl], I32)
        eq = (bits == tau).astype(F32)
        eq_incl = jnp.dot(eq.astype(BF16), tri, preferred_element_type=F32)
        tie_taken = (eq_incl - eq + c_eq) < need
        sel = jnp.where((bits > tau) | ((bits == tau) & tie_taken), 1.0, 0.0)
        sel_b = sel.astype(BF16)
        incl = jnp.dot(sel_b, tri, preferred_element_type=F32)
        before = (incl - sel + c_sel).astype(I32)
        slot_ref[:, sl] = jnp.where(sel > 0.0, before, -1)
        per_token = jnp.sum(sel, axis=0, keepdims=True)
        first = jnp.sum(incl, axis=0, keepdims=True) - per_token + jnp.sum(c_sel, axis=0, keepdims=True)
        lower = jnp.dot(low_ref[...], sel_b, preferred_element_type=F32)
        rank_ref[:, sl] = (first + lower).astype(I32)
        span_ref[0:1, sl] = first
        span_ref[1:2, sl] = first + per_token
        c_sel = c_sel + incl[:, SCAN_TILE - 1:SCAN_TILE]
        c_eq = c_eq + eq_incl[:, SCAN_TILE - 1:SCAN_TILE]
    carry_sel[...] = jnp.broadcast_to(c_sel, carry_sel.shape)
    carry_eq[...] = jnp.broadcast_to(c_eq, carry_eq.shape)


def _select(aff_t, cap):
    E, n = aff_t.shape
    stat = jax.ShapeDtypeStruct((E, LANES), I32)
    tau, need = pl.pallas_call(
        functools.partial(_threshold_kernel, cap=float(cap)),
        out_shape=[stat, jax.ShapeDtypeStruct((E, LANES), F32)],
        compiler_params=pltpu.CompilerParams(vmem_limit_bytes=VMEM_LIMIT),
        name="threshold",
    )(aff_t)
    tri = jnp.asarray(np.triu(np.ones((SCAN_TILE, SCAN_TILE), np.float32)), BF16)
    low = jnp.asarray(np.tril(np.ones((E, E), np.float32), -1), BF16)
    tb = min(SLOT_BLOCK, n)
    blk = pl.BlockSpec((E, tb), lambda i: (0, i))
    const = lambda shape: pl.BlockSpec(shape, lambda i: (0, 0))
    return pl.pallas_call(
        _slot_kernel,
        grid=(n // tb,),
        in_specs=[blk, const((E, LANES)), const((E, LANES)), const((SCAN_TILE, SCAN_TILE)), const((E, E))],
        out_specs=[blk, blk, pl.BlockSpec((2, tb), lambda i: (0, i))],
        out_shape=[jax.ShapeDtypeStruct((E, n), I32)] * 2 + [jax.ShapeDtypeStruct((2, n), F32)],
        scratch_shapes=[pltpu.VMEM((E, LANES), F32), pltpu.VMEM((E, LANES), F32)],
        compiler_params=_params("arbitrary"),
        name="slots",
    )(aff_t, tau, need, tri, low)


def _work_list(span, total_rows, t_chunk, t_rows):
    n = span.shape[1]
    nch, ntl = n // t_chunk, total_rows // t_rows
    kmax = N_EXPERTS * t_chunk // t_rows + 1
    base = span[0, ::t_chunk].astype(I32)
    end = jnp.concatenate([base[1:], jnp.full((1,), total_rows, I32)])
    first = jnp.minimum(base // t_rows, ntl - 1)
    last = jnp.where(end > base, (end - 1) // t_rows, first)
    tile = first[:, None] + jnp.arange(kmax, dtype=I32)
    valid = (tile <= last[:, None]).reshape(-1)
    tile = jnp.minimum(tile, ntl - 1).reshape(-1)
    chunk = jnp.repeat(jnp.arange(nch, dtype=I32), kmax)
    length = nch + ntl
    count = jnp.sum(valid.astype(I32))
    idx = jnp.nonzero(valid, size=length, fill_value=0)[0].astype(I32)
    ar = jnp.arange(length, dtype=I32)
    live = ar < count
    idx = jnp.where(live, idx, idx[count - 1])
    c, t = chunk[idx], tile[idx]
    prev = jnp.concatenate([jnp.full((1,), -1, I32), c[:-1]])
    nxt = jnp.concatenate([c[1:], jnp.full((1,), -1, I32)])
    is_first = live & ((ar == 0) | (c != prev))
    is_last = live & ((ar == count - 1) | (c != nxt))
    return c, t, live.astype(I32) + 2 * is_first.astype(I32) + 4 * is_last.astype(I32)


def _pack_bf16_pairs(x):
    bits = lax.bitcast_convert_type(x.astype(BF16).astype(F32), I32)
    half = bits.shape[1] // 2
    return lax.shift_right_logical(bits[:, :half], 16) | bits[:, half:]


def _unpack_bf16_pairs(w):
    left = lax.bitcast_convert_type(lax.shift_left(w, 16), F32)
    right = lax.bitcast_convert_type(w & jnp.int32(-65536), F32)
    return jnp.concatenate([left, right], axis=1).astype(BF16)


def _sc_layout(n_experts, cap):
    info = plsc.get_sparse_core_info()
    workers = info.num_cores * info.num_subcores
    per_expert = workers // n_experts
    assert per_expert * n_experts == workers and cap % (per_expert * SC_ROWS) == 0, (workers, n_experts, cap)
    return info.num_cores, info.num_lanes, per_expert, cap // per_expert


def _sc_worker(n_cores, per_expert):
    wid = lax.axis_index("s") * n_cores + lax.axis_index("c")
    return wid // per_expert, wid % per_expert


def _sc_invert(slot_hbm, payload_hbms, slot_v, payload_vs, e, part, share, n, chunk, lanes, store):
    @pl.loop(0, n // chunk)
    def _(ci):
        pltpu.sync_copy(slot_hbm.at[e, pl.ds(ci * chunk, chunk)], slot_v)
        for src, dst in zip(payload_hbms, payload_vs):
            pltpu.sync_copy(src.at[e, pl.ds(ci * chunk, chunk)], dst)

        @pl.loop(0, chunk // lanes)
        def _(i):
            local = slot_v[pl.ds(i * lanes, lanes)] - part * share
            mine = (local >= 0) & (local < share)
            tok = ci * chunk + i * lanes + lax.iota(I32, lanes)
            store(local, mine, tok, [v[pl.ds(i * lanes, lanes)] for v in payload_vs])


_SC_PARAMS = dataclasses.replace(pltpu.CompilerParams(), needs_layout_passes=False)


def _dispatch(x1p, slot, aff_t, cap):
    n, width = x1p.shape
    E = slot.shape[0]
    n_cores, lanes, per_expert, share = _sc_layout(E, cap)
    chunk = min(n, SC_SLOT_CHUNK)
    mesh = plsc.VectorSubcoreMesh(core_axis_name="c", subcore_axis_name="s")

    @functools.partial(
        pl.kernel, mesh=mesh, compiler_params=_SC_PARAMS,
        out_type=[jax.ShapeDtypeStruct((E * cap, width), I32), jax.ShapeDtypeStruct((E * cap,), F32)],
        scratch_types=[pltpu.VMEM((chunk,), I32), pltpu.VMEM((chunk,), F32), pltpu.VMEM((share,), I32),
                       pltpu.VMEM((share,), F32), pltpu.VMEM((SC_ROWS, width), I32), pltpu.SemaphoreType.DMA],
    )
    def gather(x_hbm, slot_hbm, aff_hbm, out_hbm, gate_hbm, slot_v, aff_v, idx_v, gate_v, rows_v, sem):
        e, part = _sc_worker(n_cores, per_expert)

        def store(local, mine, tok, payloads):
            plsc.store_scatter(idx_v, [local], tok, mask=mine)
            plsc.store_scatter(gate_v, [local], payloads[0], mask=mine)

        _sc_invert(slot_hbm, [aff_hbm], slot_v, [aff_v], e, part, share, n, chunk, lanes, store)
        base = e * cap + part * share
        pltpu.sync_copy(gate_v, gate_hbm.at[pl.ds(base, share)])

        @pl.loop(0, share // SC_ROWS)
        def _(j):
            pltpu.async_copy(x_hbm.at[idx_v.at[pl.ds(j * SC_ROWS, SC_ROWS)]], rows_v, sem).wait()
            pltpu.sync_copy(rows_v, out_hbm.at[pl.ds(base + j * SC_ROWS, SC_ROWS)])

    xs, gates = gather(x1p, slot, aff_t)
    return xs.reshape(E, cap, width), gates.reshape(E, cap // LANES, LANES)


def _to_token_order(ysp, slot, rank):
    E, cap, width = ysp.shape
    n = slot.shape[1]
    n_cores, lanes, per_expert, share = _sc_layout(E, cap)
    chunk = min(n, SC_SLOT_CHUNK)
    mesh = plsc.VectorSubcoreMesh(core_axis_name="c", subcore_axis_name="s")

    @functools.partial(
        pl.kernel, mesh=mesh, compiler_params=_SC_PARAMS,
        out_type=jax.ShapeDtypeStruct((E * cap, width), I32),
        scratch_types=[pltpu.VMEM((chunk,), I32), pltpu.VMEM((chunk,), I32),
                       pltpu.VMEM((share // SC_ROWS, SC_ROWS), I32),
                       pltpu.VMEM((SC_ROWS, width), I32), pltpu.SemaphoreType.DMA],
    )
    def scatter(y_hbm, slot_hbm, rank_hbm, z_hbm, slot_v, rank_v, dest_v, rows_v, sem):
        e, part = _sc_worker(n_cores, per_expert)

        def store(local, mine, tok, payloads):
            plsc.store_scatter(dest_v, [local // SC_ROWS, local % SC_ROWS], payloads[0], mask=mine)

        _sc_invert(slot_hbm, [rank_hbm], slot_v, [rank_v], e, part, share, n, chunk, lanes, store)
        base = e * cap + part * share

        @pl.loop(0, share // SC_ROWS)
        def _(j):
            pltpu.sync_copy(y_hbm.at[pl.ds(base + j * SC_ROWS, SC_ROWS)], rows_v)
            pltpu.async_copy(rows_v, z_hbm.at[dest_v.at[j]], sem).wait()

    return scatter(ysp.reshape(E * cap, width), slot, rank)


def _ffn_kernel(x_ref, gate_ref, wg_ref, wu_ref, wd_ref, y_ref, acc_ref):
    x = _unpack_bf16_pairs(x_ref[...])
    for c in range(D_FF // FFN_COLS):
        cols = slice(c * FFN_COLS, (c + 1) * FFN_COLS)
        g = jnp.dot(x, wg_ref[:, cols], preferred_element_type=F32)
        u = jnp.dot(x, wu_ref[:, cols], preferred_element_type=F32)
        h = (g * jax.nn.sigmoid(g) * u).astype(BF16)
        part = jnp.dot(h, wd_ref[cols, :], preferred_element_type=F32)
        if c == 0:
            acc_ref[...] = part
        else:
            acc_ref[...] += part

    gates = gate_ref[...]
    pad = jnp.zeros((LANES - gates.shape[0], LANES), F32)
    scale = jnp.concatenate([gates, pad], axis=0).T
    for j in range(gates.shape[0]):
        rows = slice(j * LANES, (j + 1) * LANES)
        y_ref[rows, :] = _pack_bf16_pairs(acc_ref[rows, :] * scale[:, j:j + 1])


def _experts(xs, gates, w_gate, w_up, w_down):
    E, cap, packed = xs.shape
    D = 2 * packed
    tm = min(FFN_ROWS, cap)
    weight = lambda shape: pl.BlockSpec((None,) + shape, lambda e, m: (e, 0, 0))
    return pl.pallas_call(
        _ffn_kernel,
        grid=(E, cap // tm),
        in_specs=[pl.BlockSpec((None, tm, packed), lambda e, m: (e, m, 0)),
                  pl.BlockSpec((None, tm // LANES, LANES), lambda e, m: (e, m, 0)),
                  weight((D, D_FF)), weight((D, D_FF)), weight((D_FF, D))],
        out_specs=pl.BlockSpec((None, tm, packed), lambda e, m: (e, m, 0)),
        out_shape=jax.ShapeDtypeStruct((E, cap, packed), I32),
        scratch_shapes=[pltpu.VMEM((tm, D), F32)],
        compiler_params=_params("parallel", "arbitrary"),
        name="experts",
    )(xs, gates, w_gate, w_up, w_down)


def _combine_kernel(wc_ref, wt_ref, wf_ref, z_ref, span_ref, x1_ref, g_ref, b_ref, o_ref):
    w = pl.program_id(0)
    flags = wf_ref[w]

    @pl.when((flags & 2) == 2)
    def _():
        o_ref[...] = jnp.zeros_like(o_ref)

    @pl.when((flags & 1) == 1)
    def _():
        tr, tc = z_ref.shape[0], o_ref.shape[0]
        row = (lax.broadcasted_iota(I32, (tc, tr), 1) + wt_ref[w] * tr).astype(F32)
        owns = ((row >= span_ref[:, 0:1]) & (row < span_ref[:, 1:2])).astype(BF16)
        o_ref[...] += jnp.dot(owns, _unpack_bf16_pairs(z_ref[...]), preferred_element_type=F32)

    @pl.when((flags & 4) == 4)
    def _():
        o_ref[...] = _layer_norm(ALPHA * x1_ref[...] + o_ref[...], g_ref[...], b_ref[...])


def _combine(z, span_tm, x1, ln_g, ln_b, work):
    rows, packed = z.shape
    n, D = x1.shape
    length = work[0].shape[0]
    chunk = lambda w, wc, wt, wf: (wc[w], 0)
    const = lambda w, wc, wt, wf: (0, 0)
    return pl.pallas_call(
        _combine_kernel,
        grid_spec=pltpu.PrefetchScalarGridSpec(
            num_scalar_prefetch=3,
            grid=(length,),
            in_specs=[pl.BlockSpec((COMBINE_ROWS, packed), lambda w, wc, wt, wf: (wt[w], 0)),
                      pl.BlockSpec((COMBINE_CHUNK, 2), chunk),
                      pl.BlockSpec((COMBINE_CHUNK, D), chunk),
                      pl.BlockSpec((1, D), const), pl.BlockSpec((1, D), const)],
            out_specs=pl.BlockSpec((COMBINE_CHUNK, D), chunk),
        ),
        out_shape=jax.ShapeDtypeStruct((n, D), F32),
        compiler_params=_params("arbitrary"),
        name="combine",
    )(*work, z, span_tm, x1, ln_g, ln_b)


def _prepare_weights(rel_bias_table, w_in, b_in, gmlp_ln_g, gmlp_ln_b, gmlp_w_s, gmlp_b_s, w_out,
                     ln1_g, ln1_b, w_router, w_gate, w_up, w_down, ln2_g, ln2_b):
    row = lambda t: t[0].reshape(1, -1).astype(F32)
    ws = gmlp_w_s[0].astype(BF16)
    ws_pairs = jnp.concatenate([ws[0::2], ws[1::2]], axis=-1)
    bs_full = jnp.repeat(gmlp_b_s[0].T, HEAD_DIM, axis=1).astype(F32)
    wr_t = w_router[0].T.astype(F32)
    wr_hi = wr_t.astype(BF16)
    wr_lo = (wr_t - wr_hi.astype(F32)).astype(BF16)
    return dict(
        rel=rel_bias_table.astype(F32), w_in=w_in[0].astype(BF16), b_in=row(b_in),
        gln_g=row(gmlp_ln_g), gln_b=row(gmlp_ln_b), ws_pairs=ws_pairs, bs_full=bs_full,
        w_out=w_out[0].astype(BF16), ln1_g=row(ln1_g), ln1_b=row(ln1_b), wr_hi=wr_hi, wr_lo=wr_lo,
        w_gate=w_gate[0].astype(BF16), w_up=w_up[0].astype(BF16), w_down=w_down[0].astype(BF16),
        ln2_g=row(ln2_g), ln2_b=row(ln2_b))


def _trunk(x, p):
    B, L, D = x.shape
    n = B * L
    cap = CAPACITY_FACTOR * n // N_EXPERTS
    q, k, v, gm = _mixer_in(x, p["w_in"], p["b_in"], p["gln_g"], p["gln_b"], p["ws_pairs"], p["bs_full"])
    attn = _attention(q, k, v, p["rel"])
    x1, x1p, aff_t = _mixer_out(x, attn, gm, p["w_out"], p["ln1_g"], p["ln1_b"], p["wr_hi"], p["wr_lo"])
    slot, rank, span = _select(aff_t, cap)
    xs, gates = _dispatch(x1p.reshape(n, D // 2), slot, aff_t, cap)
    ys = _experts(xs, gates, p["w_gate"], p["w_up"], p["w_down"])
    z = _to_token_order(ys, slot, rank)
    work = _work_list(span, N_EXPERTS * cap, COMBINE_CHUNK, COMBINE_ROWS)
    y = _combine(z, span.T, x1.reshape(n, D), p["ln2_g"], p["ln2_b"], work)
    return y.reshape(B, L, D)


def kernel(x_prompt, x_sample, rel_bias_table, w_in, b_in, gmlp_ln_g, gmlp_ln_b, gmlp_w_s, gmlp_b_s, w_out,
           ln1_g, ln1_b, w_router, w_gate, w_up, w_down, ln2_g, ln2_b):
    p = _prepare_weights(rel_bias_table, w_in, b_in, gmlp_ln_g, gmlp_ln_b, gmlp_w_s, gmlp_b_s, w_out,
                         ln1_g, ln1_b, w_router, w_gate, w_up, w_down, ln2_g, ln2_b)
    return (_trunk(x_prompt, p), _trunk(x_sample, p))
```

```python
import dataclasses
import functools
import math

import numpy as np
import jax
import jax.numpy as jnp
from jax import lax
from jax.experimental import pallas as pl
from jax.experimental.pallas import tpu as pltpu
from jax.experimental.pallas import tpu_sc as plsc

F32 = jnp.float32
BF16 = jnp.bfloat16
I32 = jnp.int32

HEAD_DIM = 64
ATTN_WIDTH = 512
GMLP_WIDTH = 512
N_HEADS = ATTN_WIDTH // HEAD_DIM
IN_WIDTH = 3 * ATTN_WIDTH + 2 * GMLP_WIDTH
GMLP_CHUNK = 128
DILATIONS = (16, 4, 1)
HALF_WINDOW = 64
N_BUCKETS = 32
REL_MAX_DISTANCE = 1024
N_EXPERTS = 16
CAPACITY_FACTOR = 2
D_FF = 2816
ALPHA = 2.0 ** 0.25
LN_EPS = 1e-5
MASK_VALUE = -1e30

LANES = 128
VMEM_LIMIT = 56 * 1024 * 1024

Q_TILE = 128
HEAD_GROUP = 4
HG_WIDTH = HEAD_GROUP * HEAD_DIM
ATTN_UNROLL = 4
ATTN_SPAN = 2048
ROW_TILE = 1024
SCAN_TILE = 256
SLOT_BLOCK = 2048
SC_SLOT_CHUNK = 4096
SC_ROWS = 64
COMBINE_CHUNK = 512
COMBINE_ROWS = 512
FFN_ROWS = 1024
FFN_COLS = 256


def _params(*sem):
    return pltpu.CompilerParams(dimension_semantics=sem, vmem_limit_bytes=VMEM_LIMIT)


def _layer_norm(y, g, b):
    mu = jnp.mean(y, axis=-1, keepdims=True)
    yc = y - mu
    var = jnp.mean(yc * yc, axis=-1, keepdims=True)
    return yc * lax.rsqrt(var + LN_EPS) * g + b


def _gelu_tanh(x):
    return 0.5 * x * (1.0 + jnp.tanh(math.sqrt(2.0 / math.pi) * (x + 0.044715 * (x * x * x))))


def _mixer_in_kernel(x_ref, w_ref, b_ref, lng_ref, lnb_ref, ws_ref, bs_ref,
                     q_ref, k_ref, v_ref, gm_ref):
    x = x_ref[...].astype(BF16)

    def proj(lo, hi):
        return jnp.dot(x, w_ref[:, lo:hi], preferred_element_type=F32) + b_ref[:, lo:hi]

    a = ATTN_WIDTH
    for i, ref in enumerate((q_ref, k_ref, v_ref)):
        t = proj(i * a, (i + 1) * a)
        if i == 0:
            t = t * (HEAD_DIM ** -0.5)
        for s in range(a // LANES):
            ref[s] = t[:, s * LANES:(s + 1) * LANES]
    gu = _gelu_tanh(proj(3 * a, 3 * a + GMLP_WIDTH))
    gv = _gelu_tanh(proj(3 * a + GMLP_WIDTH, IN_WIDTH))
    vn = _layer_norm(gv, lng_ref[...], lnb_ref[...]).astype(BF16)

    rows = x_ref.shape[0]
    low_half = lax.broadcasted_iota(I32, (1, LANES), 1) < HEAD_DIM
    for c in range(rows // GMLP_CHUNK):
        r0 = c * GMLP_CHUNK
        for s in range(GMLP_WIDTH // LANES):
            c0 = s * LANES
            vs = vn[r0:r0 + GMLP_CHUNK, c0:c0 + LANES]
            zero = jnp.zeros_like(vs)
            rhs = jnp.concatenate([jnp.where(low_half, vs, zero), jnp.where(low_half, zero, vs)], axis=0)
            vm = jnp.dot(ws_ref[s], rhs, preferred_element_type=F32) + bs_ref[:, c0:c0 + LANES]
            gm_ref[r0:r0 + GMLP_CHUNK, c0:c0 + LANES] = (gu[r0:r0 + GMLP_CHUNK, c0:c0 + LANES] * vm).astype(BF16)


def _mixer_in(x, w_in, b_in, ln_g, ln_b, ws_pairs, bs_full):
    B, L, D = x.shape
    tm = min(ROW_TILE, L)
    row = lambda w: pl.BlockSpec((None, tm, w), lambda b, i: (b, i, 0))
    full = lambda shape: pl.BlockSpec(shape, lambda b, i: (0,) * len(shape))
    n_slab = ATTN_WIDTH // LANES
    slab = pl.BlockSpec((None, n_slab, tm, LANES), lambda b, i: (b, 0, i, 0))
    slab_shape = jax.ShapeDtypeStruct((B, n_slab, L, LANES), F32)
    return pl.pallas_call(
        _mixer_in_kernel,
        grid=(B, L // tm),
        in_specs=[row(D), full(w_in.shape), full(b_in.shape), full(ln_g.shape), full(ln_b.shape),
                  full(ws_pairs.shape), full(bs_full.shape)],
        out_specs=[slab, slab, slab, row(GMLP_WIDTH)],
        out_shape=[slab_shape] * 3 + [jax.ShapeDtypeStruct((B, L, GMLP_WIDTH), BF16)],
        compiler_params=_params("parallel", "parallel"),
        name="mixer_in",
    )(x, w_in, b_in, ln_g, ln_b, ws_pairs, bs_full)


def _attn_kernel(b16_ref, b4_ref, b1_ref, q_ref, k_ref, v_ref, out_ref, o_acc, m_acc, l_acc, *, seq_len):
    span = out_ref.shape[0]
    span_idx = pl.program_id(2)
    n_slab = HG_WIDTH // LANES
    lane = lax.broadcasted_iota(I32, (1, HG_WIDTH), 1)
    head_masks = [(lane >= h * HEAD_DIM) & (lane < (h + 1) * HEAD_DIM) for h in range(HEAD_GROUP)]

    def rows(ref, start, size, stride):
        idx = pl.ds(start, size) if stride == 1 else pl.ds(start, size, stride=stride)
        return jnp.concatenate([ref[s, idx, :] for s in range(n_slab)], axis=1)

    def put(ref, start, stride, val):
        idx = pl.ds(start, Q_TILE) if stride == 1 else pl.ds(start, Q_TILE, stride=stride)
        for s in range(n_slab):
            ref[s, idx, :] = val[:, s * LANES:(s + 1) * LANES]

    def per_head(stacked):
        out = jnp.zeros((Q_TILE, HG_WIDTH), F32)
        for h in range(HEAD_GROUP):
            out = jnp.where(head_masks[h], stacked[h * Q_TILE:(h + 1) * Q_TILE], out)
        return out

    branches = tuple(zip(DILATIONS, (b16_ref, b4_ref, b1_ref)))
    for bi, (d, bias_ref) in enumerate(branches):
        lr = seq_len // d
        tk = min(2 * Q_TILE, lr)
        tiles_total = lr // Q_TILE
        tiles_per_residue = span // d // Q_TILE
        first, final = bi == 0, bi == len(branches) - 1

        def tile(idx, carry, d=d, bias_ref=bias_ref, lr=lr, tk=tk, tiles_total=tiles_total,
                 tiles_per_residue=tiles_per_residue, first=first, final=final):
            r = idx // tiles_per_residue
            jt = idx % tiles_per_residue
            jg = span_idx * tiles_per_residue + jt
            start = jnp.clip(jg * Q_TILE - HALF_WINDOW, 0, lr - tk)
            variant = jnp.where(jg == 0, 0, jnp.where(jg == tiles_total - 1, 2, 1))
            q_row = r + d * (jt * Q_TILE)
            q = rows(q_ref, q_row, Q_TILE, d).astype(BF16)
            kw = rows(k_ref, r + d * start, tk, d).astype(BF16)
            vw = rows(v_ref, r + d * start, tk, d).astype(BF16)
            qs = jnp.concatenate([jnp.where(hm, q, jnp.zeros_like(q)) for hm in head_masks], axis=0)
            s = lax.dot_general(qs, kw, (((1,), (1,)), ((), ())), preferred_element_type=F32)
            s = s + bias_ref[variant]
            m = jnp.max(s, axis=1, keepdims=True)
            p = jnp.exp(s - m)
            l = jnp.sum(p, axis=1, keepdims=True)
            pv = jnp.dot(p.astype(BF16), vw, preferred_element_type=F32)
            o_t = per_head(pv)
            m_t, l_t = per_head(m), per_head(l)
            if not first:
                m_old = rows(m_acc, q_row, Q_TILE, d)
                m_new = jnp.maximum(m_old, m_t)
                a_old, a_t = jnp.exp(m_old - m_new), jnp.exp(m_t - m_new)
                l_t = a_old * rows(l_acc, q_row, Q_TILE, d) + a_t * l_t
                o_t = a_old * rows(o_acc, q_row, Q_TILE, d) + a_t * o_t
                m_t = m_new
            if final:
                out_ref[pl.ds(pl.multiple_of(q_row, Q_TILE), Q_TILE), :] = (o_t / l_t).astype(BF16)
            else:
                put(o_acc, q_row, d, o_t)
                put(m_acc, q_row, d, m_t)
                put(l_acc, q_row, d, l_t)
            return carry

        lax.fori_loop(0, span // Q_TILE, tile, 0, unroll=ATTN_UNROLL)


def _t5_bucket(rel):
    half = N_BUCKETS // 2
    ret = np.where(rel > 0, half, 0)
    n = np.abs(rel)
    max_exact = half // 2
    large = max_exact + (np.log(np.maximum(n, 1) / max_exact) / np.log(REL_MAX_DISTANCE / max_exact)
                         * (half - max_exact)).astype(np.int32)
    large = np.minimum(large, half - 1)
    return (ret + np.where(n < max_exact, n, large)).astype(np.int32)


def _bias_tables(rel_table, dilation, tk):
    buckets, valids = [], []
    for delta in (0, -HALF_WINDOW, Q_TILE - tk):
        off = delta + np.arange(tk)[None, :] - np.arange(Q_TILE)[:, None]
        valids.append(np.abs(off) <= HALF_WINDOW)
        buckets.append(_t5_bucket(np.clip(off, -HALF_WINDOW, HALF_WINDOW) * dilation))
    bucket = jnp.asarray(np.stack(buckets), I32)
    onehot = (bucket[..., None] == jnp.arange(N_BUCKETS, dtype=I32)).astype(F32)
    bias = jnp.einsum("vqkb,bh->hvqk", onehot, rel_table, precision=lax.Precision.HIGHEST)
    bias = jnp.where(jnp.asarray(np.stack(valids))[None], bias, MASK_VALUE)
    n_hg = N_HEADS // HEAD_GROUP
    bias = bias.reshape(n_hg, HEAD_GROUP, 3, Q_TILE, tk).transpose(0, 2, 1, 3, 4)
    return bias.reshape(n_hg, 3, HEAD_GROUP * Q_TILE, tk)


def _attention(q, k, v, rel_table):
    B, _, L, _ = q.shape
    span = min(L, ATTN_SPAN)
    assert L % span == 0 and span % (max(DILATIONS) * Q_TILE) == 0, (L, span)
    n_hg = N_HEADS // HEAD_GROUP
    slabs = HG_WIDTH // LANES
    biases = [_bias_tables(rel_table, d, min(2 * Q_TILE, L // d)) for d in DILATIONS]
    once = pl.Buffered(1)
    bias_spec = lambda t: pl.BlockSpec((None,) + t.shape[1:], lambda g, b, s: (g, 0, 0, 0), pipeline_mode=once)
    seq = pl.BlockSpec((None, slabs, L, LANES), lambda g, b, s: (b, g, 0, 0))
    return pl.pallas_call(
        functools.partial(_attn_kernel, seq_len=L),
        grid=(n_hg, B, L // span),
        in_specs=[bias_spec(t) for t in biases]
                 + [pl.BlockSpec((None, slabs, span, LANES), lambda g, b, s: (b, g, s, 0)), seq, seq],
        out_specs=pl.BlockSpec((None, span, HG_WIDTH), lambda g, b, s: (b, s, g)),
        out_shape=jax.ShapeDtypeStruct((B, L, ATTN_WIDTH), BF16),
        scratch_shapes=[pltpu.VMEM((slabs, span, LANES), F32)] * 3,
        compiler_params=_params("parallel", "parallel", "arbitrary"),
        name="attention",
    )(*biases, q, k, v)


def _mixer_out_kernel(x_ref, attn_ref, gm_ref, wo_ref, g_ref, b_ref, wrh_ref, wrl_ref,
                      x1_ref, x1p_ref, aff_ref):
    mix = jnp.dot(attn_ref[...], wo_ref[0:ATTN_WIDTH, :], preferred_element_type=F32)
    mix = mix + jnp.dot(gm_ref[...], wo_ref[ATTN_WIDTH:, :], preferred_element_type=F32)
    x1 = _layer_norm(ALPHA * x_ref[...] + mix, g_ref[...], b_ref[...])
    x1_ref[...] = x1
    hi = x1.astype(BF16)
    x1p_ref[...] = _pack_bf16_pairs(x1)
    lo = (x1 - hi.astype(F32)).astype(BF16)
    nt = (((1,), (1,)), ((), ()))
    logits = (lax.dot_general(wrh_ref[...], hi, nt, preferred_element_type=F32)
              + lax.dot_general(wrl_ref[...], hi, nt, preferred_element_type=F32)
              + lax.dot_general(wrh_ref[...], lo, nt, preferred_element_type=F32))
    m = jnp.max(logits, axis=0, keepdims=True)
    e = jnp.exp(logits - m)
    aff_ref[...] = e / jnp.sum(e, axis=0, keepdims=True)


def _mixer_out(x, attn, gm, w_out, ln_g, ln_b, wr_hi, wr_lo):
    B, L, D = x.shape
    tm = min(ROW_TILE, L)
    per_seq = L // tm
    row = lambda w: pl.BlockSpec((None, tm, w), lambda b, i: (b, i, 0))
    full = lambda shape: pl.BlockSpec(shape, lambda b, i: (0,) * len(shape))
    return pl.pallas_call(
        _mixer_out_kernel,
        grid=(B, per_seq),
        in_specs=[row(D), row(ATTN_WIDTH), row(GMLP_WIDTH)]
                 + [full(w_out.shape), full(ln_g.shape), full(ln_b.shape), full(wr_hi.shape), full(wr_lo.shape)],
        out_specs=[row(D), row(D // 2), pl.BlockSpec((N_EXPERTS, tm), lambda b, i: (0, b * per_seq + i))],
        out_shape=[jax.ShapeDtypeStruct((B, L, D), F32), jax.ShapeDtypeStruct((B, L, D // 2), I32),
                   jax.ShapeDtypeStruct((N_EXPERTS, B * L), F32)],
        compiler_params=_params("parallel", "parallel"),
        name="mixer_out",
    )(x, attn, gm, w_out, ln_g, ln_b, wr_hi, wr_lo)


def _threshold_kernel(aff_ref, tau_ref, need_ref, *, cap):
    bits = lax.bitcast_convert_type(aff_ref[...], I32)

    def step(i, tau):
        cand = tau | jnp.left_shift(jnp.int32(1), 30 - i)
        cnt = jnp.sum((bits >= cand).astype(F32), axis=1, keepdims=True)
        return jnp.where(cnt >= cap, cand, tau)

    tau = lax.fori_loop(0, 31, step, jnp.zeros((bits.shape[0], 1), I32))
    above = jnp.sum((bits > tau).astype(F32), axis=1, keepdims=True)
    tau_ref[...] = jnp.broadcast_to(tau, tau_ref.shape)
    need_ref[...] = jnp.broadcast_to(cap - above, need_ref.shape)


def _slot_kernel(aff_ref, tau_ref, need_ref, tri_ref, low_ref, slot_ref, rank_ref, span_ref, carry_sel, carry_eq):
    @pl.when(pl.program_id(0) == 0)
    def _():
        carry_sel[...] = jnp.zeros_like(carry_sel)
        carry_eq[...] = jnp.zeros_like(carry_eq)

    tau = tau_ref[:, 0:1]
    need = need_ref[:, 0:1]
    c_sel = carry_sel[:, 0:1]
    c_eq = carry_eq[:, 0:1]
    tri = tri_ref[...]
    for s in range(aff_ref.shape[1] // SCAN_TILE):
        sl = slice(s * SCAN_TILE, (s + 1) * SCAN_TILE)
        bits = lax.bitcast_convert_type(aff_ref[:, sl], I32)
        eq = (bits == tau).astype(F32)
        eq_incl = jnp.dot(eq.astype(BF16), tri, preferred_element_type=F32)
        tie_taken = (eq_incl - eq + c_eq) < need
        sel = jnp.where((bits > tau) | ((bits == tau) & tie_taken), 1.0, 0.0)
        sel_b = sel.astype(BF16)
        incl = jnp.dot(sel_b, tri, preferred_element_type=F32)
        before = (incl - sel + c_sel).astype(I32)
        slot_ref[:, sl] = jnp.where(sel > 0.0, before, -1)
        per_token = jnp.sum(sel, axis=0, keepdims=True)
        first = jnp.sum(incl, axis=0, keepdims=True) - per_token + jnp.sum(c_sel, axis=0, keepdims=True)
        lower = jnp.dot(low_ref[...], sel_b, preferred_element_type=F32)
        rank_ref[:, sl] = (first + lower).astype(I32)
        span_ref[0:1, sl] = first
        span_ref[1:2, sl] = first + per_token
        c_sel = c_sel + incl[:, SCAN_TILE - 1:SCAN_TILE]
        c_eq = c_eq + eq_incl[:, SCAN_TILE - 1:SCAN_TILE]
    carry_sel[...] = jnp.broadcast_to(c_sel, carry_sel.shape)
    carry_eq[...] = jnp.broadcast_to(c_eq, carry_eq.shape)


def _select(aff_t, cap):
    E, n = aff_t.shape
    stat = jax.ShapeDtypeStruct((E, LANES), I32)
    tau, need = pl.pallas_call(
        functools.partial(_threshold_kernel, cap=float(cap)),
        out_shape=[stat, jax.ShapeDtypeStruct((E, LANES), F32)],
        compiler_params=pltpu.CompilerParams(vmem_limit_bytes=VMEM_LIMIT),
        name="threshold",
    )(aff_t)
    tri = jnp.asarray(np.triu(np.ones((SCAN_TILE, SCAN_TILE), np.float32)), BF16)
    low = jnp.asarray(np.tril(np.ones((E, E), np.float32), -1), BF16)
    tb = min(SLOT_BLOCK, n)
    blk = pl.BlockSpec((E, tb), lambda i: (0, i))
    const = lambda shape: pl.BlockSpec(shape, lambda i: (0, 0))
    return pl.pallas_call(
        _slot_kernel,
        grid=(n // tb,),
        in_specs=[blk, const((E, LANES)), const((E, LANES)), const((SCAN_TILE, SCAN_TILE)), const((E, E))],
        out_specs=[blk, blk, pl.BlockSpec((2, tb), lambda i: (0, i))],
        out_shape=[jax.ShapeDtypeStruct((E, n), I32)] * 2 + [jax.ShapeDtypeStruct((2, n), F32)],
        scratch_shapes=[pltpu.VMEM((E, LANES), F32), pltpu.VMEM((E, LANES), F32)],
        compiler_params=_params("arbitrary"),
        name="slots",
    )(aff_t, tau, need, tri, low)


def _work_list(span, total_rows, t_chunk, t_rows):
    n = span.shape[1]
    nch, ntl = n // t_chunk, total_rows // t_rows
    kmax = N_EXPERTS * t_chunk // t_rows + 1
    base = span[0, ::t_chunk].astype(I32)
    end = jnp.concatenate([base[1:], jnp.full((1,), total_rows, I32)])
    first = jnp.minimum(base // t_rows, ntl - 1)
    last = jnp.where(end > base, (end - 1) // t_rows, first)
    tile = first[:, None] + jnp.arange(kmax, dtype=I32)
    valid = (tile <= last[:, None]).reshape(-1)
    tile = jnp.minimum(tile, ntl - 1).reshape(-1)
    chunk = jnp.repeat(jnp.arange(nch, dtype=I32), kmax)
    length = nch + ntl
    count = jnp.sum(valid.astype(I32))
    idx = jnp.nonzero(valid, size=length, fill_value=0)[0].astype(I32)
    ar = jnp.arange(length, dtype=I32)
    live = ar < count
    idx = jnp.where(live, idx, idx[count - 1])
    c, t = chunk[idx], tile[idx]
    prev = jnp.concatenate([jnp.full((1,), -1, I32), c[:-1]])
    nxt = jnp.concatenate([c[1:], jnp.full((1,), -1, I32)])
    is_first = live & ((ar == 0) | (c != prev))
    is_last = live & ((ar == count - 1) | (c != nxt))
    return c, t, live.astype(I32) + 2 * is_first.astype(I32) + 4 * is_last.astype(I32)


def _pack_bf16_pairs(x):
    bits = lax.bitcast_convert_type(x.astype(BF16).astype(F32), I32)
    half = bits.shape[1] // 2
    return lax.shift_right_logical(bits[:, :half], 16) | bits[:, half:]


def _unpack_bf16_pairs(w):
    left = lax.bitcast_convert_type(lax.shift_left(w, 16), F32)
    right = lax.bitcast_convert_type(w & jnp.int32(-65536), F32)
    return jnp.concatenate([left, right], axis=1).astype(BF16)


def _sc_layout(n_experts, cap):
    info = plsc.get_sparse_core_info()
    workers = info.num_cores * info.num_subcores
    per_expert = workers // n_experts
    assert per_expert * n_experts == workers and cap % (per_expert * SC_ROWS) == 0, (workers, n_experts, cap)
    return info.num_cores, info.num_lanes, per_expert, cap // per_expert


def _sc_worker(n_cores, per_expert):
    wid = lax.axis_index("s") * n_cores + lax.axis_index("c")
    return wid // per_expert, wid % per_expert


def _sc_invert(slot_hbm, payload_hbms, slot_v, payload_vs, e, part, share, n, chunk, lanes, store):
    @pl.loop(0, n // chunk)
    def _(ci):
        pltpu.sync_copy(slot_hbm.at[e, pl.ds(ci * chunk, chunk)], slot_v)
        for src, dst in zip(payload_hbms, payload_vs):
            pltpu.sync_copy(src.at[e, pl.ds(ci * chunk, chunk)], dst)

        @pl.loop(0, chunk // lanes)
        def _(i):
            local = slot_v[pl.ds(i * lanes, lanes)] - part * share
            mine = (local >= 0) & (local < share)
            tok = ci * chunk + i * lanes + lax.iota(I32, lanes)
            store(local, mine, tok, [v[pl.ds(i * lanes, lanes)] for v in payload_vs])


_SC_PARAMS = dataclasses.replace(pltpu.CompilerParams(), needs_layout_passes=False)


def _dispatch(x1p, slot, aff_t, cap):
    n, width = x1p.shape
    E = slot.shape[0]
    n_cores, lanes, per_expert, share = _sc_layout(E, cap)
    chunk = min(n, SC_SLOT_CHUNK)
    mesh = plsc.VectorSubcoreMesh(core_axis_name="c", subcore_axis_name="s")

    @functools.partial(
        pl.kernel, mesh=mesh, compiler_params=_SC_PARAMS,
        out_type=[jax.ShapeDtypeStruct((E * cap, width), I32), jax.ShapeDtypeStruct((E * cap,), F32)],
        scratch_types=[pltpu.VMEM((chunk,), I32), pltpu.VMEM((chunk,), F32), pltpu.VMEM((share,), I32),
                       pltpu.VMEM((share,), F32), pltpu.VMEM((SC_ROWS, width), I32), pltpu.SemaphoreType.DMA],
    )
    def gather(x_hbm, slot_hbm, aff_hbm, out_hbm, gate_hbm, slot_v, aff_v, idx_v, gate_v, rows_v, sem):
        e, part = _sc_worker(n_cores, per_expert)

        def store(local, mine, tok, payloads):
            plsc.store_scatter(idx_v, [local], tok, mask=mine)
            plsc.store_scatter(gate_v, [local], payloads[0], mask=mine)

        _sc_invert(slot_hbm, [aff_hbm], slot_v, [aff_v], e, part, share, n, chunk, lanes, store)
        base = e * cap + part * share
        pltpu.sync_copy(gate_v, gate_hbm.at[pl.ds(base, share)])

        @pl.loop(0, share // SC_ROWS)
        def _(j):
            pltpu.async_copy(x_hbm.at[idx_v.at[pl.ds(j * SC_ROWS, SC_ROWS)]], rows_v, sem).wait()
            pltpu.sync_copy(rows_v, out_hbm.at[pl.ds(base + j * SC_ROWS, SC_ROWS)])

    xs, gates = gather(x1p, slot, aff_t)
    return xs.reshape(E, cap, width), gates.reshape(E, cap // LANES, LANES)


def _to_token_order(ysp, slot, rank):
    E, cap, width = ysp.shape
    n = slot.shape[1]
    n_cores, lanes, per_expert, share = _sc_layout(E, cap)
    chunk = min(n, SC_SLOT_CHUNK)
    mesh = plsc.VectorSubcoreMesh(core_axis_name="c", subcore_axis_name="s")

    @functools.partial(
        pl.kernel, mesh=mesh, compiler_params=_SC_PARAMS,
        out_type=jax.ShapeDtypeStruct((E * cap, width), I32),
        scratch_types=[pltpu.VMEM((chunk,), I32), pltpu.VMEM((chunk,), I32),
                       pltpu.VMEM((share // SC_ROWS, SC_ROWS), I32),
                       pltpu.VMEM((SC_ROWS, width), I32), pltpu.SemaphoreType.DMA],
    )
    def scatter(y_hbm, slot_hbm, rank_hbm, z_hbm, slot_v, rank_v, dest_v, rows_v, sem):
        e, part = _sc_worker(n_cores, per_expert)

        def store(local, mine, tok, payloads):
            plsc.store_scatter(dest_v, [local // SC_ROWS, local % SC_ROWS], payloads[0], mask=mine)

        _sc_invert(slot_hbm, [rank_hbm], slot_v, [rank_v], e, part, share, n, chunk, lanes, store)
        base = e * cap + part * share

        @pl.loop(0, share // SC_ROWS)
        def _(j):
            pltpu.sync_copy(y_hbm.at[pl.ds(base + j * SC_ROWS, SC_ROWS)], rows_v)
            pltpu.async_copy(rows_v, z_hbm.at[dest_v.at[j]], sem).wait()

    return scatter(ysp.reshape(E * cap, width), slot, rank)


def _ffn_kernel(x_ref, gate_ref, wg_ref, wu_ref, wd_ref, y_ref, acc_ref):
    x = _unpack_bf16_pairs(x_ref[...])
    for c in range(D_FF // FFN_COLS):
        cols = slice(c * FFN_COLS, (c + 1) * FFN_COLS)
        g = jnp.dot(x, wg_ref[:, cols], preferred_element_type=F32)
        u = jnp.dot(x, wu_ref[:, cols], preferred_element_type=F32)
        h = (g * jax.nn.sigmoid(g) * u).astype(BF16)
        part = jnp.dot(h, wd_ref[cols, :], preferred_element_type=F32)
        if c == 0:
            acc_ref[...] = part
        else:
            acc_ref[...] += part

    gates = gate_ref[...]
    pad = jnp.zeros((LANES - gates.shape[0], LANES), F32)
    scale = jnp.concatenate([gates, pad], axis=0).T
    for j in range(gates.shape[0]):
        rows = slice(j * LANES, (j + 1) * LANES)
        y_ref[rows, :] = _pack_bf16_pairs(acc_ref[rows, :] * scale[:, j:j + 1])


def _experts(xs, gates, w_gate, w_up, w_down):
    E, cap, packed = xs.shape
    D = 2 * packed
    tm = min(FFN_ROWS, cap)
    weight = lambda shape: pl.BlockSpec((None,) + shape, lambda e, m: (e, 0, 0))
    return pl.pallas_call(
        _ffn_kernel,
        grid=(E, cap // tm),
        in_specs=[pl.BlockSpec((None, tm, packed), lambda e, m: (e, m, 0)),
                  pl.BlockSpec((None, tm // LANES, LANES), lambda e, m: (e, m, 0)),
                  weight((D, D_FF)), weight((D, D_FF)), weight((D_FF, D))],
        out_specs=pl.BlockSpec((None, tm, packed), lambda e, m: (e, m, 0)),
        out_shape=jax.ShapeDtypeStruct((E, cap, packed), I32),
        scratch_shapes=[pltpu.VMEM((tm, D), F32)],
        compiler_params=_params("parallel", "arbitrary"),
        name="experts",
    )(xs, gates, w_gate, w_up, w_down)


def _combine_kernel(wc_ref, wt_ref, wf_ref, z_ref, span_ref, x1_ref, g_ref, b_ref, o_ref):
    w = pl.program_id(0)
    flags = wf_ref[w]

    @pl.when((flags & 2) == 2)
    def _():
        o_ref[...] = jnp.zeros_like(o_ref)

    @pl.when((flags & 1) == 1)
    def _():
        tr, tc = z_ref.shape[0], o_ref.shape[0]
        row = (lax.broadcasted_iota(I32, (tc, tr), 1) + wt_ref[w] * tr).astype(F32)
        owns = ((row >= span_ref[:, 0:1]) & (row < span_ref[:, 1:2])).astype(BF16)
        o_ref[...] += jnp.dot(owns, _unpack_bf16_pairs(z_ref[...]), preferred_element_type=F32)

    @pl.when((flags & 4) == 4)
    def _():
        o_ref[...] = _layer_norm(ALPHA * x1_ref[...] + o_ref[...], g_ref[...], b_ref[...])


def _combine(z, span_tm, x1, ln_g, ln_b, work):
    rows, packed = z.shape
    n, D = x1.shape
    length = work[0].shape[0]
    chunk = lambda w, wc, wt, wf: (wc[w], 0)
    const = lambda w, wc, wt, wf: (0, 0)
    return pl.pallas_call(
        _combine_kernel,
        grid_spec=pltpu.PrefetchScalarGridSpec(
            num_scalar_prefetch=3,
            grid=(length,),
            in_specs=[pl.BlockSpec((COMBINE_ROWS, packed), lambda w, wc, wt, wf: (wt[w], 0)),
                      pl.BlockSpec((COMBINE_CHUNK, 2), chunk),
                      pl.BlockSpec((COMBINE_CHUNK, D), chunk),
                      pl.BlockSpec((1, D), const), pl.BlockSpec((1, D), const)],
            out_specs=pl.BlockSpec((COMBINE_CHUNK, D), chunk),
        ),
        out_shape=jax.ShapeDtypeStruct((n, D), F32),
        compiler_params=_params("arbitrary"),
        name="combine",
    )(*work, z, span_tm, x1, ln_g, ln_b)


def _prepare_weights(rel_bias_table, w_in, b_in, gmlp_ln_g, gmlp_ln_b, gmlp_w_s, gmlp_b_s, w_out,
                     ln1_g, ln1_b, w_router, w_gate, w_up, w_down, ln2_g, ln2_b):
    row = lambda t: t[0].reshape(1, -1).astype(F32)
    ws = gmlp_w_s[0].astype(BF16)
    ws_pairs = jnp.concatenate([ws[0::2], ws[1::2]], axis=-1)
    bs_full = jnp.repeat(gmlp_b_s[0].T, HEAD_DIM, axis=1).astype(F32)
    wr_t = w_router[0].T.astype(F32)
    wr_hi = wr_t.astype(BF16)
    wr_lo = (wr_t - wr_hi.astype(F32)).astype(BF16)
    return dict(
        rel=rel_bias_table.astype(F32), w_in=w_in[0].astype(BF16), b_in=row(b_in),
        gln_g=row(gmlp_ln_g), gln_b=row(gmlp_ln_b), ws_pairs=ws_pairs, bs_full=bs_full,
        w_out=w_out[0].astype(BF16), ln1_g=row(ln1_g), ln1_b=row(ln1_b), wr_hi=wr_hi, wr_lo=wr_lo,
        w_gate=w_gate[0].astype(BF16), w_up=w_up[0].astype(BF16), w_down=w_down[0].astype(BF16),
        ln2_g=row(ln2_g), ln2_b=row(ln2_b))


def _trunk(x, p):
    B, L, D = x.shape
    n = B * L
    cap = CAPACITY_FACTOR * n // N_EXPERTS
    q, k, v, gm = _mixer_in(x, p["w_in"], p["b_in"], p["gln_g"], p["gln_b"], p["ws_pairs"], p["bs_full"])
    attn = _attention(q, k, v, p["rel"])
    x1, x1p, aff_t = _mixer_out(x, attn, gm, p["w_out"], p["ln1_g"], p["ln1_b"], p["wr_hi"], p["wr_lo"])
    slot, rank, span = _select(aff_t, cap)
    xs, gates = _dispatch(x1p.reshape(n, D // 2), slot, aff_t, cap)
    ys = _experts(xs, gates, p["w_gate"], p["w_up"], p["w_down"])
    z = _to_token_order(ys, slot, rank)
    work = _work_list(span, N_EXPERTS * cap, COMBINE_CHUNK, COMBINE_ROWS)
    y = _combine(z, span.T, x1.reshape(n, D), p["ln2_g"], p["ln2_b"], work)
    return y.reshape(B, L, D)


def kernel(x_prompt, x_sample, rel_bias_table, w_in, b_in, gmlp_ln_g, gmlp_ln_b, gmlp_w_s, gmlp_b_s, w_out,
           ln1_g, ln1_b, w_router, w_gate, w_up, w_down, ln2_g, ln2_b):
    p = _prepare_weights(rel_bias_table, w_in, b_in, gmlp_ln_g, gmlp_ln_b, gmlp_w_s, gmlp_b_s, w_out,
                         ln1_g, ln1_b, w_router, w_gate, w_up, w_down, ln2_g, ln2_b)
    return (_trunk(x_prompt, p), _trunk(x_sample, p))
```

```python
import dataclasses
import functools
import math

import numpy as np
import jax
import jax.numpy as jnp
from jax import lax
from jax.experimental import pallas as pl
from jax.experimental.pallas import tpu as pltpu
from jax.experimental.pallas import tpu_sc as plsc

F32 = jnp.float32
BF16 = jnp.bfloat16
I32 = jnp.int32

HEAD_DIM = 64
ATTN_WIDTH = 512
GMLP_WIDTH = 512
N_HEADS = ATTN_WIDTH // HEAD_DIM
IN_WIDTH = 3 * ATTN_WIDTH + 2 * GMLP_WIDTH
GMLP_CHUNK = 128
DILATIONS = (16, 4, 1)
HALF_WINDOW = 64
N_BUCKETS = 32
REL_MAX_DISTANCE = 1024
N_EXPERTS = 16
CAPACITY_FACTOR = 2
D_FF = 2816
ALPHA = 2.0 ** 0.25
LN_EPS = 1e-5
MASK_VALUE = -1e30

LANES = 128
BF16_SUBLANES = 16
VMEM_LIMIT = 56 * 1024 * 1024

Q_TILE = 128
HEAD_GROUP = 4
HG_WIDTH = HEAD_GROUP * HEAD_DIM
ATTN_UNROLL = 4
ATTN_SPAN = 2048
ROW_TILE = 1024
SCAN_TILE = 256
SLOT_BLOCK = 2048
SC_SLOT_CHUNK = 4096
SC_ROWS = 64
COMBINE_CHUNK = 512
COMBINE_ROWS = 512
FFN_ROWS = 1024
FFN_COLS = 256


def _params(*sem):
    return pltpu.CompilerParams(dimension_semantics=sem, vmem_limit_bytes=VMEM_LIMIT)


def _layer_norm(y, g, b):
    mu = jnp.mean(y, axis=-1, keepdims=True)
    yc = y - mu
    var = jnp.mean(yc * yc, axis=-1, keepdims=True)
    return yc * lax.rsqrt(var + LN_EPS) * g + b


def _gelu_tanh(x):
    return 0.5 * x * (1.0 + jnp.tanh(math.sqrt(2.0 / math.pi) * (x + 0.044715 * (x * x * x))))


def _mixer_in_kernel(x_ref, w_ref, b_ref, lng_ref, lnb_ref, ws_ref, bs_ref,
                     q_ref, k_ref, v_ref, gm_ref):
    x = x_ref[...].astype(BF16)

    def proj(lo, hi):
        return jnp.dot(x, w_ref[:, lo:hi], preferred_element_type=F32) + b_ref[:, lo:hi]

    a = ATTN_WIDTH
    for i, ref in enumerate((q_ref, k_ref, v_ref)):
        t = proj(i * a, (i + 1) * a)
        if i == 0:
            t = t * (HEAD_DIM ** -0.5)
        for s in range(a // LANES):
            ref[s] = t[:, s * LANES:(s + 1) * LANES]
    gu = _gelu_tanh(proj(3 * a, 3 * a + GMLP_WIDTH))
    gv = _gelu_tanh(proj(3 * a + GMLP_WIDTH, IN_WIDTH))
    vn = _layer_norm(gv, lng_ref[...], lnb_ref[...]).astype(BF16)

    rows = x_ref.shape[0]
    low_half = lax.broadcasted_iota(I32, (1, LANES), 1) < HEAD_DIM
    for c in range(rows // GMLP_CHUNK):
        r0 = c * GMLP_CHUNK
        for s in range(GMLP_WIDTH // LANES):
            c0 = s * LANES
            vs = vn[r0:r0 + GMLP_CHUNK, c0:c0 + LANES]
            zero = jnp.zeros_like(vs)
            rhs = jnp.concatenate([jnp.where(low_half, vs, zero), jnp.where(low_half, zero, vs)], axis=0)
            vm = jnp.dot(ws_ref[s], rhs, preferred_element_type=F32) + bs_ref[:, c0:c0 + LANES]
            gm_ref[r0:r0 + GMLP_CHUNK, c0:c0 + LANES] = (gu[r0:r0 + GMLP_CHUNK, c0:c0 + LANES] * vm).astype(BF16)


def _mixer_in(x, w_in, b_in, ln_g, ln_b, ws_pairs, bs_full):
    B, L, D = x.shape
    tm = min(ROW_TILE, L)
    row = lambda w: pl.BlockSpec((None, tm, w), lambda b, i: (b, i, 0))
    full = lambda shape: pl.BlockSpec(shape, lambda b, i: (0,) * len(shape))
    n_slab = ATTN_WIDTH // LANES
    slab = pl.BlockSpec((None, n_slab, tm, LANES), lambda b, i: (b, 0, i, 0))
    slab_shape = jax.ShapeDtypeStruct((B, n_slab, L, LANES), F32)
    return pl.pallas_call(
        _mixer_in_kernel,
        grid=(B, L // tm),
        in_specs=[row(D), full(w_in.shape), full(b_in.shape), full(ln_g.shape), full(ln_b.shape),
                  full(ws_pairs.shape), full(bs_full.shape)],
        out_specs=[slab, slab, slab, row(GMLP_WIDTH)],
        out_shape=[slab_shape] * 3 + [jax.ShapeDtypeStruct((B, L, GMLP_WIDTH), BF16)],
        compiler_params=_params("parallel", "parallel"),
        name="mixer_in",
    )(x, w_in, b_in, ln_g, ln_b, ws_pairs, bs_full)


def _attn_kernel(b16_ref, b4_ref, b1_ref, q_ref, k_ref, v_ref, out_ref, o_acc, m_acc, l_acc, *, seq_len):
    span = out_ref.shape[0]
    span_idx = pl.program_id(2)
    n_slab = HG_WIDTH // LANES
    lane = lax.broadcasted_iota(I32, (1, HG_WIDTH), 1)
    head_masks = [(lane >= h * HEAD_DIM) & (lane < (h + 1) * HEAD_DIM) for h in range(HEAD_GROUP)]

    def rows(ref, start, size, stride):
        idx = pl.ds(start, size) if stride == 1 else pl.ds(start, size, stride=stride)
        return jnp.concatenate([ref[s, idx, :] for s in range(n_slab)], axis=1)

    def put(ref, start, stride, val):
        idx = pl.ds(start, Q_TILE) if stride == 1 else pl.ds(start, Q_TILE, stride=stride)
        for s in range(n_slab):
            ref[s, idx, :] = val[:, s * LANES:(s + 1) * LANES]

    def per_head(stacked):
        out = jnp.zeros((Q_TILE, HG_WIDTH), F32)
        for h in range(HEAD_GROUP):
            out = jnp.where(head_masks[h], stacked[h * Q_TILE:(h + 1) * Q_TILE], out)
        return out

    branches = tuple(zip(DILATIONS, (b16_ref, b4_ref, b1_ref)))
    for bi, (d, bias_ref) in enumerate(branches):
        lr = seq_len // d
        tk = min(2 * Q_TILE, lr)
        tiles_total = lr // Q_TILE
        tiles_per_residue = span // d // Q_TILE
        first, final = bi == 0, bi == len(branches) - 1

        def tile(idx, carry, d=d, bias_ref=bias_ref, lr=lr, tk=tk, tiles_total=tiles_total,
                 tiles_per_residue=tiles_per_residue, first=first, final=final):
            r = idx // tiles_per_residue
            jt = idx % tiles_per_residue
            jg = span_idx * tiles_per_residue + jt
            start = jnp.clip(jg * Q_TILE - HALF_WINDOW, 0, lr - tk)
            variant = jnp.where(jg == 0, 0, jnp.where(jg == tiles_total - 1, 2, 1))
            q_row = r + d * (jt * Q_TILE)
            q = rows(q_ref, q_row, Q_TILE, d).astype(BF16)
            kw = rows(k_ref, r + d * start, tk, d).astype(BF16)
            vw = rows(v_ref, r + d * start, tk, d).astype(BF16)
            qs = jnp.concatenate([jnp.where(hm, q, jnp.zeros_like(q)) for hm in head_masks], axis=0)
            s = lax.dot_general(qs, kw, (((1,), (1,)), ((), ())), preferred_element_type=F32)
            s = s + bias_ref[variant]
            m = jnp.max(s, axis=1, keepdims=True)
            p = jnp.exp(s - m)
            l = jnp.sum(p, axis=1, keepdims=True)
            pv = jnp.dot(p.astype(BF16), vw, preferred_element_type=F32)
            o_t = per_head(pv)
            m_t, l_t = per_head(m), per_head(l)
            if not first:
                m_old = rows(m_acc, q_row, Q_TILE, d)
                m_new = jnp.maximum(m_old, m_t)
                a_old, a_t = jnp.exp(m_old - m_new), jnp.exp(m_t - m_new)
                l_t = a_old * rows(l_acc, q_row, Q_TILE, d) + a_t * l_t
                o_t = a_old * rows(o_acc, q_row, Q_TILE, d) + a_t * o_t
                m_t = m_new
            if final:
                out_ref[pl.ds(pl.multiple_of(q_row, Q_TILE), Q_TILE), :] = (o_t / l_t).astype(BF16)
            else:
                put(o_acc, q_row, d, o_t)
                put(m_acc, q_row, d, m_t)
                put(l_acc, q_row, d, l_t)
            return carry

        lax.fori_loop(0, span // Q_TILE, tile, 0, unroll=ATTN_UNROLL)


def _t5_bucket(rel):
    half = N_BUCKETS // 2
    ret = np.where(rel > 0, half, 0)
    n = np.abs(rel)
    max_exact = half // 2
    large = max_exact + (np.log(np.maximum(n, 1) / max_exact) / np.log(REL_MAX_DISTANCE / max_exact)
                         * (half - max_exact)).astype(np.int32)
    large = np.minimum(large, half - 1)
    return (ret + np.where(n < max_exact, n, large)).astype(np.int32)


def _bias_tables(rel_table, dilation, tk):
    buckets, valids = [], []
    for delta in (0, -HALF_WINDOW, Q_TILE - tk):
        off = delta + np.arange(tk)[None, :] - np.arange(Q_TILE)[:, None]
        valids.append(np.abs(off) <= HALF_WINDOW)
        buckets.append(_t5_bucket(np.clip(off, -HALF_WINDOW, HALF_WINDOW) * dilation))
    bucket = jnp.asarray(np.stack(buckets), I32)
    onehot = (bucket[..., None] == jnp.arange(N_BUCKETS, dtype=I32)).astype(F32)
    bias = jnp.einsum("vqkb,bh->hvqk", onehot, rel_table, precision=lax.Precision.HIGHEST)
    bias = jnp.where(jnp.asarray(np.stack(valids))[None], bias, MASK_VALUE)
    n_hg = N_HEADS // HEAD_GROUP
    bias = bias.reshape(n_hg, HEAD_GROUP, 3, Q_TILE, tk).transpose(0, 2, 1, 3, 4)
    return bias.reshape(n_hg, 3, HEAD_GROUP * Q_TILE, tk)


def _attn_with_casts_kernel(*refs, seq_len, n_casts):
    n_in = 6 + n_casts
    for src, dst in zip(refs[6:n_in], refs[n_in + 1:n_in + 1 + n_casts]):
        dst[...] = src[...].astype(BF16)
    _attn_kernel(*refs[:6], refs[n_in], *refs[n_in + 1 + n_casts:], seq_len=seq_len)


def _attention(q, k, v, rel_table, casts=()):
    B, _, L, _ = q.shape
    span = min(L, ATTN_SPAN)
    assert L % span == 0 and span % (max(DILATIONS) * Q_TILE) == 0, (L, span)
    n_hg = N_HEADS // HEAD_GROUP
    slabs = HG_WIDTH // LANES
    spans = L // span
    n_steps = n_hg * B * spans
    biases = [_bias_tables(rel_table, d, min(2 * Q_TILE, L // d)) for d in DILATIONS]
    once = pl.Buffered(1)
    bias_spec = lambda t: pl.BlockSpec((None,) + t.shape[1:], lambda g, b, s: (g, 0, 0, 0), pipeline_mode=once)
    seq = pl.BlockSpec((None, slabs, L, LANES), lambda g, b, s: (b, g, 0, 0))
    for t in casts:
        assert t.shape[0] % (n_steps * BF16_SUBLANES) == 0, (t.shape, n_steps)
    cast_specs = [pl.BlockSpec((t.shape[0] // n_steps, t.shape[1]), lambda g, b, s: ((g * B + b) * spans + s, 0))
                  for t in casts]
    outs = pl.pallas_call(
        functools.partial(_attn_with_casts_kernel, seq_len=L, n_casts=len(casts)),
        grid=(n_hg, B, spans),
        in_specs=[bias_spec(t) for t in biases]
                 + [pl.BlockSpec((None, slabs, span, LANES), lambda g, b, s: (b, g, s, 0)), seq, seq] + cast_specs,
        out_specs=[pl.BlockSpec((None, span, HG_WIDTH), lambda g, b, s: (b, s, g))] + cast_specs,
        out_shape=[jax.ShapeDtypeStruct((B, L, ATTN_WIDTH), BF16)]
                  + [jax.ShapeDtypeStruct(t.shape, BF16) for t in casts],
        scratch_shapes=[pltpu.VMEM((slabs, span, LANES), F32)] * 3,
        compiler_params=_params("parallel", "parallel", "arbitrary"),
        name="attention",
    )(*biases, q, k, v, *casts)
    return outs[0], tuple(outs[1:])


def _mixer_out_kernel(x_ref, attn_ref, gm_ref, wo_ref, g_ref, b_ref, wrh_ref, wrl_ref,
                      x1_ref, x1p_ref, aff_ref):
    mix = jnp.dot(attn_ref[...], wo_ref[0:ATTN_WIDTH, :], preferred_element_type=F32)
    mix = mix + jnp.dot(gm_ref[...], wo_ref[ATTN_WIDTH:, :], preferred_element_type=F32)
    x1 = _layer_norm(ALPHA * x_ref[...] + mix, g_ref[...], b_ref[...])
    x1_ref[...] = x1
    hi = x1.astype(BF16)
    x1p_ref[...] = _pack_bf16_pairs(x1)
    lo = (x1 - hi.astype(F32)).astype(BF16)
    nt = (((1,), (1,)), ((), ()))
    logits = (lax.dot_general(wrh_ref[...], hi, nt, preferred_element_type=F32)
              + lax.dot_general(wrl_ref[...], hi, nt, preferred_element_type=F32)
              + lax.dot_general(wrh_ref[...], lo, nt, preferred_element_type=F32))
    m = jnp.max(logits, axis=0, keepdims=True)
    e = jnp.exp(logits - m)
    aff_ref[...] = e / jnp.sum(e, axis=0, keepdims=True)


def _mixer_out(x, attn, gm, w_out, ln_g, ln_b, wr_hi, wr_lo):
    B, L, D = x.shape
    tm = min(ROW_TILE, L)
    per_seq = L // tm
    row = lambda w: pl.BlockSpec((None, tm, w), lambda b, i: (b, i, 0))
    full = lambda shape: pl.BlockSpec(shape, lambda b, i: (0,) * len(shape))
    return pl.pallas_call(
        _mixer_out_kernel,
        grid=(B, per_seq),
        in_specs=[row(D), row(ATTN_WIDTH), row(GMLP_WIDTH)]
                 + [full(w_out.shape), full(ln_g.shape), full(ln_b.shape), full(wr_hi.shape), full(wr_lo.shape)],
        out_specs=[row(D), row(D // 2), pl.BlockSpec((N_EXPERTS, tm), lambda b, i: (0, b * per_seq + i))],
        out_shape=[jax.ShapeDtypeStruct((B, L, D), F32), jax.ShapeDtypeStruct((B, L, D // 2), I32),
                   jax.ShapeDtypeStruct((N_EXPERTS, B * L), F32)],
        compiler_params=_params("parallel", "parallel"),
        name="mixer_out",
    )(x, attn, gm, w_out, ln_g, ln_b, wr_hi, wr_lo)


def _threshold_kernel(aff_ref, tau_ref, need_ref, *, cap):
    bits = lax.bitcast_convert_type(aff_ref[...], I32)

    def step(i, tau):
        cand = tau | jnp.left_shift(jnp.int32(1), 30 - i)
        cnt = jnp.sum((bits >= cand).astype(F32), axis=1, keepdims=True)
        return jnp.where(cnt >= cap, cand, tau)

    tau = lax.fori_loop(0, 31, step, jnp.zeros((bits.shape[0], 1), I32))
    above = jnp.sum((bits > tau).astype(F32), axis=1, keepdims=True)
    tau_ref[...] = jnp.broadcast_to(tau, tau_ref.shape)
    need_ref[...] = jnp.broadcast_to(cap - above, need_ref.shape)


def _slot_kernel(aff_ref, tau_ref, need_ref, tri_ref, low_ref, slot_ref, rank_ref, span_ref, carry_sel, carry_eq):
    @pl.when(pl.program_id(0) == 0)
    def _():
        carry_sel[...] = jnp.zeros_like(carry_sel)
        carry_eq[...] = jnp.zeros_like(carry_eq)

    tau = tau_ref[:, 0:1]
    need = need_ref[:, 0:1]
    c_sel = carry_sel[:, 0:1]
    c_eq = carry_eq[:, 0:1]
    tri = tri_ref[...]
    for s in range(aff_ref.shape[1] // SCAN_TILE):
        sl = slice(s * SCAN_TILE, (s + 1) * SCAN_TILE)
        bits = lax.bitcast_convert_type(aff_ref[:, sl], I32)
        eq = (bits == tau).astype(F32)
        eq_incl = jnp.dot(eq.astype(BF16), tri, preferred_element_type=F32)
        tie_taken = (eq_incl - eq + c_eq) < need
        sel = jnp.where((bits > tau) | ((bits == tau) & tie_taken), 1.0, 0.0)
        sel_b = sel.astype(BF16)
        incl = jnp.dot(sel_b, tri, preferred_element_type=F32)
        before = (incl - sel + c_sel).astype(I32)
        slot_ref[:, sl] = jnp.where(sel > 0.0, before, -1)
        per_token = jnp.sum(sel, axis=0, keepdims=True)
        first = jnp.sum(incl, axis=0, keepdims=True) - per_token + jnp.sum(c_sel, axis=0, keepdims=True)
        lower = jnp.dot(low_ref[...], sel_b, preferred_element_type=F32)
        rank_ref[:, sl] = (first + lower).astype(I32)
        span_ref[0:1, sl] = first
        span_ref[1:2, sl] = first + per_token
        c_sel = c_sel + incl[:, SCAN_TILE - 1:SCAN_TILE]
        c_eq = c_eq + eq_incl[:, SCAN_TILE - 1:SCAN_TILE]
    carry_sel[...] = jnp.broadcast_to(c_sel, carry_sel.shape)
    carry_eq[...] = jnp.broadcast_to(c_eq, carry_eq.shape)


def _select(aff_t, cap):
    E, n = aff_t.shape
    stat = jax.ShapeDtypeStruct((E, LANES), I32)
    tau, need = pl.pallas_call(
        functools.partial(_threshold_kernel, cap=float(cap)),
        out_shape=[stat, jax.ShapeDtypeStruct((E, LANES), F32)],
        compiler_params=pltpu.CompilerParams(vmem_limit_bytes=VMEM_LIMIT),
        name="threshold",
    )(aff_t)
    tri = jnp.asarray(np.triu(np.ones((SCAN_TILE, SCAN_TILE), np.float32)), BF16)
    low = jnp.asarray(np.tril(np.ones((E, E), np.float32), -1), BF16)
    tb = min(SLOT_BLOCK, n)
    blk = pl.BlockSpec((E, tb), lambda i: (0, i))
    const = lambda shape: pl.BlockSpec(shape, lambda i: (0, 0))
    return pl.pallas_call(
        _slot_kernel,
        grid=(n // tb,),
        in_specs=[blk, const((E, LANES)), const((E, LANES)), const((SCAN_TILE, SCAN_TILE)), const((E, E))],
        out_specs=[blk, blk, pl.BlockSpec((2, tb), lambda i: (0, i))],
        out_shape=[jax.ShapeDtypeStruct((E, n), I32)] * 2 + [jax.ShapeDtypeStruct((2, n), F32)],
        scratch_shapes=[pltpu.VMEM((E, LANES), F32), pltpu.VMEM((E, LANES), F32)],
        compiler_params=_params("arbitrary"),
        name="slots",
    )(aff_t, tau, need, tri, low)


def _work_list(span, total_rows, t_chunk, t_rows):
    n = span.shape[1]
    nch, ntl = n // t_chunk, total_rows // t_rows
    kmax = N_EXPERTS * t_chunk // t_rows + 1
    base = span[0, ::t_chunk].astype(I32)
    end = jnp.concatenate([base[1:], jnp.full((1,), total_rows, I32)])
    first = jnp.minimum(base // t_rows, ntl - 1)
    last = jnp.where(end > base, (end - 1) // t_rows, first)
    tile = first[:, None] + jnp.arange(kmax, dtype=I32)
    valid = (tile <= last[:, None]).reshape(-1)
    tile = jnp.minimum(tile, ntl - 1).reshape(-1)
    chunk = jnp.repeat(jnp.arange(nch, dtype=I32), kmax)
    length = nch + ntl
    count = jnp.sum(valid.astype(I32))
    idx = jnp.nonzero(valid, size=length, fill_value=0)[0].astype(I32)
    ar = jnp.arange(length, dtype=I32)
    live = ar < count
    idx = jnp.where(live, idx, idx[count - 1])
    c, t = chunk[idx], tile[idx]
    prev = jnp.concatenate([jnp.full((1,), -1, I32), c[:-1]])
    nxt = jnp.concatenate([c[1:], jnp.full((1,), -1, I32)])
    is_first = live & ((ar == 0) | (c != prev))
    is_last = live & ((ar == count - 1) | (c != nxt))
    return c, t, live.astype(I32) + 2 * is_first.astype(I32) + 4 * is_last.astype(I32)


def _pack_bf16_pairs(x):
    bits = lax.bitcast_convert_type(x.astype(BF16).astype(F32), I32)
    half = bits.shape[1] // 2
    return lax.shift_right_logical(bits[:, :half], 16) | bits[:, half:]


def _unpack_bf16_pairs(w):
    left = lax.bitcast_convert_type(lax.shift_left(w, 16), F32)
    right = lax.bitcast_convert_type(w & jnp.int32(-65536), F32)
    return jnp.concatenate([left, right], axis=1).astype(BF16)


def _sc_layout(n_experts, cap):
    info = plsc.get_sparse_core_info()
    workers = info.num_cores * info.num_subcores
    per_expert = workers // n_experts
    assert per_expert * n_experts == workers and cap % (per_expert * SC_ROWS) == 0, (workers, n_experts, cap)
    return info.num_cores, info.num_lanes, per_expert, cap // per_expert


def _sc_worker(n_cores, per_expert):
    wid = lax.axis_index("s") * n_cores + lax.axis_index("c")
    return wid // per_expert, wid % per_expert


def _sc_invert(slot_hbm, payload_hbms, slot_v, payload_vs, e, part, share, n, chunk, lanes, store):
    @pl.loop(0, n // chunk)
    def _(ci):
        pltpu.sync_copy(slot_hbm.at[e, pl.ds(ci * chunk, chunk)], slot_v)
        for src, dst in zip(payload_hbms, payload_vs):
            pltpu.sync_copy(src.at[e, pl.ds(ci * chunk, chunk)], dst)

        @pl.loop(0, chunk // lanes)
        def _(i):
            local = slot_v[pl.ds(i * lanes, lanes)] - part * share
            mine = (local >= 0) & (local < share)
            tok = ci * chunk + i * lanes + lax.iota(I32, lanes)
            store(local, mine, tok, [v[pl.ds(i * lanes, lanes)] for v in payload_vs])


_SC_PARAMS = dataclasses.replace(pltpu.CompilerParams(), needs_layout_passes=False)


def _dispatch(x1p, slot, aff_t, cap):
    n, width = x1p.shape
    E = slot.shape[0]
    n_cores, lanes, per_expert, share = _sc_layout(E, cap)
    chunk = min(n, SC_SLOT_CHUNK)
    mesh = plsc.VectorSubcoreMesh(core_axis_name="c", subcore_axis_name="s")

    @functools.partial(
        pl.kernel, mesh=mesh, compiler_params=_SC_PARAMS,
        out_type=[jax.ShapeDtypeStruct((E * cap, width), I32), jax.ShapeDtypeStruct((E * cap,), F32)],
        scratch_types=[pltpu.VMEM((chunk,), I32), pltpu.VMEM((chunk,), F32), pltpu.VMEM((share,), I32),
                       pltpu.VMEM((share,), F32), pltpu.VMEM((SC_ROWS, width), I32), pltpu.SemaphoreType.DMA],
    )
    def gather(x_hbm, slot_hbm, aff_hbm, out_hbm, gate_hbm, slot_v, aff_v, idx_v, gate_v, rows_v, sem):
        e, part = _sc_worker(n_cores, per_expert)

        def store(local, mine, tok, payloads):
            plsc.store_scatter(idx_v, [local], tok, mask=mine)
            plsc.store_scatter(gate_v, [local], payloads[0], mask=mine)

        _sc_invert(slot_hbm, [aff_hbm], slot_v, [aff_v], e, part, share, n, chunk, lanes, store)
        base = e * cap + part * share
        pltpu.sync_copy(gate_v, gate_hbm.at[pl.ds(base, share)])

        @pl.loop(0, share // SC_ROWS)
        def _(j):
            pltpu.async_copy(x_hbm.at[idx_v.at[pl.ds(j * SC_ROWS, SC_ROWS)]], rows_v, sem).wait()
            pltpu.sync_copy(rows_v, out_hbm.at[pl.ds(base + j * SC_ROWS, SC_ROWS)])

    xs, gates = gather(x1p, slot, aff_t)
    return xs.reshape(E, cap, width), gates.reshape(E, cap // LANES, LANES)


def _to_token_order(ysp, slot, rank):
    E, cap, width = ysp.shape
    n = slot.shape[1]
    n_cores, lanes, per_expert, share = _sc_layout(E, cap)
    chunk = min(n, SC_SLOT_CHUNK)
    mesh = plsc.VectorSubcoreMesh(core_axis_name="c", subcore_axis_name="s")

    @functools.partial(
        pl.kernel, mesh=mesh, compiler_params=_SC_PARAMS,
        out_type=jax.ShapeDtypeStruct((E * cap, width), I32),
        scratch_types=[pltpu.VMEM((chunk,), I32), pltpu.VMEM((chunk,), I32),
                       pltpu.VMEM((share // SC_ROWS, SC_ROWS), I32),
                       pltpu.VMEM((SC_ROWS, width), I32), pltpu.SemaphoreType.DMA],
    )
    def scatter(y_hbm, slot_hbm, rank_hbm, z_hbm, slot_v, rank_v, dest_v, rows_v, sem):
        e, part = _sc_worker(n_cores, per_expert)

        def store(local, mine, tok, payloads):
            plsc.store_scatter(dest_v, [local // SC_ROWS, local % SC_ROWS], payloads[0], mask=mine)

        _sc_invert(slot_hbm, [rank_hbm], slot_v, [rank_v], e, part, share, n, chunk, lanes, store)
        base = e * cap + part * share

        @pl.loop(0, share // SC_ROWS)
        def _(j):
            pltpu.sync_copy(y_hbm.at[pl.ds(base + j * SC_ROWS, SC_ROWS)], rows_v)
            pltpu.async_copy(rows_v, z_hbm.at[dest_v.at[j]], sem).wait()

    return scatter(ysp.reshape(E * cap, width), slot, rank)


def _ffn_kernel(x_ref, gate_ref, wg_ref, wu_ref, wd_ref, y_ref, acc_ref):
    x = _unpack_bf16_pairs(x_ref[...])
    for c in range(D_FF // FFN_COLS):
        cols = slice(c * FFN_COLS, (c + 1) * FFN_COLS)
        g = jnp.dot(x, wg_ref[:, cols], preferred_element_type=F32)
        u = jnp.dot(x, wu_ref[:, cols], preferred_element_type=F32)
        h = (g * jax.nn.sigmoid(g) * u).astype(BF16)
        part = jnp.dot(h, wd_ref[cols, :], preferred_element_type=F32)
        if c == 0:
            acc_ref[...] = part
        else:
            acc_ref[...] += part

    gates = gate_ref[...]
    pad = jnp.zeros((LANES - gates.shape[0], LANES), F32)
    scale = jnp.concatenate([gates, pad], axis=0).T
    for j in range(gates.shape[0]):
        rows = slice(j * LANES, (j + 1) * LANES)
        y_ref[rows, :] = _pack_bf16_pairs(acc_ref[rows, :] * scale[:, j:j + 1])


def _experts(xs, gates, w_gate, w_up, w_down):
    E, cap, packed = xs.shape
    D = 2 * packed
    tm = min(FFN_ROWS, cap)
    weight = lambda shape: pl.BlockSpec((None,) + shape, lambda e, m: (e, 0, 0))
    return pl.pallas_call(
        _ffn_kernel,
        grid=(E, cap // tm),
        in_specs=[pl.BlockSpec((None, tm, packed), lambda e, m: (e, m, 0)),
                  pl.BlockSpec((None, tm // LANES, LANES), lambda e, m: (e, m, 0)),
                  weight((D, D_FF)), weight((D, D_FF)), weight((D_FF, D))],
        out_specs=pl.BlockSpec((None, tm, packed), lambda e, m: (e, m, 0)),
        out_shape=jax.ShapeDtypeStruct((E, cap, packed), I32),
        scratch_shapes=[pltpu.VMEM((tm, D), F32)],
        compiler_params=_params("parallel", "arbitrary"),
        name="experts",
    )(xs, gates, w_gate, w_up, w_down)


def _combine_kernel(wc_ref, wt_ref, wf_ref, z_ref, span_ref, x1_ref, g_ref, b_ref, o_ref):
    w = pl.program_id(0)
    flags = wf_ref[w]

    @pl.when((flags & 2) == 2)
    def _():
        o_ref[...] = jnp.zeros_like(o_ref)

    @pl.when((flags & 1) == 1)
    def _():
        tr, tc = z_ref.shape[0], o_ref.shape[0]
        row = (lax.broadcasted_iota(I32, (tc, tr), 1) + wt_ref[w] * tr).astype(F32)
        owns = ((row >= span_ref[:, 0:1]) & (row < span_ref[:, 1:2])).astype(BF16)
        o_ref[...] += jnp.dot(owns, _unpack_bf16_pairs(z_ref[...]), preferred_element_type=F32)

    @pl.when((flags & 4) == 4)
    def _():
        o_ref[...] = _layer_norm(ALPHA * x1_ref[...] + o_ref[...], g_ref[...], b_ref[...])


def _combine(z, span_tm, x1, ln_g, ln_b, work):
    rows, packed = z.shape
    n, D = x1.shape
    length = work[0].shape[0]
    chunk = lambda w, wc, wt, wf: (wc[w], 0)
    const = lambda w, wc, wt, wf: (0, 0)
    return pl.pallas_call(
        _combine_kernel,
        grid_spec=pltpu.PrefetchScalarGridSpec(
            num_scalar_prefetch=3,
            grid=(length,),
            in_specs=[pl.BlockSpec((COMBINE_ROWS, packed), lambda w, wc, wt, wf: (wt[w], 0)),
                      pl.BlockSpec((COMBINE_CHUNK, 2), chunk),
                      pl.BlockSpec((COMBINE_CHUNK, D), chunk),
                      pl.BlockSpec((1, D), const), pl.BlockSpec((1, D), const)],
            out_specs=pl.BlockSpec((COMBINE_CHUNK, D), chunk),
        ),
        out_shape=jax.ShapeDtypeStruct((n, D), F32),
        compiler_params=_params("arbitrary"),
        name="combine",
    )(*work, z, span_tm, x1, ln_g, ln_b)


def _prepare_weights(rel_bias_table, w_in, b_in, gmlp_ln_g, gmlp_ln_b, gmlp_w_s, gmlp_b_s, w_out,
                     ln1_g, ln1_b, w_router, ln2_g, ln2_b):
    row = lambda t: t[0].reshape(1, -1).astype(F32)
    ws = gmlp_w_s[0].astype(BF16)
    ws_pairs = jnp.concatenate([ws[0::2], ws[1::2]], axis=-1)
    bs_full = jnp.repeat(gmlp_b_s[0].T, HEAD_DIM, axis=1).astype(F32)
    wr_t = w_router[0].T.astype(F32)
    wr_hi = wr_t.astype(BF16)
    wr_lo = (wr_t - wr_hi.astype(F32)).astype(BF16)
    return dict(
        rel=rel_bias_table.astype(F32), w_in=w_in[0].astype(BF16), b_in=row(b_in),
        gln_g=row(gmlp_ln_g), gln_b=row(gmlp_ln_b), ws_pairs=ws_pairs, bs_full=bs_full,
        w_out=w_out[0].astype(BF16), ln1_g=row(ln1_g), ln1_b=row(ln1_b), wr_hi=wr_hi, wr_lo=wr_lo,
        ln2_g=row(ln2_g), ln2_b=row(ln2_b))


def _trunk(x, p, expert_weights):
    B, L, D = x.shape
    n = B * L
    cap = CAPACITY_FACTOR * n // N_EXPERTS
    q, k, v, gm = _mixer_in(x, p["w_in"], p["b_in"], p["gln_g"], p["gln_b"], p["ws_pairs"], p["bs_full"])
    if expert_weights[0].dtype == BF16:
        attn, _ = _attention(q, k, v, p["rel"])
    else:
        attn, flat = _attention(q, k, v, p["rel"], [w.reshape(-1, w.shape[-1]) for w in expert_weights])
        expert_weights = tuple(f.reshape(w.shape) for f, w in zip(flat, expert_weights))
    x1, x1p, aff_t = _mixer_out(x, attn, gm, p["w_out"], p["ln1_g"], p["ln1_b"], p["wr_hi"], p["wr_lo"])
    slot, rank, span = _select(aff_t, cap)
    xs, gates = _dispatch(x1p.reshape(n, D // 2), slot, aff_t, cap)
    ys = _experts(xs, gates, *expert_weights)
    z = _to_token_order(ys, slot, rank)
    work = _work_list(span, N_EXPERTS * cap, COMBINE_CHUNK, COMBINE_ROWS)
    y = _combine(z, span.T, x1.reshape(n, D), p["ln2_g"], p["ln2_b"], work)
    return y.reshape(B, L, D), expert_weights


def kernel(x_prompt, x_sample, rel_bias_table, w_in, b_in, gmlp_ln_g, gmlp_ln_b, gmlp_w_s, gmlp_b_s, w_out,
           ln1_g, ln1_b, w_router, w_gate, w_up, w_down, ln2_g, ln2_b):
    p = _prepare_weights(rel_bias_table, w_in, b_in, gmlp_ln_g, gmlp_ln_b, gmlp_w_s, gmlp_b_s, w_out,
                         ln1_g, ln1_b, w_router, ln2_g, ln2_b)
    y_prompt, expert_weights = _trunk(x_prompt, p, (w_gate[0], w_up[0], w_down[0]))
    y_sample, _ = _trunk(x_sample, p, expert_weights)
    return (y_prompt, y_sample)
```

```python
import dataclasses
import functools
import math

import numpy as np
import jax
import jax.numpy as jnp
from jax import lax
from jax.experimental import pallas as pl
from jax.experimental.pallas import tpu as pltpu
from jax.experimental.pallas import tpu_sc as plsc

F32 = jnp.float32
BF16 = jnp.bfloat16
I32 = jnp.int32

HEAD_DIM = 64
ATTN_WIDTH = 512
GMLP_WIDTH = 512
N_HEADS = ATTN_WIDTH // HEAD_DIM
IN_WIDTH = 3 * ATTN_WIDTH + 2 * GMLP_WIDTH
GMLP_CHUNK = 128
DILATIONS = (16, 4, 1)
HALF_WINDOW = 64
N_BUCKETS = 32
REL_MAX_DISTANCE = 1024
N_EXPERTS = 16
CAPACITY_FACTOR = 2
D_FF = 2816
ALPHA = 2.0 ** 0.25
LN_EPS = 1e-5
MASK_VALUE = -1e30

LANES = 128
BF16_SUBLANES = 16
BF16_BITS = 16
VMEM_LIMIT = 56 * 1024 * 1024

Q_TILE = 128
HEAD_GROUP = 4
HG_WIDTH = HEAD_GROUP * HEAD_DIM
ATTN_UNROLL = 4
ATTN_SPAN = 2048
ROW_TILE = 1024
SCAN_TILE = 256
SLOT_BLOCK = 2048
SC_SLOT_CHUNK = 4096
SC_ROWS = 64
COMBINE_CHUNK = 512
COMBINE_ROWS = 512
FFN_ROWS = 1024
FFN_COLS = 256


def _params(*sem):
    return pltpu.CompilerParams(dimension_semantics=sem, vmem_limit_bytes=VMEM_LIMIT)


def _layer_norm(y, g, b):
    mu = jnp.mean(y, axis=-1, keepdims=True)
    yc = y - mu
    var = jnp.mean(yc * yc, axis=-1, keepdims=True)
    return yc * lax.rsqrt(var + LN_EPS) * g + b


def _gelu_tanh(x):
    return 0.5 * x * (1.0 + jnp.tanh(math.sqrt(2.0 / math.pi) * (x + 0.044715 * (x * x * x))))


def _mixer_in_kernel(x_ref, w_ref, b_ref, lng_ref, lnb_ref, ws_ref, bs_ref,
                     q_ref, k_ref, v_ref, gm_ref):
    x = x_ref[...].astype(BF16)

    def proj(lo, hi):
        return jnp.dot(x, w_ref[:, lo:hi], preferred_element_type=F32) + b_ref[:, lo:hi]

    a = ATTN_WIDTH
    for i, ref in enumerate((q_ref, k_ref, v_ref)):
        t = proj(i * a, (i + 1) * a)
        if i == 0:
            t = t * (HEAD_DIM ** -0.5)
        for s in range(a // LANES):
            ref[s] = t[:, s * LANES:(s + 1) * LANES]
    gu = _gelu_tanh(proj(3 * a, 3 * a + GMLP_WIDTH))
    gv = _gelu_tanh(proj(3 * a + GMLP_WIDTH, IN_WIDTH))
    vn = _layer_norm(gv, lng_ref[...], lnb_ref[...]).astype(BF16)

    rows = x_ref.shape[0]
    low_half = lax.broadcasted_iota(I32, (1, LANES), 1) < HEAD_DIM
    for c in range(rows // GMLP_CHUNK):
        r0 = c * GMLP_CHUNK
        for s in range(GMLP_WIDTH // LANES):
            c0 = s * LANES
            vs = vn[r0:r0 + GMLP_CHUNK, c0:c0 + LANES]
            zero = jnp.zeros_like(vs)
            rhs = jnp.concatenate([jnp.where(low_half, vs, zero), jnp.where(low_half, zero, vs)], axis=0)
            vm = jnp.dot(ws_ref[s], rhs, preferred_element_type=F32) + bs_ref[:, c0:c0 + LANES]
            gm_ref[r0:r0 + GMLP_CHUNK, c0:c0 + LANES] = (gu[r0:r0 + GMLP_CHUNK, c0:c0 + LANES] * vm).astype(BF16)


def _mixer_in(x, w_in, b_in, ln_g, ln_b, ws_pairs, bs_full):
    B, L, D = x.shape
    tm = min(ROW_TILE, L)
    row = lambda w: pl.BlockSpec((None, tm, w), lambda b, i: (b, i, 0))
    full = lambda shape: pl.BlockSpec(shape, lambda b, i: (0,) * len(shape))
    n_slab = ATTN_WIDTH // LANES
    slab = pl.BlockSpec((None, n_slab, tm, LANES), lambda b, i: (b, 0, i, 0))
    slab_shape = jax.ShapeDtypeStruct((B, n_slab, L, LANES), F32)
    return pl.pallas_call(
        _mixer_in_kernel,
        grid=(B, L // tm),
        in_specs=[row(D), full(w_in.shape), full(b_in.shape), full(ln_g.shape), full(ln_b.shape),
                  full(ws_pairs.shape), full(bs_full.shape)],
        out_specs=[slab, slab, slab, row(GMLP_WIDTH)],
        out_shape=[slab_shape] * 3 + [jax.ShapeDtypeStruct((B, L, GMLP_WIDTH), BF16)],
        compiler_params=_params("parallel", "parallel"),
        name="mixer_in",
    )(x, w_in, b_in, ln_g, ln_b, ws_pairs, bs_full)


def _attn_kernel(b16_ref, b4_ref, b1_ref, q_ref, k_ref, v_ref, out_ref, o_acc, m_acc, l_acc, *, seq_len):
    span = out_ref.shape[0]
    span_idx = pl.program_id(2)
    n_slab = HG_WIDTH // LANES
    lane = lax.broadcasted_iota(I32, (1, HG_WIDTH), 1)
    head_masks = [(lane >= h * HEAD_DIM) & (lane < (h + 1) * HEAD_DIM) for h in range(HEAD_GROUP)]

    def rows(ref, start, size, stride):
        idx = pl.ds(start, size) if stride == 1 else pl.ds(start, size, stride=stride)
        return jnp.concatenate([ref[s, idx, :] for s in range(n_slab)], axis=1)

    def put(ref, start, stride, val):
        idx = pl.ds(start, Q_TILE) if stride == 1 else pl.ds(start, Q_TILE, stride=stride)
        for s in range(n_slab):
            ref[s, idx, :] = val[:, s * LANES:(s + 1) * LANES]

    def per_head(stacked):
        out = jnp.zeros((Q_TILE, HG_WIDTH), F32)
        for h in range(HEAD_GROUP):
            out = jnp.where(head_masks[h], stacked[h * Q_TILE:(h + 1) * Q_TILE], out)
        return out

    branches = tuple(zip(DILATIONS, (b16_ref, b4_ref, b1_ref)))
    for bi, (d, bias_ref) in enumerate(branches):
        lr = seq_len // d
        tk = min(2 * Q_TILE, lr)
        tiles_total = lr // Q_TILE
        tiles_per_residue = span // d // Q_TILE
        first, final = bi == 0, bi == len(branches) - 1

        def tile(idx, carry, d=d, bias_ref=bias_ref, lr=lr, tk=tk, tiles_total=tiles_total,
                 tiles_per_residue=tiles_per_residue, first=first, final=final):
            r = idx // tiles_per_residue
            jt = idx % tiles_per_residue
            jg = span_idx * tiles_per_residue + jt
            start = jnp.clip(jg * Q_TILE - HALF_WINDOW, 0, lr - tk)
            variant = jnp.where(jg == 0, 0, jnp.where(jg == tiles_total - 1, 2, 1))
            q_row = r + d * (jt * Q_TILE)
            q = rows(q_ref, q_row, Q_TILE, d).astype(BF16)
            kw = rows(k_ref, r + d * start, tk, d).astype(BF16)
            vw = rows(v_ref, r + d * start, tk, d).astype(BF16)
            qs = jnp.concatenate([jnp.where(hm, q, jnp.zeros_like(q)) for hm in head_masks], axis=0)
            s = lax.dot_general(qs, kw, (((1,), (1,)), ((), ())), preferred_element_type=F32)
            s = s + bias_ref[variant]
            m = jnp.max(s, axis=1, keepdims=True)
            p = jnp.exp(s - m)
            l = jnp.sum(p, axis=1, keepdims=True)
            pv = jnp.dot(p.astype(BF16), vw, preferred_element_type=F32)
            o_t = per_head(pv)
            m_t, l_t = per_head(m), per_head(l)
            if not first:
                m_old = rows(m_acc, q_row, Q_TILE, d)
                m_new = jnp.maximum(m_old, m_t)
                a_old, a_t = jnp.exp(m_old - m_new), jnp.exp(m_t - m_new)
                l_t = a_old * rows(l_acc, q_row, Q_TILE, d) + a_t * l_t
                o_t = a_old * rows(o_acc, q_row, Q_TILE, d) + a_t * o_t
                m_t = m_new
            if final:
                out_ref[pl.ds(pl.multiple_of(q_row, Q_TILE), Q_TILE), :] = (o_t / l_t).astype(BF16)
            else:
                put(o_acc, q_row, d, o_t)
                put(m_acc, q_row, d, m_t)
                put(l_acc, q_row, d, l_t)
            return carry

        lax.fori_loop(0, span // Q_TILE, tile, 0, unroll=ATTN_UNROLL)


def _t5_bucket(rel):
    half = N_BUCKETS // 2
    ret = np.where(rel > 0, half, 0)
    n = np.abs(rel)
    max_exact = half // 2
    large = max_exact + (np.log(np.maximum(n, 1) / max_exact) / np.log(REL_MAX_DISTANCE / max_exact)
                         * (half - max_exact)).astype(np.int32)
    large = np.minimum(large, half - 1)
    return (ret + np.where(n < max_exact, n, large)).astype(np.int32)


def _bias_tables(rel_table, dilation, tk):
    buckets, valids = [], []
    for delta in (0, -HALF_WINDOW, Q_TILE - tk):
        off = delta + np.arange(tk)[None, :] - np.arange(Q_TILE)[:, None]
        valids.append(np.abs(off) <= HALF_WINDOW)
        buckets.append(_t5_bucket(np.clip(off, -HALF_WINDOW, HALF_WINDOW) * dilation))
    bucket = jnp.asarray(np.stack(buckets), I32)
    onehot = (bucket[..., None] == jnp.arange(N_BUCKETS, dtype=I32)).astype(F32)
    bias = jnp.einsum("vqkb,bh->hvqk", onehot, rel_table, precision=lax.Precision.HIGHEST)
    bias = jnp.where(jnp.asarray(np.stack(valids))[None], bias, MASK_VALUE)
    n_hg = N_HEADS // HEAD_GROUP
    bias = bias.reshape(n_hg, HEAD_GROUP, 3, Q_TILE, tk).transpose(0, 2, 1, 3, 4)
    return bias.reshape(n_hg, 3, HEAD_GROUP * Q_TILE, tk)


def _attn_with_casts_kernel(*refs, seq_len, n_casts):
    n_in = 6 + n_casts
    for src, dst in zip(refs[6:n_in], refs[n_in + 1:n_in + 1 + n_casts]):
        dst[...] = src[...].astype(BF16)
    _attn_kernel(*refs[:6], refs[n_in], *refs[n_in + 1 + n_casts:], seq_len=seq_len)


def _attention(q, k, v, rel_table, casts=()):
    B, _, L, _ = q.shape
    span = min(L, ATTN_SPAN)
    assert L % span == 0 and span % (max(DILATIONS) * Q_TILE) == 0, (L, span)
    n_hg = N_HEADS // HEAD_GROUP
    slabs = HG_WIDTH // LANES
    spans = L // span
    n_steps = n_hg * B * spans
    biases = [_bias_tables(rel_table, d, min(2 * Q_TILE, L // d)) for d in DILATIONS]
    once = pl.Buffered(1)
    bias_spec = lambda t: pl.BlockSpec((None,) + t.shape[1:], lambda g, b, s: (g, 0, 0, 0), pipeline_mode=once)
    seq = pl.BlockSpec((None, slabs, L, LANES), lambda g, b, s: (b, g, 0, 0))
    for t in casts:
        assert t.shape[0] % (n_steps * BF16_SUBLANES) == 0, (t.shape, n_steps)
    cast_specs = [pl.BlockSpec((t.shape[0] // n_steps, t.shape[1]), lambda g, b, s: ((g * B + b) * spans + s, 0))
                  for t in casts]
    outs = pl.pallas_call(
        functools.partial(_attn_with_casts_kernel, seq_len=L, n_casts=len(casts)),
        grid=(n_hg, B, spans),
        in_specs=[bias_spec(t) for t in biases]
                 + [pl.BlockSpec((None, slabs, span, LANES), lambda g, b, s: (b, g, s, 0)), seq, seq] + cast_specs,
        out_specs=[pl.BlockSpec((None, span, HG_WIDTH), lambda g, b, s: (b, s, g))] + cast_specs,
        out_shape=[jax.ShapeDtypeStruct((B, L, ATTN_WIDTH), BF16)]
                  + [jax.ShapeDtypeStruct(t.shape, BF16) for t in casts],
        scratch_shapes=[pltpu.VMEM((slabs, span, LANES), F32)] * 3,
        compiler_params=_params("parallel", "parallel", "arbitrary"),
        name="attention",
    )(*biases, q, k, v, *casts)
    return outs[0], tuple(outs[1:])


def _mixer_out_kernel(x_ref, attn_ref, gm_ref, wo_ref, g_ref, b_ref, wrh_ref, wrl_ref,
                      x1_ref, x1p_ref, aff_ref):
    mix = jnp.dot(attn_ref[...], wo_ref[0:ATTN_WIDTH, :], preferred_element_type=F32)
    mix = mix + jnp.dot(gm_ref[...], wo_ref[ATTN_WIDTH:, :], preferred_element_type=F32)
    x1 = _layer_norm(ALPHA * x_ref[...] + mix, g_ref[...], b_ref[...])
    x1_ref[...] = x1
    hi = x1.astype(BF16)
    x1p_ref[...] = _pack_bf16_pairs(x1)
    lo = (x1 - hi.astype(F32)).astype(BF16)
    nt = (((1,), (1,)), ((), ()))
    logits = (lax.dot_general(wrh_ref[...], hi, nt, preferred_element_type=F32)
              + lax.dot_general(wrl_ref[...], hi, nt, preferred_element_type=F32)
              + lax.dot_general(wrh_ref[...], lo, nt, preferred_element_type=F32))
    m = jnp.max(logits, axis=0, keepdims=True)
    e = jnp.exp(logits - m)
    aff_ref[...] = e / jnp.sum(e, axis=0, keepdims=True)


def _mixer_out(x, attn, gm, w_out, ln_g, ln_b, wr_hi, wr_lo):
    B, L, D = x.shape
    tm = min(ROW_TILE, L)
    per_seq = L // tm
    row = lambda w: pl.BlockSpec((None, tm, w), lambda b, i: (b, i, 0))
    full = lambda shape: pl.BlockSpec(shape, lambda b, i: (0,) * len(shape))
    return pl.pallas_call(
        _mixer_out_kernel,
        grid=(B, per_seq),
        in_specs=[row(D), row(ATTN_WIDTH), row(GMLP_WIDTH)]
                 + [full(w_out.shape), full(ln_g.shape), full(ln_b.shape), full(wr_hi.shape), full(wr_lo.shape)],
        out_specs=[row(D), row(D // 2), pl.BlockSpec((N_EXPERTS, tm), lambda b, i: (0, b * per_seq + i))],
        out_shape=[jax.ShapeDtypeStruct((B, L, D), F32), jax.ShapeDtypeStruct((B, L, D // 2), I32),
                   jax.ShapeDtypeStruct((N_EXPERTS, B * L), F32)],
        compiler_params=_params("parallel", "parallel"),
        name="mixer_out",
    )(x, attn, gm, w_out, ln_g, ln_b, wr_hi, wr_lo)


def _threshold_kernel(aff_ref, tau_ref, need_ref, *, cap):
    bits = lax.bitcast_convert_type(aff_ref[...], I32)

    def step(i, tau):
        cand = tau | jnp.left_shift(jnp.int32(1), 30 - i)
        cnt = jnp.sum((bits >= cand).astype(F32), axis=1, keepdims=True)
        return jnp.where(cnt >= cap, cand, tau)

    tau = lax.fori_loop(0, 31, step, jnp.zeros((bits.shape[0], 1), I32))
    above = jnp.sum((bits > tau).astype(F32), axis=1, keepdims=True)
    tau_ref[...] = jnp.broadcast_to(tau, tau_ref.shape)
    need_ref[...] = jnp.broadcast_to(cap - above, need_ref.shape)


def _slot_kernel(aff_ref, tau_ref, need_ref, tri_ref, low_ref, slot_ref, rank_ref, span_ref, carry_sel, carry_eq):
    @pl.when(pl.program_id(0) == 0)
    def _():
        carry_sel[...] = jnp.zeros_like(carry_sel)
        carry_eq[...] = jnp.zeros_like(carry_eq)

    tau = tau_ref[:, 0:1]
    need = need_ref[:, 0:1]
    c_sel = carry_sel[:, 0:1]
    c_eq = carry_eq[:, 0:1]
    tri = tri_ref[...]
    for s in range(aff_ref.shape[1] // SCAN_TILE):
        sl = slice(s * SCAN_TILE, (s + 1) * SCAN_TILE)
        bits = lax.bitcast_convert_type(aff_ref[:, sl], I32)
        eq = (bits == tau).astype(F32)
        eq_incl = jnp.dot(eq.astype(BF16), tri, preferred_element_type=F32)
        tie_taken = (eq_incl - eq + c_eq) < need
        sel = jnp.where((bits > tau) | ((bits == tau) & tie_taken), 1.0, 0.0)
        sel_b = sel.astype(BF16)
        incl = jnp.dot(sel_b, tri, preferred_element_type=F32)
        before = (incl - sel + c_sel).astype(I32)
        slot_ref[:, sl] = jnp.where(sel > 0.0, before, -1)
        per_token = jnp.sum(sel, axis=0, keepdims=True)
        first = jnp.sum(incl, axis=0, keepdims=True) - per_token + jnp.sum(c_sel, axis=0, keepdims=True)
        lower = jnp.dot(low_ref[...], sel_b, preferred_element_type=F32)
        rank_ref[:, sl] = (first + lower).astype(I32)
        span_ref[0:1, sl] = first
        span_ref[1:2, sl] = first + per_token
        c_sel = c_sel + incl[:, SCAN_TILE - 1:SCAN_TILE]
        c_eq = c_eq + eq_incl[:, SCAN_TILE - 1:SCAN_TILE]
    carry_sel[...] = jnp.broadcast_to(c_sel, carry_sel.shape)
    carry_eq[...] = jnp.broadcast_to(c_eq, carry_eq.shape)


def _select(aff_t, cap):
    E, n = aff_t.shape
    stat = jax.ShapeDtypeStruct((E, LANES), I32)
    tau, need = pl.pallas_call(
        functools.partial(_threshold_kernel, cap=float(cap)),
        out_shape=[stat, jax.ShapeDtypeStruct((E, LANES), F32)],
        compiler_params=pltpu.CompilerParams(vmem_limit_bytes=VMEM_LIMIT),
        name="threshold",
    )(aff_t)
    tri = jnp.asarray(np.triu(np.ones((SCAN_TILE, SCAN_TILE), np.float32)), BF16)
    low = jnp.asarray(np.tril(np.ones((E, E), np.float32), -1), BF16)
    tb = min(SLOT_BLOCK, n)
    blk = pl.BlockSpec((E, tb), lambda i: (0, i))
    const = lambda shape: pl.BlockSpec(shape, lambda i: (0, 0))
    return pl.pallas_call(
        _slot_kernel,
        grid=(n // tb,),
        in_specs=[blk, const((E, LANES)), const((E, LANES)), const((SCAN_TILE, SCAN_TILE)), const((E, E))],
        out_specs=[blk, blk, pl.BlockSpec((2, tb), lambda i: (0, i))],
        out_shape=[jax.ShapeDtypeStruct((E, n), I32)] * 2 + [jax.ShapeDtypeStruct((2, n), F32)],
        scratch_shapes=[pltpu.VMEM((E, LANES), F32), pltpu.VMEM((E, LANES), F32)],
        compiler_params=_params("arbitrary"),
        name="slots",
    )(aff_t, tau, need, tri, low)


def _work_list(span, total_rows, t_chunk, t_rows):
    n = span.shape[1]
    nch, ntl = n // t_chunk, total_rows // t_rows
    kmax = N_EXPERTS * t_chunk // t_rows + 1
    base = span[0, ::t_chunk].astype(I32)
    end = jnp.concatenate([base[1:], jnp.full((1,), total_rows, I32)])
    first = jnp.minimum(base // t_rows, ntl - 1)
    last = jnp.where(end > base, (end - 1) // t_rows, first)
    tile = first[:, None] + jnp.arange(kmax, dtype=I32)
    valid = (tile <= last[:, None]).reshape(-1)
    tile = jnp.minimum(tile, ntl - 1).reshape(-1)
    chunk = jnp.repeat(jnp.arange(nch, dtype=I32), kmax)
    length = nch + ntl
    count = jnp.sum(valid.astype(I32))
    idx = jnp.nonzero(valid, size=length, fill_value=0)[0].astype(I32)
    ar = jnp.arange(length, dtype=I32)
    live = ar < count
    idx = jnp.where(live, idx, idx[count - 1])
    c, t = chunk[idx], tile[idx]
    prev = jnp.concatenate([jnp.full((1,), -1, I32), c[:-1]])
    nxt = jnp.concatenate([c[1:], jnp.full((1,), -1, I32)])
    is_first = live & ((ar == 0) | (c != prev))
    is_last = live & ((ar == count - 1) | (c != nxt))
    return c, t, live.astype(I32) + 2 * is_first.astype(I32) + 4 * is_last.astype(I32)


def _pack_bf16_pairs(x):
    bits = lax.bitcast_convert_type(x.astype(BF16).astype(F32), I32)
    half = bits.shape[1] // 2
    return lax.shift_right_logical(bits[:, :half], BF16_BITS) | bits[:, half:]


def _unpack_bf16_pairs(w):
    left = lax.bitcast_convert_type(lax.shift_left(w, BF16_BITS), F32)
    right = lax.bitcast_convert_type(w & jnp.int32(-(1 << BF16_BITS)), F32)
    return jnp.concatenate([left, right], axis=1).astype(BF16)


def _sc_layout(n_experts, cap):
    info = plsc.get_sparse_core_info()
    workers = info.num_cores * info.num_subcores
    per_expert = workers // n_experts
    assert per_expert * n_experts == workers and cap % (per_expert * SC_ROWS) == 0, (workers, n_experts, cap)
    return info.num_cores, info.num_lanes, per_expert, cap // per_expert


def _sc_worker(n_cores, per_expert):
    wid = lax.axis_index("s") * n_cores + lax.axis_index("c")
    return wid // per_expert, wid % per_expert


def _sc_invert(slot_hbm, payload_hbms, slot_v, payload_vs, e, part, share, n, chunk, lanes, store):
    @pl.loop(0, n // chunk)
    def _(ci):
        pltpu.sync_copy(slot_hbm.at[e, pl.ds(ci * chunk, chunk)], slot_v)
        for src, dst in zip(payload_hbms, payload_vs):
            pltpu.sync_copy(src.at[e, pl.ds(ci * chunk, chunk)], dst)

        @pl.loop(0, chunk // lanes)
        def _(i):
            local = slot_v[pl.ds(i * lanes, lanes)] - part * share
            mine = (local >= 0) & (local < share)
            tok = ci * chunk + i * lanes + lax.iota(I32, lanes)
            store(local, mine, tok, [v[pl.ds(i * lanes, lanes)] for v in payload_vs])


_SC_PARAMS = dataclasses.replace(pltpu.CompilerParams(), needs_layout_passes=False)


def _dispatch(x1p, slot, aff_t, cap):
    n, width = x1p.shape
    E = slot.shape[0]
    n_cores, lanes, per_expert, share = _sc_layout(E, cap)
    chunk = min(n, SC_SLOT_CHUNK)
    mesh = plsc.VectorSubcoreMesh(core_axis_name="c", subcore_axis_name="s")

    @functools.partial(
        pl.kernel, mesh=mesh, compiler_params=_SC_PARAMS,
        out_type=[jax.ShapeDtypeStruct((E * cap, width), I32), jax.ShapeDtypeStruct((E * cap,), F32)],
        scratch_types=[pltpu.VMEM((chunk,), I32), pltpu.VMEM((chunk,), F32), pltpu.VMEM((share,), I32),
                       pltpu.VMEM((share,), F32), pltpu.VMEM((SC_ROWS, width), I32), pltpu.SemaphoreType.DMA],
    )
    def gather(x_hbm, slot_hbm, aff_hbm, out_hbm, gate_hbm, slot_v, aff_v, idx_v, gate_v, rows_v, sem):
        e, part = _sc_worker(n_cores, per_expert)

        def store(local, mine, tok, payloads):
            plsc.store_scatter(idx_v, [local], tok, mask=mine)
            plsc.store_scatter(gate_v, [local], payloads[0], mask=mine)

        _sc_invert(slot_hbm, [aff_hbm], slot_v, [aff_v], e, part, share, n, chunk, lanes, store)
        base = e * cap + part * share
        pltpu.sync_copy(gate_v, gate_hbm.at[pl.ds(base, share)])

        @pl.loop(0, share // SC_ROWS)
        def _(j):
            pltpu.async_copy(x_hbm.at[idx_v.at[pl.ds(j * SC_ROWS, SC_ROWS)]], rows_v, sem).wait()
            pltpu.sync_copy(rows_v, out_hbm.at[pl.ds(base + j * SC_ROWS, SC_ROWS)])

    xs, gates = gather(x1p, slot, aff_t)
    return xs.reshape(E, cap, width), gates.reshape(E, cap // LANES, LANES)


def _to_token_order(ysp, slot, rank):
    E, cap, width = ysp.shape
    n = slot.shape[1]
    n_cores, lanes, per_expert, share = _sc_layout(E, cap)
    chunk = min(n, SC_SLOT_CHUNK)
    mesh = plsc.VectorSubcoreMesh(core_axis_name="c", subcore_axis_name="s")

    @functools.partial(
        pl.kernel, mesh=mesh, compiler_params=_SC_PARAMS,
        out_type=jax.ShapeDtypeStruct((E * cap, width), I32),
        scratch_types=[pltpu.VMEM((chunk,), I32), pltpu.VMEM((chunk,), I32),
                       pltpu.VMEM((share // SC_ROWS, SC_ROWS), I32),
                       pltpu.VMEM((SC_ROWS, width), I32), pltpu.SemaphoreType.DMA],
    )
    def scatter(y_hbm, slot_hbm, rank_hbm, z_hbm, slot_v, rank_v, dest_v, rows_v, sem):
        e, part = _sc_worker(n_cores, per_expert)

        def store(local, mine, tok, payloads):
            plsc.store_scatter(dest_v, [local // SC_ROWS, local % SC_ROWS], payloads[0], mask=mine)

        _sc_invert(slot_hbm, [rank_hbm], slot_v, [rank_v], e, part, share, n, chunk, lanes, store)
        base = e * cap + part * share

        @pl.loop(0, share // SC_ROWS)
        def _(j):
            pltpu.sync_copy(y_hbm.at[pl.ds(base + j * SC_ROWS, SC_ROWS)], rows_v)
            pltpu.async_copy(rows_v, z_hbm.at[dest_v.at[j]], sem).wait()

    return scatter(ysp.reshape(E * cap, width), slot, rank)


def _ffn_kernel(x_ref, gate_ref, wg_ref, wu_ref, wd_ref, y_ref, acc_ref):
    x = _unpack_bf16_pairs(x_ref[...])
    for c in range(D_FF // FFN_COLS):
        cols = slice(c * FFN_COLS, (c + 1) * FFN_COLS)
        g = jnp.dot(x, wg_ref[:, cols], preferred_element_type=F32)
        u = jnp.dot(x, wu_ref[:, cols], preferred_element_type=F32)
        h = (g * jax.nn.sigmoid(g) * u).astype(BF16)
        part = jnp.dot(h, wd_ref[cols, :], preferred_element_type=F32)
        if c == 0:
            acc_ref[...] = part
        else:
            acc_ref[...] += part

    gates = gate_ref[...]
    pad = jnp.zeros((LANES - gates.shape[0], LANES), F32)
    scale = jnp.concatenate([gates, pad], axis=0).T
    for j in range(gates.shape[0]):
        rows = slice(j * LANES, (j + 1) * LANES)
        y_ref[rows, :] = _pack_bf16_pairs(acc_ref[rows, :] * scale[:, j:j + 1])


def _experts(xs, gates, w_gate, w_up, w_down):
    E, cap, packed = xs.shape
    D = 2 * packed
    tm = min(FFN_ROWS, cap)
    weight = lambda shape: pl.BlockSpec((None,) + shape, lambda e, m: (e, 0, 0))
    return pl.pallas_call(
        _ffn_kernel,
        grid=(E, cap // tm),
        in_specs=[pl.BlockSpec((None, tm, packed), lambda e, m: (e, m, 0)),
                  pl.BlockSpec((None, tm // LANES, LANES), lambda e, m: (e, m, 0)),
                  weight((D, D_FF)), weight((D, D_FF)), weight((D_FF, D))],
        out_specs=pl.BlockSpec((None, tm, packed), lambda e, m: (e, m, 0)),
        out_shape=jax.ShapeDtypeStruct((E, cap, packed), I32),
        scratch_shapes=[pltpu.VMEM((tm, D), F32)],
        compiler_params=_params("parallel", "arbitrary"),
        name="experts",
    )(xs, gates, w_gate, w_up, w_down)


def _combine_kernel(wc_ref, wt_ref, wf_ref, z_even_ref, z_odd_ref, span_ref, x1_ref, g_ref, b_ref, o_ref):
    w = pl.program_id(0)
    flags = wf_ref[w]

    @pl.when((flags & 2) == 2)
    def _():
        o_ref[...] = jnp.zeros_like(o_ref)

    def add_rows(z_ref):
        tr, tc = z_ref.shape[0], o_ref.shape[0]
        row = (lax.broadcasted_iota(I32, (tc, tr), 1) + wt_ref[w] * tr).astype(F32)
        owns = ((row >= span_ref[:, 0:1]) & (row < span_ref[:, 1:2])).astype(BF16)
        o_ref[...] += jnp.dot(owns, _unpack_bf16_pairs(z_ref[...]), preferred_element_type=F32)

    pl.when(((flags & 1) == 1) & (w % 2 == 0))(lambda: add_rows(z_even_ref))
    pl.when(((flags & 1) == 1) & (w % 2 == 1))(lambda: add_rows(z_odd_ref))

    @pl.when((flags & 4) == 4)
    def _():
        o_ref[...] = _layer_norm(ALPHA * x1_ref[...] + o_ref[...], g_ref[...], b_ref[...])


def _combine(z, span_tm, x1, ln_g, ln_b, work):
    rows, packed = z.shape
    n, D = x1.shape
    wc, wt, wf = work
    length = wc.shape[0]
    wt = jnp.concatenate([wt, wt[-1:]])
    chunk = lambda w, wc, wt, wf: (wc[w], 0)
    const = lambda w, wc, wt, wf: (0, 0)
    tile = lambda pick: pl.BlockSpec((COMBINE_ROWS, packed), lambda w, wc, wt, wf: (wt[pick(w)], 0))
    return pl.pallas_call(
        _combine_kernel,
        grid_spec=pltpu.PrefetchScalarGridSpec(
            num_scalar_prefetch=3,
            grid=(length,),
            in_specs=[tile(lambda w: 2 * ((w + 1) // 2)), tile(lambda w: 2 * (w // 2) + 1),
                      pl.BlockSpec((COMBINE_CHUNK, 2), chunk),
                      pl.BlockSpec((COMBINE_CHUNK, D), chunk),
                      pl.BlockSpec((1, D), const), pl.BlockSpec((1, D), const)],
            out_specs=pl.BlockSpec((COMBINE_CHUNK, D), chunk),
        ),
        out_shape=jax.ShapeDtypeStruct((n, D), F32),
        compiler_params=_params("arbitrary"),
        name="combine",
    )(wc, wt, wf, z, z, span_tm, x1, ln_g, ln_b)


def _prepare_weights(rel_bias_table, w_in, b_in, gmlp_ln_g, gmlp_ln_b, gmlp_w_s, gmlp_b_s, w_out,
                     ln1_g, ln1_b, w_router, ln2_g, ln2_b):
    row = lambda t: t[0].reshape(1, -1).astype(F32)
    ws = gmlp_w_s[0].astype(BF16)
    ws_pairs = jnp.concatenate([ws[0::2], ws[1::2]], axis=-1)
    bs_full = jnp.repeat(gmlp_b_s[0].T, HEAD_DIM, axis=1).astype(F32)
    wr_t = w_router[0].T.astype(F32)
    wr_hi = wr_t.astype(BF16)
    wr_lo = (wr_t - wr_hi.astype(F32)).astype(BF16)
    return dict(
        rel=rel_bias_table.astype(F32), w_in=w_in[0].astype(BF16), b_in=row(b_in),
        gln_g=row(gmlp_ln_g), gln_b=row(gmlp_ln_b), ws_pairs=ws_pairs, bs_full=bs_full,
        w_out=w_out[0].astype(BF16), ln1_g=row(ln1_g), ln1_b=row(ln1_b), wr_hi=wr_hi, wr_lo=wr_lo,
        ln2_g=row(ln2_g), ln2_b=row(ln2_b))


def _trunk(x, p, expert_weights):
    B, L, D = x.shape
    n = B * L
    cap = CAPACITY_FACTOR * n // N_EXPERTS
    q, k, v, gm = _mixer_in(x, p["w_in"], p["b_in"], p["gln_g"], p["gln_b"], p["ws_pairs"], p["bs_full"])
    if expert_weights[0].dtype == BF16:
        attn, _ = _attention(q, k, v, p["rel"])
    else:
        attn, flat = _attention(q, k, v, p["rel"], [w.reshape(-1, w.shape[-1]) for w in expert_weights])
        expert_weights = tuple(f.reshape(w.shape) for f, w in zip(flat, expert_weights))
    x1, x1p, aff_t = _mixer_out(x, attn, gm, p["w_out"], p["ln1_g"], p["ln1_b"], p["wr_hi"], p["wr_lo"])
    slot, rank, span = _select(aff_t, cap)
    xs, gates = _dispatch(x1p.reshape(n, D // 2), slot, aff_t, cap)
    ys = _experts(xs, gates, *expert_weights)
    z = _to_token_order(ys, slot, rank)
    work = _work_list(span, N_EXPERTS * cap, COMBINE_CHUNK, COMBINE_ROWS)
    y = _combine(z, span.T, x1.reshape(n, D), p["ln2_g"], p["ln2_b"], work)
    return y.reshape(B, L, D), expert_weights


def kernel(x_prompt, x_sample, rel_bias_table, w_in, b_in, gmlp_ln_g, gmlp_ln_b, gmlp_w_s, gmlp_b_s, w_out,
           ln1_g, ln1_b, w_router, w_gate, w_up, w_down, ln2_g, ln2_b):
    p = _prepare_weights(rel_bias_table, w_in, b_in, gmlp_ln_g, gmlp_ln_b, gmlp_w_s, gmlp_b_s, w_out,
                         ln1_g, ln1_b, w_router, ln2_g, ln2_b)
    y_prompt, expert_weights = _trunk(x_prompt, p, (w_gate[0], w_up[0], w_down[0]))
    y_sample, _ = _trunk(x_sample, p, expert_weights)
    return (y_prompt, y_sample)
```

```python
import dataclasses
import functools
import math

import numpy as np
import jax
import jax.numpy as jnp
from jax import lax
from jax.experimental import pallas as pl
from jax.experimental.pallas import tpu as pltpu
from jax.experimental.pallas import tpu_sc as plsc

F32 = jnp.float32
BF16 = jnp.bfloat16
I32 = jnp.int32

HEAD_DIM = 64
ATTN_WIDTH = 512
GMLP_WIDTH = 512
N_HEADS = ATTN_WIDTH // HEAD_DIM
IN_WIDTH = 3 * ATTN_WIDTH + 2 * GMLP_WIDTH
GMLP_CHUNK = 128
DILATIONS = (16, 4, 1)
HALF_WINDOW = 64
N_BUCKETS = 32
REL_MAX_DISTANCE = 1024
N_EXPERTS = 16
CAPACITY_FACTOR = 2
D_FF = 2816
ALPHA = 2.0 ** 0.25
LN_EPS = 1e-5
MASK_VALUE = -1e30

LANES = 128
BF16_SUBLANES = 16
BF16_BITS = 16
VMEM_LIMIT = 56 * 1024 * 1024

Q_TILE = 128
HEAD_GROUP = 4
HG_WIDTH = HEAD_GROUP * HEAD_DIM
ATTN_UNROLL = 8
ATTN_SPAN = 2048
ROW_TILE = 1024
SCAN_TILE = 256
SLOT_BLOCK = 2048
SC_SLOT_CHUNK = 4096
SC_ROWS = 64
COMBINE_CHUNK = 512
COMBINE_ROWS = 512
FFN_ROWS = 1024
FFN_COLS = 256


def _params(*sem):
    return pltpu.CompilerParams(dimension_semantics=sem, vmem_limit_bytes=VMEM_LIMIT)


def _layer_norm(y, g, b):
    mu = jnp.mean(y, axis=-1, keepdims=True)
    yc = y - mu
    var = jnp.mean(yc * yc, axis=-1, keepdims=True)
    return yc * lax.rsqrt(var + LN_EPS) * g + b


def _gelu_tanh(x):
    return 0.5 * x * (1.0 + jnp.tanh(math.sqrt(2.0 / math.pi) * (x + 0.044715 * (x * x * x))))


def _mixer_in_kernel(x_ref, w_ref, b_ref, lng_ref, lnb_ref, ws_ref, bs_ref,
                     q_ref, k_ref, v_ref, gm_ref):
    x = x_ref[...].astype(BF16)

    def proj(lo, hi):
        return jnp.dot(x, w_ref[:, lo:hi], preferred_element_type=F32) + b_ref[:, lo:hi]

    a = ATTN_WIDTH
    for i, ref in enumerate((q_ref, k_ref, v_ref)):
        t = proj(i * a, (i + 1) * a)
        if i == 0:
            t = t * (HEAD_DIM ** -0.5)
        for s in range(a // LANES):
            ref[s] = t[:, s * LANES:(s + 1) * LANES]
    gu = _gelu_tanh(proj(3 * a, 3 * a + GMLP_WIDTH))
    gv = _gelu_tanh(proj(3 * a + GMLP_WIDTH, IN_WIDTH))
    vn = _layer_norm(gv, lng_ref[...], lnb_ref[...]).astype(BF16)

    rows = x_ref.shape[0]
    low_half = lax.broadcasted_iota(I32, (1, LANES), 1) < HEAD_DIM
    for c in range(rows // GMLP_CHUNK):
        r0 = c * GMLP_CHUNK
        for s in range(GMLP_WIDTH // LANES):
            c0 = s * LANES
            vs = vn[r0:r0 + GMLP_CHUNK, c0:c0 + LANES]
            zero = jnp.zeros_like(vs)
            rhs = jnp.concatenate([jnp.where(low_half, vs, zero), jnp.where(low_half, zero, vs)], axis=0)
            vm = jnp.dot(ws_ref[s], rhs, preferred_element_type=F32) + bs_ref[:, c0:c0 + LANES]
            gm_ref[r0:r0 + GMLP_CHUNK, c0:c0 + LANES] = (gu[r0:r0 + GMLP_CHUNK, c0:c0 + LANES] * vm).astype(BF16)


def _mixer_in(x, w_in, b_in, ln_g, ln_b, ws_pairs, bs_full):
    B, L, D = x.shape
    tm = min(ROW_TILE, L)
    row = lambda w: pl.BlockSpec((None, tm, w), lambda b, i: (b, i, 0))
    full = lambda shape: pl.BlockSpec(shape, lambda b, i: (0,) * len(shape))
    n_slab = ATTN_WIDTH // LANES
    slab = pl.BlockSpec((None, n_slab, tm, LANES), lambda b, i: (b, 0, i, 0))
    slab_shape = jax.ShapeDtypeStruct((B, n_slab, L, LANES), F32)
    return pl.pallas_call(
        _mixer_in_kernel,
        grid=(B, L // tm),
        in_specs=[row(D), full(w_in.shape), full(b_in.shape), full(ln_g.shape), full(ln_b.shape),
                  full(ws_pairs.shape), full(bs_full.shape)],
        out_specs=[slab, slab, slab, row(GMLP_WIDTH)],
        out_shape=[slab_shape] * 3 + [jax.ShapeDtypeStruct((B, L, GMLP_WIDTH), BF16)],
        compiler_params=_params("parallel", "parallel"),
        name="mixer_in",
    )(x, w_in, b_in, ln_g, ln_b, ws_pairs, bs_full)


def _attn_kernel(b16_ref, b4_ref, b1_ref, q_ref, k_ref, v_ref, out_ref, o_acc, m_acc, l_acc, *, seq_len):
    span = out_ref.shape[0]
    span_idx = pl.program_id(2)
    n_slab = HG_WIDTH // LANES
    lane = lax.broadcasted_iota(I32, (1, HG_WIDTH), 1)
    head_masks = [(lane >= h * HEAD_DIM) & (lane < (h + 1) * HEAD_DIM) for h in range(HEAD_GROUP)]

    def rows(ref, start, size, stride):
        idx = pl.ds(start, size) if stride == 1 else pl.ds(start, size, stride=stride)
        return jnp.concatenate([ref[s, idx, :] for s in range(n_slab)], axis=1)

    def put(ref, start, stride, val):
        idx = pl.ds(start, Q_TILE) if stride == 1 else pl.ds(start, Q_TILE, stride=stride)
        for s in range(n_slab):
            ref[s, idx, :] = val[:, s * LANES:(s + 1) * LANES]

    def per_head(stacked):
        out = jnp.zeros((Q_TILE, HG_WIDTH), F32)
        for h in range(HEAD_GROUP):
            out = jnp.where(head_masks[h], stacked[h * Q_TILE:(h + 1) * Q_TILE], out)
        return out

    branches = tuple(zip(DILATIONS, (b16_ref, b4_ref, b1_ref)))
    for bi, (d, bias_ref) in enumerate(branches):
        lr = seq_len // d
        tk = min(2 * Q_TILE, lr)
        tiles_total = lr // Q_TILE
        tiles_per_residue = span // d // Q_TILE
        first, final = bi == 0, bi == len(branches) - 1

        def tile(idx, carry, d=d, bias_ref=bias_ref, lr=lr, tk=tk, tiles_total=tiles_total,
                 tiles_per_residue=tiles_per_residue, first=first, final=final):
            r = idx // tiles_per_residue
            jt = idx % tiles_per_residue
            jg = span_idx * tiles_per_residue + jt
            start = jnp.clip(jg * Q_TILE - HALF_WINDOW, 0, lr - tk)
            variant = jnp.where(jg == 0, 0, jnp.where(jg == tiles_total - 1, 2, 1))
            q_row = r + d * (jt * Q_TILE)
            q = rows(q_ref, q_row, Q_TILE, d).astype(BF16)
            kw = rows(k_ref, r + d * start, tk, d).astype(BF16)
            vw = rows(v_ref, r + d * start, tk, d).astype(BF16)
            qs = jnp.concatenate([jnp.where(hm, q, jnp.zeros_like(q)) for hm in head_masks], axis=0)
            s = lax.dot_general(qs, kw, (((1,), (1,)), ((), ())), preferred_element_type=F32)
            s = s + bias_ref[variant]
            m = jnp.max(s, axis=1, keepdims=True)
            p = jnp.exp(s - m)
            l = jnp.sum(p, axis=1, keepdims=True)
            pv = jnp.dot(p.astype(BF16), vw, preferred_element_type=F32)
            o_t = per_head(pv)
            m_t, l_t = per_head(m), per_head(l)
            if not first:
                m_old = rows(m_acc, q_row, Q_TILE, d)
                m_new = jnp.maximum(m_old, m_t)
                a_old, a_t = jnp.exp(m_old - m_new), jnp.exp(m_t - m_new)
                l_t = a_old * rows(l_acc, q_row, Q_TILE, d) + a_t * l_t
                o_t = a_old * rows(o_acc, q_row, Q_TILE, d) + a_t * o_t
                m_t = m_new
            if final:
                out_ref[pl.ds(pl.multiple_of(q_row, Q_TILE), Q_TILE), :] = (o_t / l_t).astype(BF16)
            else:
                put(o_acc, q_row, d, o_t)
                put(m_acc, q_row, d, m_t)
                put(l_acc, q_row, d, l_t)
            return carry

        lax.fori_loop(0, span // Q_TILE, tile, 0, unroll=ATTN_UNROLL)


def _t5_bucket(rel):
    half = N_BUCKETS // 2
    ret = np.where(rel > 0, half, 0)
    n = np.abs(rel)
    max_exact = half // 2
    large = max_exact + (np.log(np.maximum(n, 1) / max_exact) / np.log(REL_MAX_DISTANCE / max_exact)
                         * (half - max_exact)).astype(np.int32)
    large = np.minimum(large, half - 1)
    return (ret + np.where(n < max_exact, n, large)).astype(np.int32)


def _bias_tables(rel_table, dilation, tk):
    buckets, valids = [], []
    for delta in (0, -HALF_WINDOW, Q_TILE - tk):
        off = delta + np.arange(tk)[None, :] - np.arange(Q_TILE)[:, None]
        valids.append(np.abs(off) <= HALF_WINDOW)
        buckets.append(_t5_bucket(np.clip(off, -HALF_WINDOW, HALF_WINDOW) * dilation))
    bucket = jnp.asarray(np.stack(buckets), I32)
    onehot = (bucket[..., None] == jnp.arange(N_BUCKETS, dtype=I32)).astype(F32)
    bias = jnp.einsum("vqkb,bh->hvqk", onehot, rel_table, precision=lax.Precision.HIGHEST)
    bias = jnp.where(jnp.asarray(np.stack(valids))[None], bias, MASK_VALUE)
    n_hg = N_HEADS // HEAD_GROUP
    bias = bias.reshape(n_hg, HEAD_GROUP, 3, Q_TILE, tk).transpose(0, 2, 1, 3, 4)
    return bias.reshape(n_hg, 3, HEAD_GROUP * Q_TILE, tk)


def _attn_with_casts_kernel(*refs, seq_len, n_casts):
    n_in = 6 + n_casts
    for src, dst in zip(refs[6:n_in], refs[n_in + 1:n_in + 1 + n_casts]):
        dst[...] = src[...].astype(BF16)
    _attn_kernel(*refs[:6], refs[n_in], *refs[n_in + 1 + n_casts:], seq_len=seq_len)


def _attention(q, k, v, rel_table, casts=()):
    B, _, L, _ = q.shape
    span = min(L, ATTN_SPAN)
    assert L % span == 0 and span % (max(DILATIONS) * Q_TILE) == 0, (L, span)
    n_hg = N_HEADS // HEAD_GROUP
    slabs = HG_WIDTH // LANES
    spans = L // span
    n_steps = n_hg * B * spans
    biases = [_bias_tables(rel_table, d, min(2 * Q_TILE, L // d)) for d in DILATIONS]
    once = pl.Buffered(1)
    bias_spec = lambda t: pl.BlockSpec((None,) + t.shape[1:], lambda g, b, s: (g, 0, 0, 0), pipeline_mode=once)
    seq = pl.BlockSpec((None, slabs, L, LANES), lambda g, b, s: (b, g, 0, 0))
    for t in casts:
        assert t.shape[0] % (n_steps * BF16_SUBLANES) == 0, (t.shape, n_steps)
    cast_specs = [pl.BlockSpec((t.shape[0] // n_steps, t.shape[1]), lambda g, b, s: ((g * B + b) * spans + s, 0))
                  for t in casts]
    outs = pl.pallas_call(
        functools.partial(_attn_with_casts_kernel, seq_len=L, n_casts=len(casts)),
        grid=(n_hg, B, spans),
        in_specs=[bias_spec(t) for t in biases]
                 + [pl.BlockSpec((None, slabs, span, LANES), lambda g, b, s: (b, g, s, 0)), seq, seq] + cast_specs,
        out_specs=[pl.BlockSpec((None, span, HG_WIDTH), lambda g, b, s: (b, s, g))] + cast_specs,
        out_shape=[jax.ShapeDtypeStruct((B, L, ATTN_WIDTH), BF16)]
                  + [jax.ShapeDtypeStruct(t.shape, BF16) for t in casts],
        scratch_shapes=[pltpu.VMEM((slabs, span, LANES), F32)] * 3,
        compiler_params=_params("parallel", "parallel", "arbitrary"),
        name="attention",
    )(*biases, q, k, v, *casts)
    return outs[0], tuple(outs[1:])


def _mixer_out_kernel(x_ref, attn_ref, gm_ref, wo_ref, g_ref, b_ref, wrh_ref, wrl_ref,
                      x1_ref, x1p_ref, aff_ref):
    mix = jnp.dot(attn_ref[...], wo_ref[0:ATTN_WIDTH, :], preferred_element_type=F32)
    mix = mix + jnp.dot(gm_ref[...], wo_ref[ATTN_WIDTH:, :], preferred_element_type=F32)
    x1 = _layer_norm(ALPHA * x_ref[...] + mix, g_ref[...], b_ref[...])
    x1_ref[...] = x1
    hi = x1.astype(BF16)
    x1p_ref[...] = _pack_bf16_pairs(x1)
    lo = (x1 - hi.astype(F32)).astype(BF16)
    nt = (((1,), (1,)), ((), ()))
    logits = (lax.dot_general(wrh_ref[...], hi, nt, preferred_element_type=F32)
              + lax.dot_general(wrl_ref[...], hi, nt, preferred_element_type=F32)
              + lax.dot_general(wrh_ref[...], lo, nt, preferred_element_type=F32))
    m = jnp.max(logits, axis=0, keepdims=True)
    e = jnp.exp(logits - m)
    aff_ref[...] = e / jnp.sum(e, axis=0, keepdims=True)


def _mixer_out(x, attn, gm, w_out, ln_g, ln_b, wr_hi, wr_lo):
    B, L, D = x.shape
    tm = min(ROW_TILE, L)
    per_seq = L // tm
    row = lambda w: pl.BlockSpec((None, tm, w), lambda b, i: (b, i, 0))
    full = lambda shape: pl.BlockSpec(shape, lambda b, i: (0,) * len(shape))
    return pl.pallas_call(
        _mixer_out_kernel,
        grid=(B, per_seq),
        in_specs=[row(D), row(ATTN_WIDTH), row(GMLP_WIDTH)]
                 + [full(w_out.shape), full(ln_g.shape), full(ln_b.shape), full(wr_hi.shape), full(wr_lo.shape)],
        out_specs=[row(D), row(D // 2), pl.BlockSpec((N_EXPERTS, tm), lambda b, i: (0, b * per_seq + i))],
        out_shape=[jax.ShapeDtypeStruct((B, L, D), F32), jax.ShapeDtypeStruct((B, L, D // 2), I32),
                   jax.ShapeDtypeStruct((N_EXPERTS, B * L), F32)],
        compiler_params=_params("parallel", "parallel"),
        name="mixer_out",
    )(x, attn, gm, w_out, ln_g, ln_b, wr_hi, wr_lo)


def _threshold_kernel(aff_ref, tau_ref, need_ref, *, cap):
    bits = lax.bitcast_convert_type(aff_ref[...], I32)

    def step(i, tau):
        cand = tau | jnp.left_shift(jnp.int32(1), 30 - i)
        cnt = jnp.sum((bits >= cand).astype(F32), axis=1, keepdims=True)
        return jnp.where(cnt >= cap, cand, tau)

    tau = lax.fori_loop(0, 31, step, jnp.zeros((bits.shape[0], 1), I32))
    above = jnp.sum((bits > tau).astype(F32), axis=1, keepdims=True)
    tau_ref[...] = jnp.broadcast_to(tau, tau_ref.shape)
    need_ref[...] = jnp.broadcast_to(cap - above, need_ref.shape)


def _slot_kernel(aff_ref, tau_ref, need_ref, tri_ref, low_ref, slot_ref, rank_ref, span_ref, carry_sel, carry_eq):
    @pl.when(pl.program_id(0) == 0)
    def _():
        carry_sel[...] = jnp.zeros_like(carry_sel)
        carry_eq[...] = jnp.zeros_like(carry_eq)

    tau = tau_ref[:, 0:1]
    need = need_ref[:, 0:1]
    c_sel = carry_sel[:, 0:1]
    c_eq = carry_eq[:, 0:1]
    tri = tri_ref[...]
    for s in range(aff_ref.shape[1] // SCAN_TILE):
        sl = slice(s * SCAN_TILE, (s + 1) * SCAN_TILE)
        bits = lax.bitcast_convert_type(aff_ref[:, sl], I32)
        eq = (bits == tau).astype(F32)
        eq_incl = jnp.dot(eq.astype(BF16), tri, preferred_element_type=F32)
        tie_taken = (eq_incl - eq + c_eq) < need
        sel = jnp.where((bits > tau) | ((bits == tau) & tie_taken), 1.0, 0.0)
        sel_b = sel.astype(BF16)
        incl = jnp.dot(sel_b, tri, preferred_element_type=F32)
        before = (incl - sel + c_sel).astype(I32)
        slot_ref[:, sl] = jnp.where(sel > 0.0, before, -1)
        per_token = jnp.sum(sel, axis=0, keepdims=True)
        first = jnp.sum(incl, axis=0, keepdims=True) - per_token + jnp.sum(c_sel, axis=0, keepdims=True)
        lower = jnp.dot(low_ref[...], sel_b, preferred_element_type=F32)
        rank_ref[:, sl] = (first + lower).astype(I32)
        span_ref[0:1, sl] = first
        span_ref[1:2, sl] = first + per_token
        c_sel = c_sel + incl[:, SCAN_TILE - 1:SCAN_TILE]
        c_eq = c_eq + eq_incl[:, SCAN_TILE - 1:SCAN_TILE]
    carry_sel[...] = jnp.broadcast_to(c_sel, carry_sel.shape)
    carry_eq[...] = jnp.broadcast_to(c_eq, carry_eq.shape)


def _select(aff_t, cap):
    E, n = aff_t.shape
    stat = jax.ShapeDtypeStruct((E, LANES), I32)
    tau, need = pl.pallas_call(
        functools.partial(_threshold_kernel, cap=float(cap)),
        out_shape=[stat, jax.ShapeDtypeStruct((E, LANES), F32)],
        compiler_params=pltpu.CompilerParams(vmem_limit_bytes=VMEM_LIMIT),
        name="threshold",
    )(aff_t)
    tri = jnp.asarray(np.triu(np.ones((SCAN_TILE, SCAN_TILE), np.float32)), BF16)
    low = jnp.asarray(np.tril(np.ones((E, E), np.float32), -1), BF16)
    tb = min(SLOT_BLOCK, n)
    blk = pl.BlockSpec((E, tb), lambda i: (0, i))
    const = lambda shape: pl.BlockSpec(shape, lambda i: (0, 0))
    return pl.pallas_call(
        _slot_kernel,
        grid=(n // tb,),
        in_specs=[blk, const((E, LANES)), const((E, LANES)), const((SCAN_TILE, SCAN_TILE)), const((E, E))],
        out_specs=[blk, blk, pl.BlockSpec((2, tb), lambda i: (0, i))],
        out_shape=[jax.ShapeDtypeStruct((E, n), I32)] * 2 + [jax.ShapeDtypeStruct((2, n), F32)],
        scratch_shapes=[pltpu.VMEM((E, LANES), F32), pltpu.VMEM((E, LANES), F32)],
        compiler_params=_params("arbitrary"),
        name="slots",
    )(aff_t, tau, need, tri, low)


def _work_list(span, total_rows, t_chunk, t_rows):
    n = span.shape[1]
    nch, ntl = n // t_chunk, total_rows // t_rows
    kmax = N_EXPERTS * t_chunk // t_rows + 1
    base = span[0, ::t_chunk].astype(I32)
    end = jnp.concatenate([base[1:], jnp.full((1,), total_rows, I32)])
    first = jnp.minimum(base // t_rows, ntl - 1)
    last = jnp.where(end > base, (end - 1) // t_rows, first)
    tile = first[:, None] + jnp.arange(kmax, dtype=I32)
    valid = (tile <= last[:, None]).reshape(-1)
    tile = jnp.minimum(tile, ntl - 1).reshape(-1)
    chunk = jnp.repeat(jnp.arange(nch, dtype=I32), kmax)
    length = nch + ntl
    count = jnp.sum(valid.astype(I32))
    idx = jnp.nonzero(valid, size=length, fill_value=0)[0].astype(I32)
    ar = jnp.arange(length, dtype=I32)
    live = ar < count
    idx = jnp.where(live, idx, idx[count - 1])
    c, t = chunk[idx], tile[idx]
    prev = jnp.concatenate([jnp.full((1,), -1, I32), c[:-1]])
    nxt = jnp.concatenate([c[1:], jnp.full((1,), -1, I32)])
    is_first = live & ((ar == 0) | (c != prev))
    is_last = live & ((ar == count - 1) | (c != nxt))
    return c, t, live.astype(I32) + 2 * is_first.astype(I32) + 4 * is_last.astype(I32)


def _pack_bf16_pairs(x):
    bits = lax.bitcast_convert_type(x.astype(BF16).astype(F32), I32)
    half = bits.shape[1] // 2
    return lax.shift_right_logical(bits[:, :half], BF16_BITS) | bits[:, half:]


def _unpack_bf16_pairs(w):
    left = lax.bitcast_convert_type(lax.shift_left(w, BF16_BITS), F32)
    right = lax.bitcast_convert_type(w & jnp.int32(-(1 << BF16_BITS)), F32)
    return jnp.concatenate([left, right], axis=1).astype(BF16)


def _sc_layout(n_experts, cap):
    info = plsc.get_sparse_core_info()
    workers = info.num_cores * info.num_subcores
    per_expert = workers // n_experts
    assert per_expert * n_experts == workers and cap % (per_expert * SC_ROWS) == 0, (workers, n_experts, cap)
    return info.num_cores, info.num_lanes, per_expert, cap // per_expert


def _sc_worker(n_cores, per_expert):
    wid = lax.axis_index("s") * n_cores + lax.axis_index("c")
    return wid // per_expert, wid % per_expert


def _sc_invert(slot_hbm, payload_hbms, slot_v, payload_vs, e, part, share, n, chunk, lanes, store):
    @pl.loop(0, n // chunk)
    def _(ci):
        pltpu.sync_copy(slot_hbm.at[e, pl.ds(ci * chunk, chunk)], slot_v)
        for src, dst in zip(payload_hbms, payload_vs):
            pltpu.sync_copy(src.at[e, pl.ds(ci * chunk, chunk)], dst)

        @pl.loop(0, chunk // lanes)
        def _(i):
            local = slot_v[pl.ds(i * lanes, lanes)] - part * share
            mine = (local >= 0) & (local < share)
            tok = ci * chunk + i * lanes + lax.iota(I32, lanes)
            store(local, mine, tok, [v[pl.ds(i * lanes, lanes)] for v in payload_vs])


_SC_PARAMS = dataclasses.replace(pltpu.CompilerParams(), needs_layout_passes=False)


def _dispatch(x1p, slot, aff_t, cap):
    n, width = x1p.shape
    E = slot.shape[0]
    n_cores, lanes, per_expert, share = _sc_layout(E, cap)
    chunk = min(n, SC_SLOT_CHUNK)
    mesh = plsc.VectorSubcoreMesh(core_axis_name="c", subcore_axis_name="s")

    @functools.partial(
        pl.kernel, mesh=mesh, compiler_params=_SC_PARAMS,
        out_type=[jax.ShapeDtypeStruct((E * cap, width), I32), jax.ShapeDtypeStruct((E * cap,), F32)],
        scratch_types=[pltpu.VMEM((chunk,), I32), pltpu.VMEM((chunk,), F32), pltpu.VMEM((share,), I32),
                       pltpu.VMEM((share,), F32), pltpu.VMEM((SC_ROWS, width), I32), pltpu.SemaphoreType.DMA],
    )
    def gather(x_hbm, slot_hbm, aff_hbm, out_hbm, gate_hbm, slot_v, aff_v, idx_v, gate_v, rows_v, sem):
        e, part = _sc_worker(n_cores, per_expert)

        def store(local, mine, tok, payloads):
            plsc.store_scatter(idx_v, [local], tok, mask=mine)
            plsc.store_scatter(gate_v, [local], payloads[0], mask=mine)

        _sc_invert(slot_hbm, [aff_hbm], slot_v, [aff_v], e, part, share, n, chunk, lanes, store)
        base = e * cap + part * share
        pltpu.sync_copy(gate_v, gate_hbm.at[pl.ds(base, share)])

        @pl.loop(0, share // SC_ROWS)
        def _(j):
            pltpu.async_copy(x_hbm.at[idx_v.at[pl.ds(j * SC_ROWS, SC_ROWS)]], rows_v, sem).wait()
            pltpu.sync_copy(rows_v, out_hbm.at[pl.ds(base + j * SC_ROWS, SC_ROWS)])

    xs, gates = gather(x1p, slot, aff_t)
    return xs.reshape(E, cap, width), gates.reshape(E, cap // LANES, LANES)


def _to_token_order(ysp, slot, rank):
    E, cap, width = ysp.shape
    n = slot.shape[1]
    n_cores, lanes, per_expert, share = _sc_layout(E, cap)
    chunk = min(n, SC_SLOT_CHUNK)
    mesh = plsc.VectorSubcoreMesh(core_axis_name="c", subcore_axis_name="s")

    @functools.partial(
        pl.kernel, mesh=mesh, compiler_params=_SC_PARAMS,
        out_type=jax.ShapeDtypeStruct((E * cap, width), I32),
        scratch_types=[pltpu.VMEM((chunk,), I32), pltpu.VMEM((chunk,), I32),
                       pltpu.VMEM((share // SC_ROWS, SC_ROWS), I32),
                       pltpu.VMEM((SC_ROWS, width), I32), pltpu.SemaphoreType.DMA],
    )
    def scatter(y_hbm, slot_hbm, rank_hbm, z_hbm, slot_v, rank_v, dest_v, rows_v, sem):
        e, part = _sc_worker(n_cores, per_expert)

        def store(local, mine, tok, payloads):
            plsc.store_scatter(dest_v, [local // SC_ROWS, local % SC_ROWS], payloads[0], mask=mine)

        _sc_invert(slot_hbm, [rank_hbm], slot_v, [rank_v], e, part, share, n, chunk, lanes, store)
        base = e * cap + part * share

        @pl.loop(0, share // SC_ROWS)
        def _(j):
            pltpu.sync_copy(y_hbm.at[pl.ds(base + j * SC_ROWS, SC_ROWS)], rows_v)
            pltpu.async_copy(rows_v, z_hbm.at[dest_v.at[j]], sem).wait()

    return scatter(ysp.reshape(E * cap, width), slot, rank)


def _ffn_kernel(x_ref, gate_ref, wg_ref, wu_ref, wd_ref, y_ref, acc_ref):
    x = _unpack_bf16_pairs(x_ref[...])
    for c in range(D_FF // FFN_COLS):
        cols = slice(c * FFN_COLS, (c + 1) * FFN_COLS)
        g = jnp.dot(x, wg_ref[:, cols], preferred_element_type=F32)
        u = jnp.dot(x, wu_ref[:, cols], preferred_element_type=F32)
        h = (g * jax.nn.sigmoid(g) * u).astype(BF16)
        part = jnp.dot(h, wd_ref[cols, :], preferred_element_type=F32)
        if c == 0:
            acc_ref[...] = part
        else:
            acc_ref[...] += part

    gates = gate_ref[...]
    pad = jnp.zeros((LANES - gates.shape[0], LANES), F32)
    scale = jnp.concatenate([gates, pad], axis=0).T
    for j in range(gates.shape[0]):
        rows = slice(j * LANES, (j + 1) * LANES)
        y_ref[rows, :] = _pack_bf16_pairs(acc_ref[rows, :] * scale[:, j:j + 1])


def _experts(xs, gates, w_gate, w_up, w_down):
    E, cap, packed = xs.shape
    D = 2 * packed
    tm = min(FFN_ROWS, cap)
    weight = lambda shape: pl.BlockSpec((None,) + shape, lambda e, m: (e, 0, 0))
    return pl.pallas_call(
        _ffn_kernel,
        grid=(E, cap // tm),
        in_specs=[pl.BlockSpec((None, tm, packed), lambda e, m: (e, m, 0)),
                  pl.BlockSpec((None, tm // LANES, LANES), lambda e, m: (e, m, 0)),
                  weight((D, D_FF)), weight((D, D_FF)), weight((D_FF, D))],
        out_specs=pl.BlockSpec((None, tm, packed), lambda e, m: (e, m, 0)),
        out_shape=jax.ShapeDtypeStruct((E, cap, packed), I32),
        scratch_shapes=[pltpu.VMEM((tm, D), F32)],
        compiler_params=_params("parallel", "arbitrary"),
        name="experts",
    )(xs, gates, w_gate, w_up, w_down)


def _combine_kernel(wc_ref, wt_ref, wf_ref, z_ref, span_ref, x1_ref, g_ref, b_ref, o_ref):
    w = pl.program_id(0)
    flags = wf_ref[w]

    @pl.when((flags & 2) == 2)
    def _():
        o_ref[...] = jnp.zeros_like(o_ref)

    @pl.when((flags & 1) == 1)
    def _():
        tr, tc = z_ref.shape[0], o_ref.shape[0]
        row = (lax.broadcasted_iota(I32, (tc, tr), 1) + wt_ref[w] * tr).astype(F32)
        owns = ((row >= span_ref[:, 0:1]) & (row < span_ref[:, 1:2])).astype(BF16)
        o_ref[...] += jnp.dot(owns, _unpack_bf16_pairs(z_ref[...]), preferred_element_type=F32)

    @pl.when((flags & 4) == 4)
    def _():
        o_ref[...] = _layer_norm(ALPHA * x1_ref[...] + o_ref[...], g_ref[...], b_ref[...])


def _combine(z, span_tm, x1, ln_g, ln_b, work):
    rows, packed = z.shape
    n, D = x1.shape
    length = work[0].shape[0]
    chunk = lambda w, wc, wt, wf: (wc[w], 0)
    const = lambda w, wc, wt, wf: (0, 0)
    return pl.pallas_call(
        _combine_kernel,
        grid_spec=pltpu.PrefetchScalarGridSpec(
            num_scalar_prefetch=3,
            grid=(length,),
            in_specs=[pl.BlockSpec((COMBINE_ROWS, packed), lambda w, wc, wt, wf: (wt[w], 0)),
                      pl.BlockSpec((COMBINE_CHUNK, 2), chunk),
                      pl.BlockSpec((COMBINE_CHUNK, D), chunk),
                      pl.BlockSpec((1, D), const), pl.BlockSpec((1, D), const)],
            out_specs=pl.BlockSpec((COMBINE_CHUNK, D), chunk),
        ),
        out_shape=jax.ShapeDtypeStruct((n, D), F32),
        compiler_params=_params("arbitrary"),
        name="combine",
    )(*work, z, span_tm, x1, ln_g, ln_b)


def _prepare_weights(rel_bias_table, w_in, b_in, gmlp_ln_g, gmlp_ln_b, gmlp_w_s, gmlp_b_s, w_out,
                     ln1_g, ln1_b, w_router, ln2_g, ln2_b):
    row = lambda t: t[0].reshape(1, -1).astype(F32)
    ws = gmlp_w_s[0].astype(BF16)
    ws_pairs = jnp.concatenate([ws[0::2], ws[1::2]], axis=-1)
    bs_full = jnp.repeat(gmlp_b_s[0].T, HEAD_DIM, axis=1).astype(F32)
    wr_t = w_router[0].T.astype(F32)
    wr_hi = wr_t.astype(BF16)
    wr_lo = (wr_t - wr_hi.astype(F32)).astype(BF16)
    return dict(
        rel=rel_bias_table.astype(F32), w_in=w_in[0].astype(BF16), b_in=row(b_in),
        gln_g=row(gmlp_ln_g), gln_b=row(gmlp_ln_b), ws_pairs=ws_pairs, bs_full=bs_full,
        w_out=w_out[0].astype(BF16), ln1_g=row(ln1_g), ln1_b=row(ln1_b), wr_hi=wr_hi, wr_lo=wr_lo,
        ln2_g=row(ln2_g), ln2_b=row(ln2_b))


def _trunk(x, p, expert_weights):
    B, L, D = x.shape
    n = B * L
    cap = CAPACITY_FACTOR * n // N_EXPERTS
    q, k, v, gm = _mixer_in(x, p["w_in"], p["b_in"], p["gln_g"], p["gln_b"], p["ws_pairs"], p["bs_full"])
    if expert_weights[0].dtype == BF16:
        attn, _ = _attention(q, k, v, p["rel"])
    else:
        attn, flat = _attention(q, k, v, p["rel"], [w.reshape(-1, w.shape[-1]) for w in expert_weights])
        expert_weights = tuple(f.reshape(w.shape) for f, w in zip(flat, expert_weights))
    x1, x1p, aff_t = _mixer_out(x, attn, gm, p["w_out"], p["ln1_g"], p["ln1_b"], p["wr_hi"], p["wr_lo"])
    slot, rank, span = _select(aff_t, cap)
    xs, gates = _dispatch(x1p.reshape(n, D // 2), slot, aff_t, cap)
    ys = _experts(xs, gates, *expert_weights)
    z = _to_token_order(ys, slot, rank)
    work = _work_list(span, N_EXPERTS * cap, COMBINE_CHUNK, COMBINE_ROWS)
    y = _combine(z, span.T, x1.reshape(n, D), p["ln2_g"], p["ln2_b"], work)
    return y.reshape(B, L, D), expert_weights


def kernel(x_prompt, x_sample, rel_bias_table, w_in, b_in, gmlp_ln_g, gmlp_ln_b, gmlp_w_s, gmlp_b_s, w_out,
           ln1_g, ln1_b, w_router, w_gate, w_up, w_down, ln2_g, ln2_b):
    p = _prepare_weights(rel_bias_table, w_in, b_in, gmlp_ln_g, gmlp_ln_b, gmlp_w_s, gmlp_b_s, w_out,
                         ln1_g, ln1_b, w_router, ln2_g, ln2_b)
    y_prompt, expert_weights = _trunk(x_prompt, p, (w_gate[0], w_up[0], w_down[0]))
    y_sample, _ = _trunk(x_sample, p, expert_weights)
    return (y_prompt, y_sample)
```

```python
import dataclasses
import functools
import math

import numpy as np
import jax
import jax.numpy as jnp
from jax import lax
from jax.experimental import pallas as pl
from jax.experimental.pallas import tpu as pltpu
from jax.experimental.pallas import tpu_sc as plsc

F32 = jnp.float32
BF16 = jnp.bfloat16
I32 = jnp.int32

HEAD_DIM = 64
ATTN_WIDTH = 512
GMLP_WIDTH = 512
N_HEADS = ATTN_WIDTH // HEAD_DIM
IN_WIDTH = 3 * ATTN_WIDTH + 2 * GMLP_WIDTH
GMLP_CHUNK = 128
DILATIONS = (16, 4, 1)
HALF_WINDOW = 64
N_BUCKETS = 32
REL_MAX_DISTANCE = 1024
N_EXPERTS = 16
CAPACITY_FACTOR = 2
D_FF = 2816
ALPHA = 2.0 ** 0.25
LN_EPS = 1e-5
MASK_VALUE = -1e30

LANES = 128
BF16_SUBLANES = 16
BF16_BITS = 16
VMEM_LIMIT = 56 * 1024 * 1024

Q_TILE = 128
HEAD_GROUP = 4
HG_WIDTH = HEAD_GROUP * HEAD_DIM
HEADS_PER_PASS = 2
ATTN_UNROLL = 8
ATTN_SPAN = 2048
ROW_TILE = 1024
SCAN_TILE = 256
SLOT_BLOCK = 2048
SC_SLOT_CHUNK = 4096
SC_ROWS = 64
COMBINE_CHUNK = 512
COMBINE_ROWS = 512
FFN_ROWS = 1024
FFN_COLS = 256


def _params(*sem):
    return pltpu.CompilerParams(dimension_semantics=sem, vmem_limit_bytes=VMEM_LIMIT)


def _layer_norm(y, g, b):
    mu = jnp.mean(y, axis=-1, keepdims=True)
    yc = y - mu
    var = jnp.mean(yc * yc, axis=-1, keepdims=True)
    return yc * lax.rsqrt(var + LN_EPS) * g + b


def _gelu_tanh(x):
    return 0.5 * x * (1.0 + jnp.tanh(math.sqrt(2.0 / math.pi) * (x + 0.044715 * (x * x * x))))


def _mixer_in_kernel(x_ref, w_ref, b_ref, lng_ref, lnb_ref, ws_ref, bs_ref,
                     q_ref, k_ref, v_ref, gm_ref):
    x = x_ref[...].astype(BF16)

    def proj(lo, hi):
        return jnp.dot(x, w_ref[:, lo:hi], preferred_element_type=F32) + b_ref[:, lo:hi]

    a = ATTN_WIDTH
    for i, ref in enumerate((q_ref, k_ref, v_ref)):
        t = proj(i * a, (i + 1) * a)
        if i == 0:
            t = t * (HEAD_DIM ** -0.5)
        for s in range(a // LANES):
            ref[s] = t[:, s * LANES:(s + 1) * LANES]
    gu = _gelu_tanh(proj(3 * a, 3 * a + GMLP_WIDTH))
    gv = _gelu_tanh(proj(3 * a + GMLP_WIDTH, IN_WIDTH))
    vn = _layer_norm(gv, lng_ref[...], lnb_ref[...]).astype(BF16)

    rows = x_ref.shape[0]
    low_half = lax.broadcasted_iota(I32, (1, LANES), 1) < HEAD_DIM
    for c in range(rows // GMLP_CHUNK):
        r0 = c * GMLP_CHUNK
        for s in range(GMLP_WIDTH // LANES):
            c0 = s * LANES
            vs = vn[r0:r0 + GMLP_CHUNK, c0:c0 + LANES]
            zero = jnp.zeros_like(vs)
            rhs = jnp.concatenate([jnp.where(low_half, vs, zero), jnp.where(low_half, zero, vs)], axis=0)
            vm = jnp.dot(ws_ref[s], rhs, preferred_element_type=F32) + bs_ref[:, c0:c0 + LANES]
            gm_ref[r0:r0 + GMLP_CHUNK, c0:c0 + LANES] = (gu[r0:r0 + GMLP_CHUNK, c0:c0 + LANES] * vm).astype(BF16)


def _mixer_in(x, w_in, b_in, ln_g, ln_b, ws_pairs, bs_full):
    B, L, D = x.shape
    tm = min(ROW_TILE, L)
    row = lambda w: pl.BlockSpec((None, tm, w), lambda b, i: (b, i, 0))
    full = lambda shape: pl.BlockSpec(shape, lambda b, i: (0,) * len(shape))
    n_slab = ATTN_WIDTH // LANES
    slab = pl.BlockSpec((None, n_slab, tm, LANES), lambda b, i: (b, 0, i, 0))
    slab_shape = jax.ShapeDtypeStruct((B, n_slab, L, LANES), F32)
    return pl.pallas_call(
        _mixer_in_kernel,
        grid=(B, L // tm),
        in_specs=[row(D), full(w_in.shape), full(b_in.shape), full(ln_g.shape), full(ln_b.shape),
                  full(ws_pairs.shape), full(bs_full.shape)],
        out_specs=[slab, slab, slab, row(GMLP_WIDTH)],
        out_shape=[slab_shape] * 3 + [jax.ShapeDtypeStruct((B, L, GMLP_WIDTH), BF16)],
        compiler_params=_params("parallel", "parallel"),
        name="mixer_in",
    )(x, w_in, b_in, ln_g, ln_b, ws_pairs, bs_full)


def _attn_kernel(b16_ref, b4_ref, b1_ref, q_ref, k_ref, v_ref, out_ref, o_acc, m_acc, l_acc, *, seq_len):
    span = out_ref.shape[0]
    span_idx = pl.program_id(2)
    n_slab = HG_WIDTH // LANES
    lane = lax.broadcasted_iota(I32, (1, HG_WIDTH), 1)
    head_masks = [(lane >= h * HEAD_DIM) & (lane < (h + 1) * HEAD_DIM) for h in range(HEAD_GROUP)]

    def rows(ref, start, size, stride):
        idx = pl.ds(start, size) if stride == 1 else pl.ds(start, size, stride=stride)
        return jnp.concatenate([ref[s, idx, :] for s in range(n_slab)], axis=1)

    def put(ref, start, stride, val):
        idx = pl.ds(start, Q_TILE) if stride == 1 else pl.ds(start, Q_TILE, stride=stride)
        for s in range(n_slab):
            ref[s, idx, :] = val[:, s * LANES:(s + 1) * LANES]

    def per_head(stacked):
        out = jnp.zeros((Q_TILE, HG_WIDTH), F32)
        for h in range(HEAD_GROUP):
            out = jnp.where(head_masks[h], stacked[h * Q_TILE:(h + 1) * Q_TILE], out)
        return out

    branches = tuple(zip(DILATIONS, (b16_ref, b4_ref, b1_ref)))
    for bi, (d, bias_ref) in enumerate(branches):
        lr = seq_len // d
        tk = min(2 * Q_TILE, lr)
        tiles_total = lr // Q_TILE
        tiles_per_residue = span // d // Q_TILE
        first, final = bi == 0, bi == len(branches) - 1

        def tile(idx, carry, d=d, bias_ref=bias_ref, lr=lr, tk=tk, tiles_total=tiles_total,
                 tiles_per_residue=tiles_per_residue, first=first, final=final):
            r = idx // tiles_per_residue
            jt = idx % tiles_per_residue
            jg = span_idx * tiles_per_residue + jt
            start = jnp.clip(jg * Q_TILE - HALF_WINDOW, 0, lr - tk)
            variant = jnp.where(jg == 0, 0, jnp.where(jg == tiles_total - 1, 2, 1))
            q_row = r + d * (jt * Q_TILE)
            q = rows(q_ref, q_row, Q_TILE, d).astype(BF16)
            kw = rows(k_ref, r + d * start, tk, d).astype(BF16)
            vw = rows(v_ref, r + d * start, tk, d).astype(BF16)
            o_t = m_t = l_t = jnp.zeros((Q_TILE, HG_WIDTH), F32)
            for h0 in range(0, HEAD_GROUP, HEADS_PER_PASS):
                masks = head_masks[h0:h0 + HEADS_PER_PASS]
                qs = jnp.concatenate([jnp.where(hm, q, jnp.zeros_like(q)) for hm in masks], axis=0)
                s = lax.dot_general(qs, kw, (((1,), (1,)), ((), ())), preferred_element_type=F32)
                s = s + bias_ref[variant, pl.ds(h0 * Q_TILE, HEADS_PER_PASS * Q_TILE), :]
                m = jnp.max(s, axis=1, keepdims=True)
                p = jnp.exp(s - m)
                l = jnp.sum(p, axis=1, keepdims=True)
                pv = jnp.dot(p.astype(BF16), vw, preferred_element_type=F32)
                for i, hm in enumerate(masks):
                    part = slice(i * Q_TILE, (i + 1) * Q_TILE)
                    o_t = jnp.where(hm, pv[part], o_t)
                    m_t = jnp.where(hm, m[part], m_t)
                    l_t = jnp.where(hm, l[part], l_t)
            if not first:
                m_old = rows(m_acc, q_row, Q_TILE, d)
                m_new = jnp.maximum(m_old, m_t)
                a_old, a_t = jnp.exp(m_old - m_new), jnp.exp(m_t - m_new)
                l_t = a_old * rows(l_acc, q_row, Q_TILE, d) + a_t * l_t
                o_t = a_old * rows(o_acc, q_row, Q_TILE, d) + a_t * o_t
                m_t = m_new
            if final:
                out_ref[pl.ds(pl.multiple_of(q_row, Q_TILE), Q_TILE), :] = (o_t / l_t).astype(BF16)
            else:
                put(o_acc, q_row, d, o_t)
                put(m_acc, q_row, d, m_t)
                put(l_acc, q_row, d, l_t)
            return carry

        lax.fori_loop(0, span // Q_TILE, tile, 0, unroll=ATTN_UNROLL)


def _t5_bucket(rel):
    half = N_BUCKETS // 2
    ret = np.where(rel > 0, half, 0)
    n = np.abs(rel)
    max_exact = half // 2
    large = max_exact + (np.log(np.maximum(n, 1) / max_exact) / np.log(REL_MAX_DISTANCE / max_exact)
                         * (half - max_exact)).astype(np.int32)
    large = np.minimum(large, half - 1)
    return (ret + np.where(n < max_exact, n, large)).astype(np.int32)


def _bias_tables(rel_table, dilation, tk):
    buckets, valids = [], []
    for delta in (0, -HALF_WINDOW, Q_TILE - tk):
        off = delta + np.arange(tk)[None, :] - np.arange(Q_TILE)[:, None]
        valids.append(np.abs(off) <= HALF_WINDOW)
        buckets.append(_t5_bucket(np.clip(off, -HALF_WINDOW, HALF_WINDOW) * dilation))
    bucket = jnp.asarray(np.stack(buckets), I32)
    onehot = (bucket[..., None] == jnp.arange(N_BUCKETS, dtype=I32)).astype(F32)
    bias = jnp.einsum("vqkb,bh->hvqk", onehot, rel_table, precision=lax.Precision.HIGHEST)
    bias = jnp.where(jnp.asarray(np.stack(valids))[None], bias, MASK_VALUE)
    n_hg = N_HEADS // HEAD_GROUP
    bias = bias.reshape(n_hg, HEAD_GROUP, 3, Q_TILE, tk).transpose(0, 2, 1, 3, 4)
    return bias.reshape(n_hg, 3, HEAD_GROUP * Q_TILE, tk)


def _attn_with_casts_kernel(*refs, seq_len, n_casts):
    n_in = 6 + n_casts
    for src, dst in zip(refs[6:n_in], refs[n_in + 1:n_in + 1 + n_casts]):
        dst[...] = src[...].astype(BF16)
    _attn_kernel(*refs[:6], refs[n_in], *refs[n_in + 1 + n_casts:], seq_len=seq_len)


def _attention(q, k, v, rel_table, casts=()):
    B, _, L, _ = q.shape
    span = min(L, ATTN_SPAN)
    assert L % span == 0 and span % (max(DILATIONS) * Q_TILE) == 0, (L, span)
    n_hg = N_HEADS // HEAD_GROUP
    slabs = HG_WIDTH // LANES
    spans = L // span
    n_steps = n_hg * B * spans
    biases = [_bias_tables(rel_table, d, min(2 * Q_TILE, L // d)) for d in DILATIONS]
    once = pl.Buffered(1)
    bias_spec = lambda t: pl.BlockSpec((None,) + t.shape[1:], lambda g, b, s: (g, 0, 0, 0), pipeline_mode=once)
    seq = pl.BlockSpec((None, slabs, L, LANES), lambda g, b, s: (b, g, 0, 0))
    for t in casts:
        assert t.shape[0] % (n_steps * BF16_SUBLANES) == 0, (t.shape, n_steps)
    cast_specs = [pl.BlockSpec((t.shape[0] // n_steps, t.shape[1]), lambda g, b, s: ((g * B + b) * spans + s, 0))
                  for t in casts]
    outs = pl.pallas_call(
        functools.partial(_attn_with_casts_kernel, seq_len=L, n_casts=len(casts)),
        grid=(n_hg, B, spans),
        in_specs=[bias_spec(t) for t in biases]
                 + [pl.BlockSpec((None, slabs, span, LANES), lambda g, b, s: (b, g, s, 0)), seq, seq] + cast_specs,
        out_specs=[pl.BlockSpec((None, span, HG_WIDTH), lambda g, b, s: (b, s, g))] + cast_specs,
        out_shape=[jax.ShapeDtypeStruct((B, L, ATTN_WIDTH), BF16)]
                  + [jax.ShapeDtypeStruct(t.shape, BF16) for t in casts],
        scratch_shapes=[pltpu.VMEM((slabs, span, LANES), F32)] * 3,
        compiler_params=_params("parallel", "parallel", "arbitrary"),
        name="attention",
    )(*biases, q, k, v, *casts)
    return outs[0], tuple(outs[1:])


def _mixer_out_kernel(x_ref, attn_ref, gm_ref, wo_ref, g_ref, b_ref, wrh_ref, wrl_ref,
                      x1_ref, x1p_ref, aff_ref):
    mix = jnp.dot(attn_ref[...], wo_ref[0:ATTN_WIDTH, :], preferred_element_type=F32)
    mix = mix + jnp.dot(gm_ref[...], wo_ref[ATTN_WIDTH:, :], preferred_element_type=F32)
    x1 = _layer_norm(ALPHA * x_ref[...] + mix, g_ref[...], b_ref[...])
    x1_ref[...] = x1
    hi = x1.astype(BF16)
    x1p_ref[...] = _pack_bf16_pairs(x1)
    lo = (x1 - hi.astype(F32)).astype(BF16)
    nt = (((1,), (1,)), ((), ()))
    logits = (lax.dot_general(wrh_ref[...], hi, nt, preferred_element_type=F32)
              + lax.dot_general(wrl_ref[...], hi, nt, preferred_element_type=F32)
              + lax.dot_general(wrh_ref[...], lo, nt, preferred_element_type=F32))
    m = jnp.max(logits, axis=0, keepdims=True)
    e = jnp.exp(logits - m)
    aff_ref[...] = e / jnp.sum(e, axis=0, keepdims=True)


def _mixer_out(x, attn, gm, w_out, ln_g, ln_b, wr_hi, wr_lo):
    B, L, D = x.shape
    tm = min(ROW_TILE, L)
    per_seq = L // tm
    row = lambda w: pl.BlockSpec((None, tm, w), lambda b, i: (b, i, 0))
    full = lambda shape: pl.BlockSpec(shape, lambda b, i: (0,) * len(shape))
    return pl.pallas_call(
        _mixer_out_kernel,
        grid=(B, per_seq),
        in_specs=[row(D), row(ATTN_WIDTH), row(GMLP_WIDTH)]
                 + [full(w_out.shape), full(ln_g.shape), full(ln_b.shape), full(wr_hi.shape), full(wr_lo.shape)],
        out_specs=[row(D), row(D // 2), pl.BlockSpec((N_EXPERTS, tm), lambda b, i: (0, b * per_seq + i))],
        out_shape=[jax.ShapeDtypeStruct((B, L, D), F32), jax.ShapeDtypeStruct((B, L, D // 2), I32),
                   jax.ShapeDtypeStruct((N_EXPERTS, B * L), F32)],
        compiler_params=_params("parallel", "parallel"),
        name="mixer_out",
    )(x, attn, gm, w_out, ln_g, ln_b, wr_hi, wr_lo)


def _threshold_kernel(aff_ref, tau_ref, need_ref, *, cap):
    bits = lax.bitcast_convert_type(aff_ref[...], I32)

    def step(i, tau):
        cand = tau | jnp.left_shift(jnp.int32(1), 30 - i)
        cnt = jnp.sum((bits >= cand).astype(F32), axis=1, keepdims=True)
        return jnp.where(cnt >= cap, cand, tau)

    tau = lax.fori_loop(0, 31, step, jnp.zeros((bits.shape[0], 1), I32))
    above = jnp.sum((bits > tau).astype(F32), axis=1, keepdims=True)
    tau_ref[...] = jnp.broadcast_to(tau, tau_ref.shape)
    need_ref[...] = jnp.broadcast_to(cap - above, need_ref.shape)


def _slot_kernel(aff_ref, tau_ref, need_ref, tri_ref, low_ref, slot_ref, rank_ref, span_ref, carry_sel, carry_eq):
    @pl.when(pl.program_id(0) == 0)
    def _():
        carry_sel[...] = jnp.zeros_like(carry_sel)
        carry_eq[...] = jnp.zeros_like(carry_eq)

    tau = tau_ref[:, 0:1]
    need = need_ref[:, 0:1]
    c_sel = carry_sel[:, 0:1]
    c_eq = carry_eq[:, 0:1]
    tri = tri_ref[...]
    for s in range(aff_ref.shape[1] // SCAN_TILE):
        sl = slice(s * SCAN_TILE, (s + 1) * SCAN_TILE)
        bits = lax.bitcast_convert_type(aff_ref[:, sl], I32)
        eq = (bits == tau).astype(F32)
        eq_incl = jnp.dot(eq.astype(BF16), tri, preferred_element_type=F32)
        tie_taken = (eq_incl - eq + c_eq) < need
        sel = jnp.where((bits > tau) | ((bits == tau) & tie_taken), 1.0, 0.0)
        sel_b = sel.astype(BF16)
        incl = jnp.dot(sel_b, tri, preferred_element_type=F32)
        before = (incl - sel + c_sel).astype(I32)
        slot_ref[:, sl] = jnp.where(sel > 0.0, before, -1)
        per_token = jnp.sum(sel, axis=0, keepdims=True)
        first = jnp.sum(incl, axis=0, keepdims=True) - per_token + jnp.sum(c_sel, axis=0, keepdims=True)
        lower = jnp.dot(low_ref[...], sel_b, preferred_element_type=F32)
        rank_ref[:, sl] = (first + lower).astype(I32)
        span_ref[0:1, sl] = first
        span_ref[1:2, sl] = first + per_token
        c_sel = c_sel + incl[:, SCAN_TILE - 1:SCAN_TILE]
        c_eq = c_eq + eq_incl[:, SCAN_TILE - 1:SCAN_TILE]
    carry_sel[...] = jnp.broadcast_to(c_sel, carry_sel.shape)
    carry_eq[...] = jnp.broadcast_to(c_eq, carry_eq.shape)


def _select(aff_t, cap):
    E, n = aff_t.shape
    stat = jax.ShapeDtypeStruct((E, LANES), I32)
    tau, need = pl.pallas_call(
        functools.partial(_threshold_kernel, cap=float(cap)),
        out_shape=[stat, jax.ShapeDtypeStruct((E, LANES), F32)],
        compiler_params=pltpu.CompilerParams(vmem_limit_bytes=VMEM_LIMIT),
        name="threshold",
    )(aff_t)
    tri = jnp.asarray(np.triu(np.ones((SCAN_TILE, SCAN_TILE), np.float32)), BF16)
    low = jnp.asarray(np.tril(np.ones((E, E), np.float32), -1), BF16)
    tb = min(SLOT_BLOCK, n)
    blk = pl.BlockSpec((E, tb), lambda i: (0, i))
    const = lambda shape: pl.BlockSpec(shape, lambda i: (0, 0))
    return pl.pallas_call(
        _slot_kernel,
        grid=(n // tb,),
        in_specs=[blk, const((E, LANES)), const((E, LANES)), const((SCAN_TILE, SCAN_TILE)), const((E, E))],
        out_specs=[blk, blk, pl.BlockSpec((2, tb), lambda i: (0, i))],
        out_shape=[jax.ShapeDtypeStruct((E, n), I32)] * 2 + [jax.ShapeDtypeStruct((2, n), F32)],
        scratch_shapes=[pltpu.VMEM((E, LANES), F32), pltpu.VMEM((E, LANES), F32)],
        compiler_params=_params("arbitrary"),
        name="slots",
    )(aff_t, tau, need, tri, low)


def _work_list(span, total_rows, t_chunk, t_rows):
    n = span.shape[1]
    nch, ntl = n // t_chunk, total_rows // t_rows
    kmax = N_EXPERTS * t_chunk // t_rows + 1
    base = span[0, ::t_chunk].astype(I32)
    end = jnp.concatenate([base[1:], jnp.full((1,), total_rows, I32)])
    first = jnp.minimum(base // t_rows, ntl - 1)
    last = jnp.where(end > base, (end - 1) // t_rows, first)
    tile = first[:, None] + jnp.arange(kmax, dtype=I32)
    valid = (tile <= last[:, None]).reshape(-1)
    tile = jnp.minimum(tile, ntl - 1).reshape(-1)
    chunk = jnp.repeat(jnp.arange(nch, dtype=I32), kmax)
    length = nch + ntl
    count = jnp.sum(valid.astype(I32))
    idx = jnp.nonzero(valid, size=length, fill_value=0)[0].astype(I32)
    ar = jnp.arange(length, dtype=I32)
    live = ar < count
    idx = jnp.where(live, idx, idx[count - 1])
    c, t = chunk[idx], tile[idx]
    prev = jnp.concatenate([jnp.full((1,), -1, I32), c[:-1]])
    nxt = jnp.concatenate([c[1:], jnp.full((1,), -1, I32)])
    is_first = live & ((ar == 0) | (c != prev))
    is_last = live & ((ar == count - 1) | (c != nxt))
    return c, t, live.astype(I32) + 2 * is_first.astype(I32) + 4 * is_last.astype(I32)


def _pack_bf16_pairs(x):
    bits = lax.bitcast_convert_type(x.astype(BF16).astype(F32), I32)
    half = bits.shape[1] // 2
    return lax.shift_right_logical(bits[:, :half], BF16_BITS) | bits[:, half:]


def _unpack_bf16_pairs(w):
    left = lax.bitcast_convert_type(lax.shift_left(w, BF16_BITS), F32)
    right = lax.bitcast_convert_type(w & jnp.int32(-(1 << BF16_BITS)), F32)
    return jnp.concatenate([left, right], axis=1).astype(BF16)


def _sc_layout(n_experts, cap):
    info = plsc.get_sparse_core_info()
    workers = info.num_cores * info.num_subcores
    per_expert = workers // n_experts
    assert per_expert * n_experts == workers and cap % (per_expert * SC_ROWS) == 0, (workers, n_experts, cap)
    return info.num_cores, info.num_lanes, per_expert, cap // per_expert


def _sc_worker(n_cores, per_expert):
    wid = lax.axis_index("s") * n_cores + lax.axis_index("c")
    return wid // per_expert, wid % per_expert


def _sc_invert(slot_hbm, payload_hbms, slot_v, payload_vs, e, part, share, n, chunk, lanes, store):
    @pl.loop(0, n // chunk)
    def _(ci):
        pltpu.sync_copy(slot_hbm.at[e, pl.ds(ci * chunk, chunk)], slot_v)
        for src, dst in zip(payload_hbms, payload_vs):
            pltpu.sync_copy(src.at[e, pl.ds(ci * chunk, chunk)], dst)

        @pl.loop(0, chunk // lanes)
        def _(i):
            local = slot_v[pl.ds(i * lanes, lanes)] - part * share
            mine = (local >= 0) & (local < share)
            tok = ci * chunk + i * lanes + lax.iota(I32, lanes)
            store(local, mine, tok, [v[pl.ds(i * lanes, lanes)] for v in payload_vs])


_SC_PARAMS = dataclasses.replace(pltpu.CompilerParams(), needs_layout_passes=False)


def _dispatch(x1p, slot, aff_t, cap):
    n, width = x1p.shape
    E = slot.shape[0]
    n_cores, lanes, per_expert, share = _sc_layout(E, cap)
    chunk = min(n, SC_SLOT_CHUNK)
    mesh = plsc.VectorSubcoreMesh(core_axis_name="c", subcore_axis_name="s")

    @functools.partial(
        pl.kernel, mesh=mesh, compiler_params=_SC_PARAMS,
        out_type=[jax.ShapeDtypeStruct((E * cap, width), I32), jax.ShapeDtypeStruct((E * cap,), F32)],
        scratch_types=[pltpu.VMEM((chunk,), I32), pltpu.VMEM((chunk,), F32), pltpu.VMEM((share,), I32),
                       pltpu.VMEM((share,), F32), pltpu.VMEM((SC_ROWS, width), I32), pltpu.SemaphoreType.DMA],
    )
    def gather(x_hbm, slot_hbm, aff_hbm, out_hbm, gate_hbm, slot_v, aff_v, idx_v, gate_v, rows_v, sem):
        e, part = _sc_worker(n_cores, per_expert)

        def store(local, mine, tok, payloads):
            plsc.store_scatter(idx_v, [local], tok, mask=mine)
            plsc.store_scatter(gate_v, [local], payloads[0], mask=mine)

        _sc_invert(slot_hbm, [aff_hbm], slot_v, [aff_v], e, part, share, n, chunk, lanes, store)
        base = e * cap + part * share
        pltpu.sync_copy(gate_v, gate_hbm.at[pl.ds(base, share)])

        @pl.loop(0, share // SC_ROWS)
        def _(j):
            pltpu.async_copy(x_hbm.at[idx_v.at[pl.ds(j * SC_ROWS, SC_ROWS)]], rows_v, sem).wait()
            pltpu.sync_copy(rows_v, out_hbm.at[pl.ds(base + j * SC_ROWS, SC_ROWS)])

    xs, gates = gather(x1p, slot, aff_t)
    return xs.reshape(E, cap, width), gates.reshape(E, cap // LANES, LANES)


def _to_token_order(ysp, slot, rank):
    E, cap, width = ysp.shape
    n = slot.shape[1]
    n_cores, lanes, per_expert, share = _sc_layout(E, cap)
    chunk = min(n, SC_SLOT_CHUNK)
    mesh = plsc.VectorSubcoreMesh(core_axis_name="c", subcore_axis_name="s")

    @functools.partial(
        pl.kernel, mesh=mesh, compiler_params=_SC_PARAMS,
        out_type=jax.ShapeDtypeStruct((E * cap, width), I32),
        scratch_types=[pltpu.VMEM((chunk,), I32), pltpu.VMEM((chunk,), I32),
                       pltpu.VMEM((share // SC_ROWS, SC_ROWS), I32),
                       pltpu.VMEM((SC_ROWS, width), I32), pltpu.SemaphoreType.DMA],
    )
    def scatter(y_hbm, slot_hbm, rank_hbm, z_hbm, slot_v, rank_v, dest_v, rows_v, sem):
        e, part = _sc_worker(n_cores, per_expert)

        def store(local, mine, tok, payloads):
            plsc.store_scatter(dest_v, [local // SC_ROWS, local % SC_ROWS], payloads[0], mask=mine)

        _sc_invert(slot_hbm, [rank_hbm], slot_v, [rank_v], e, part, share, n, chunk, lanes, store)
        base = e * cap + part * share

        @pl.loop(0, share // SC_ROWS)
        def _(j):
            pltpu.sync_copy(y_hbm.at[pl.ds(base + j * SC_ROWS, SC_ROWS)], rows_v)
            pltpu.async_copy(rows_v, z_hbm.at[dest_v.at[j]], sem).wait()

    return scatter(ysp.reshape(E * cap, width), slot, rank)


def _ffn_kernel(x_ref, gate_ref, wg_ref, wu_ref, wd_ref, y_ref, acc_ref):
    x = _unpack_bf16_pairs(x_ref[...])
    for c in range(D_FF // FFN_COLS):
        cols = slice(c * FFN_COLS, (c + 1) * FFN_COLS)
        g = jnp.dot(x, wg_ref[:, cols], preferred_element_type=F32)
        u = jnp.dot(x, wu_ref[:, cols], preferred_element_type=F32)
        h = (g * jax.nn.sigmoid(g) * u).astype(BF16)
        part = jnp.dot(h, wd_ref[cols, :], preferred_element_type=F32)
        if c == 0:
            acc_ref[...] = part
        else:
            acc_ref[...] += part

    gates = gate_ref[...]
    pad = jnp.zeros((LANES - gates.shape[0], LANES), F32)
    scale = jnp.concatenate([gates, pad], axis=0).T
    for j in range(gates.shape[0]):
        rows = slice(j * LANES, (j + 1) * LANES)
        y_ref[rows, :] = _pack_bf16_pairs(acc_ref[rows, :] * scale[:, j:j + 1])


def _experts(xs, gates, w_gate, w_up, w_down):
    E, cap, packed = xs.shape
    D = 2 * packed
    tm = min(FFN_ROWS, cap)
    weight = lambda shape: pl.BlockSpec((None,) + shape, lambda e, m: (e, 0, 0))
    return pl.pallas_call(
        _ffn_kernel,
        grid=(E, cap // tm),
        in_specs=[pl.BlockSpec((None, tm, packed), lambda e, m: (e, m, 0)),
                  pl.BlockSpec((None, tm // LANES, LANES), lambda e, m: (e, m, 0)),
                  weight((D, D_FF)), weight((D, D_FF)), weight((D_FF, D))],
        out_specs=pl.BlockSpec((None, tm, packed), lambda e, m: (e, m, 0)),
        out_shape=jax.ShapeDtypeStruct((E, cap, packed), I32),
        scratch_shapes=[pltpu.VMEM((tm, D), F32)],
        compiler_params=_params("parallel", "arbitrary"),
        name="experts",
    )(xs, gates, w_gate, w_up, w_down)


def _combine_kernel(wc_ref, wt_ref, wf_ref, z_ref, span_ref, x1_ref, g_ref, b_ref, o_ref):
    w = pl.program_id(0)
    flags = wf_ref[w]

    @pl.when((flags & 2) == 2)
    def _():
        o_ref[...] = jnp.zeros_like(o_ref)

    @pl.when((flags & 1) == 1)
    def _():
        tr, tc = z_ref.shape[0], o_ref.shape[0]
        row = (lax.broadcasted_iota(I32, (tc, tr), 1) + wt_ref[w] * tr).astype(F32)
        owns = ((row >= span_ref[:, 0:1]) & (row < span_ref[:, 1:2])).astype(BF16)
        o_ref[...] += jnp.dot(owns, _unpack_bf16_pairs(z_ref[...]), preferred_element_type=F32)

    @pl.when((flags & 4) == 4)
    def _():
        o_ref[...] = _layer_norm(ALPHA * x1_ref[...] + o_ref[...], g_ref[...], b_ref[...])


def _combine(z, span_tm, x1, ln_g, ln_b, work):
    rows, packed = z.shape
    n, D = x1.shape
    length = work[0].shape[0]
    chunk = lambda w, wc, wt, wf: (wc[w], 0)
    const = lambda w, wc, wt, wf: (0, 0)
    return pl.pallas_call(
        _combine_kernel,
        grid_spec=pltpu.PrefetchScalarGridSpec(
            num_scalar_prefetch=3,
            grid=(length,),
            in_specs=[pl.BlockSpec((COMBINE_ROWS, packed), lambda w, wc, wt, wf: (wt[w], 0)),
                      pl.BlockSpec((COMBINE_CHUNK, 2), chunk),
                      pl.BlockSpec((COMBINE_CHUNK, D), chunk),
                      pl.BlockSpec((1, D), const), pl.BlockSpec((1, D), const)],
            out_specs=pl.BlockSpec((COMBINE_CHUNK, D), chunk),
        ),
        out_shape=jax.ShapeDtypeStruct((n, D), F32),
        compiler_params=_params("arbitrary"),
        name="combine",
    )(*work, z, span_tm, x1, ln_g, ln_b)


def _prepare_weights(rel_bias_table, w_in, b_in, gmlp_ln_g, gmlp_ln_b, gmlp_w_s, gmlp_b_s, w_out,
                     ln1_g, ln1_b, w_router, ln2_g, ln2_b):
    row = lambda t: t[0].reshape(1, -1).astype(F32)
    ws = gmlp_w_s[0].astype(BF16)
    ws_pairs = jnp.concatenate([ws[0::2], ws[1::2]], axis=-1)
    bs_full = jnp.repeat(gmlp_b_s[0].T, HEAD_DIM, axis=1).astype(F32)
    wr_t = w_router[0].T.astype(F32)
    wr_hi = wr_t.astype(BF16)
    wr_lo = (wr_t - wr_hi.astype(F32)).astype(BF16)
    return dict(
        rel=rel_bias_table.astype(F32), w_in=w_in[0].astype(BF16), b_in=row(b_in),
        gln_g=row(gmlp_ln_g), gln_b=row(gmlp_ln_b), ws_pairs=ws_pairs, bs_full=bs_full,
        w_out=w_out[0].astype(BF16), ln1_g=row(ln1_g), ln1_b=row(ln1_b), wr_hi=wr_hi, wr_lo=wr_lo,
        ln2_g=row(ln2_g), ln2_b=row(ln2_b))


def _trunk(x, p, expert_weights):
    B, L, D = x.shape
    n = B * L
    cap = CAPACITY_FACTOR * n // N_EXPERTS
    q, k, v, gm = _mixer_in(x, p["w_in"], p["b_in"], p["gln_g"], p["gln_b"], p["ws_pairs"], p["bs_full"])
    if expert_weights[0].dtype == BF16:
        attn, _ = _attention(q, k, v, p["rel"])
    else:
        attn, flat = _attention(q, k, v, p["rel"], [w.reshape(-1, w.shape[-1]) for w in expert_weights])
        expert_weights = tuple(f.reshape(w.shape) for f, w in zip(flat, expert_weights))
    x1, x1p, aff_t = _mixer_out(x, attn, gm, p["w_out"], p["ln1_g"], p["ln1_b"], p["wr_hi"], p["wr_lo"])
    slot, rank, span = _select(aff_t, cap)
    xs, gates = _dispatch(x1p.reshape(n, D // 2), slot, aff_t, cap)
    ys = _experts(xs, gates, *expert_weights)
    z = _to_token_order(ys, slot, rank)
    work = _work_list(span, N_EXPERTS * cap, COMBINE_CHUNK, COMBINE_ROWS)
    y = _combine(z, span.T, x1.reshape(n, D), p["ln2_g"], p["ln2_b"], work)
    return y.reshape(B, L, D), expert_weights


def kernel(x_prompt, x_sample, rel_bias_table, w_in, b_in, gmlp_ln_g, gmlp_ln_b, gmlp_w_s, gmlp_b_s, w_out,
           ln1_g, ln1_b, w_router, w_gate, w_up, w_down, ln2_g, ln2_b):
    p = _prepare_weights(rel_bias_table, w_in, b_in, gmlp_ln_g, gmlp_ln_b, gmlp_w_s, gmlp_b_s, w_out,
                         ln1_g, ln1_b, w_router, ln2_g, ln2_b)
    y_prompt, expert_weights = _trunk(x_prompt, p, (w_gate[0], w_up[0], w_down[0]))
    y_sample, _ = _trunk(x_sample, p, expert_weights)
    return (y_prompt, y_sample)
```
